```python
import math
import jax, jax.numpy as jnp
from jax import lax
import numpy as np


D_MODEL = 1024
BATCH = 2
SEQ = 16384
DEPTH = 4

GRID_W = 64
CTX_LEN = 256
N_MIXERS = 4
D_MIX = D_MODEL
GROUP_W = D_MIX // N_MIXERS
POOL_WINDOWS = (2, 4, 8, 16)
POOL_GROUPS = 4
POOL_GW = GROUP_W // POOL_GROUPS
FOURIER_GROUPS = 4
FOURIER_GW = GROUP_W // FOURIER_GROUPS
CONV_K = 31
GDN_HEAD_DIM = 64
GDN_HEADS = GROUP_W // GDN_HEAD_DIM
GDN_CONV = 3
GDN_CHUNK = 64
FFN_HIDDEN = 128 * ((8 * D_MODEL // 3 + 127) // 128)
N_MOD = 9
EPS = 1e-6
POOL_OFF = 0
FOURIER_OFF = POOL_OFF + GROUP_W
CONV_OFF = FOURIER_OFF + GROUP_W
GDN_OFF = CONV_OFF + 2 * GROUP_W
GDN_PROJ = 4 * GROUP_W + 4 * GDN_HEADS
PROJ_W = GDN_OFF + GDN_PROJ

kernel_name = 'hybrid_parallel_group_dit_block'


def _rms(x, w):
    xf = x.astype(jnp.float32)
    y = xf * lax.rsqrt(jnp.mean(xf * xf, axis=-1, keepdims=True) + EPS)
    return (y * w.astype(jnp.float32)).astype(x.dtype)


def _modulate(n, shift, scale):
    return n * (1 + scale) + shift


def _swiglu(n, wg, wu, wd):
    return (jax.nn.silu(n @ wg) * (n @ wu)) @ wd


def _dwconv(x, w, pad):
    ch = x.shape[-1]
    filt = w[:, None, :].astype(x.dtype)
    return lax.conv_general_dilated(x, filt, (1,), [(pad, pad)],
                                    dimension_numbers=('NWC', 'WIO', 'NWC'),
                                    feature_group_count=ch)


def _window_bounds(n, w):
    t = jnp.arange(n)
    return jnp.clip(t - w // 2, 0, n), jnp.clip(t + w - w // 2, 0, n)


def _pool_means_seq(u):
    n = u.shape[1]
    p = jnp.pad(jnp.cumsum(u.astype(jnp.float32), axis=1), ((0, 0), (1, 0), (0, 0)))
    outs = []
    for gi, w in enumerate(POOL_WINDOWS):
        pg = p[..., gi * POOL_GW:(gi + 1) * POOL_GW]
        lo, hi = _window_bounds(n, w)
        cnt = (hi - lo).astype(jnp.float32)
        outs.append((pg[:, hi] - pg[:, lo]) / cnt[None, :, None])
    return jnp.concatenate(outs, axis=-1)


def _pool_means_grid(u, rows):
    bsz, n, ch = u.shape
    uf = u.astype(jnp.float32).reshape(bsz, rows, GRID_W, ch)
    p = jnp.pad(jnp.cumsum(jnp.cumsum(uf, axis=1), axis=2), ((0, 0), (1, 0), (1, 0), (0, 0)))
    outs = []
    for gi, w in enumerate(POOL_WINDOWS):
        pg = p[..., gi * POOL_GW:(gi + 1) * POOL_GW]
        r0, r1 = _window_bounds(rows, w)
        c0, c1 = _window_bounds(GRID_W, w)
        s = (pg[:, r1][:, :, c1] - pg[:, r0][:, :, c1]
             - pg[:, r1][:, :, c0] + pg[:, r0][:, :, c0])
        cnt = ((r1 - r0)[:, None] * (c1 - c0)[None, :]).astype(jnp.float32)
        outs.append(s / cnt[None, :, :, None])
    return jnp.concatenate(outs, axis=-1).reshape(bsz, n, ch)


def _pool_mixer(u, mean, pool_w, pool_scale):
    bsz, n, _ = u.shape
    d = (mean - u.astype(jnp.float32)).astype(u.dtype).reshape(bsz, n, POOL_GROUPS, POOL_GW)
    y = jnp.einsum('bngc,gcd->bngd', d, pool_w).reshape(bsz, n, GROUP_W)
    return y * pool_scale


def _fourier_mixer(u, fourier_w):
    bsz, n, _ = u.shape
    uf = u.astype(jnp.float32).reshape(bsz, n, FOURIER_GROUPS, FOURIER_GW)
    f = jnp.fft.fft2(uf, axes=(1, 3), norm='ortho').real.reshape(bsz, n, GROUP_W)
    return f.astype(u.dtype) @ fourier_w


def _conformer_conv(u2, dw_w, dw_b, ln_g, ln_b, pw_w):
    a, b = jnp.split(u2, 2, axis=-1)
    h = _dwconv(a * jax.nn.sigmoid(b), dw_w, CONV_K // 2) + dw_b
    hf = h.astype(jnp.float32)
    mu = jnp.mean(hf, axis=-1, keepdims=True)
    var = jnp.mean(jnp.square(hf - mu), axis=-1, keepdims=True)
    h = ((hf - mu) * lax.rsqrt(var + EPS) * ln_g.astype(jnp.float32) + ln_b.astype(jnp.float32)).astype(u2.dtype)
    return jax.nn.silu(h) @ pw_w


def _l2norm(t):
    return t * lax.rsqrt(jnp.sum(t * t, axis=-1, keepdims=True) + EPS)


def _gated_delta_chunked(q, k, v, g, beta, s0):
    bsz, n, h, _ = q.shape
    dv = v.shape[-1]
    nc = n // GDN_CHUNK

    def blk(t):
        t = t.reshape((bsz, nc, GDN_CHUNK, h) + t.shape[3:])
        return jnp.moveaxis(t, (1, 3), (0, 2))

    qb, kb, vb, gb, bb = blk(q), blk(k), blk(v), blk(g), blk(beta)
    gc = jnp.cumsum(gb, axis=-1)
    idx = jnp.arange(GDN_CHUNK)
    incl = idx[:, None] >= idx[None, :]
    strict = idx[:, None] > idx[None, :]
    gam = jnp.exp(jnp.where(incl, gc[..., :, None] - gc[..., None, :], -jnp.inf))
    kbeta = kb * bb[..., None]
    m = jnp.where(strict, jnp.einsum('nbhik,nbhjk->nbhij', kbeta, kb) * gam, 0.0)
    a = m + jnp.eye(GDN_CHUNK, dtype=m.dtype)
    rhs = jnp.concatenate([vb * bb[..., None], kbeta * jnp.exp(gc)[..., None]], axis=-1)
    sol = lax.linalg.triangular_solve(a, rhs, left_side=True, lower=True, unit_diagonal=True)
    u, w = sol[..., :dv], sol[..., dv:]
    aqk = jnp.einsum('nbhik,nbhjk->nbhij', qb, kb) * gam
    qdec = qb * jnp.exp(gc)[..., None]
    glast = gc[..., -1]
    kdec = kb * jnp.exp(glast[..., None] - gc)[..., None]

    def step(s, xs):
        u_i, w_i, q_i, k_i, a_i, gl_i = xs
        v_new = u_i - jnp.einsum('bhck,bhkv->bhcv', w_i, s)
        o_i = jnp.einsum('bhck,bhkv->bhcv', q_i, s) + jnp.einsum('bhij,bhjv->bhiv', a_i, v_new)
        s = s * jnp.exp(gl_i)[..., None, None] + jnp.einsum('bhck,bhcv->bhkv', k_i, v_new)
        return s, o_i

    s_fin, o = lax.scan(step, s0, (u, w, qdec, kdec, aqk, glast))
    o = jnp.moveaxis(o, (0, 2), (1, 3)).reshape(bsz, n, h, dv)
    return o, s_fin


def _gdn_inputs(p, conv_w, a_log, dt_bias):
    bsz, n, _ = p.shape
    qkv = jax.nn.silu(_dwconv(p[..., :3 * GROUP_W], conv_w, GDN_CONV // 2)).astype(jnp.float32)
    q, k, v = jnp.split(qkv.reshape(bsz, n, 3, GDN_HEADS, GDN_HEAD_DIM), 3, axis=2)
    q = _l2norm(q[:, :, 0]) * (GDN_HEAD_DIM ** -0.5)
    k = _l2norm(k[:, :, 0])
    v = v[:, :, 0]
    z = p[..., 3 * GROUP_W:4 * GROUP_W]
    off = 4 * GROUP_W
    a = p[..., off:off + 2 * GDN_HEADS].astype(jnp.float32).reshape(bsz, n, 2, GDN_HEADS)
    b = p[..., off + 2 * GDN_HEADS:off + 4 * GDN_HEADS].astype(jnp.float32).reshape(bsz, n, 2, GDN_HEADS)
    g = -jnp.exp(a_log.astype(jnp.float32)) * jax.nn.softplus(a + dt_bias.astype(jnp.float32))
    return q, k, v, z, g, jax.nn.sigmoid(b)


def _gated_out(o, z, w):
    zf = z.astype(jnp.float32).reshape(o.shape)
    y = o * lax.rsqrt(jnp.mean(o * o, axis=-1, keepdims=True) + EPS) * w.astype(jnp.float32) * jax.nn.silu(zf)
    return y.reshape(o.shape[0], o.shape[1], -1).astype(z.dtype)


def _gdn_mixer(p_x, p_c, need_ctx, conv_w, a_log, dt_bias, norm_w):
    qc, kc, vc, zc, gc, bc = _gdn_inputs(p_c, conv_w, a_log, dt_bias)
    qx, kx, vx, zx, gx, bx = _gdn_inputs(p_x, conv_w, a_log, dt_bias)
    s0 = jnp.zeros((p_x.shape[0], GDN_HEADS, GDN_HEAD_DIM, GDN_HEAD_DIM), jnp.float32)
    fl = lambda t: jnp.flip(t, axis=1)
    oc_f, sc_f = _gated_delta_chunked(qc, kc, vc, gc[:, :, 0], bc[:, :, 0], s0)
    oc_b, sc_b = _gated_delta_chunked(fl(qc), fl(kc), fl(vc), fl(gc[:, :, 1]), fl(bc[:, :, 1]), s0)
    ox_f, _ = _gated_delta_chunked(qx, kx, vx, gx[:, :, 0], bx[:, :, 0], sc_f)
    ox_b, _ = _gated_delta_chunked(fl(qx), fl(kx), fl(vx), fl(gx[:, :, 1]), fl(bx[:, :, 1]), sc_b)
    yx = _gated_out(ox_f + fl(ox_b), zx, norm_w)
    yc = _gated_out(oc_f + fl(oc_b), zc, norm_w) if need_ctx else None
    return yx, yc


def _token_mixer(nx, nc, rows, need_ctx, w_in, w_out, pool_w, pool_scale, fourier_w,
                 conv_dw_w, conv_dw_b, conv_ln_g, conv_ln_b, conv_pw_w,
                 gdn_conv_w, gdn_a_log, gdn_dt_bias, gdn_norm_w):
    px = nx @ w_in
    pc = nc @ (w_in if need_ctx else w_in[:, GDN_OFF:])
    pc_gdn = pc[..., GDN_OFF:] if need_ctx else pc
    gdn_x, gdn_c = _gdn_mixer(px[..., GDN_OFF:], pc_gdn, need_ctx, gdn_conv_w, gdn_a_log, gdn_dt_bias, gdn_norm_w)

    def local_groups(p, pool_mean):
        u_pool = p[..., POOL_OFF:FOURIER_OFF]
        return [_pool_mixer(u_pool, pool_mean(u_pool), pool_w, pool_scale),
                _fourier_mixer(p[..., FOURIER_OFF:CONV_OFF], fourier_w),
                _conformer_conv(p[..., CONV_OFF:GDN_OFF], conv_dw_w, conv_dw_b, conv_ln_g, conv_ln_b, conv_pw_w)]

    yx = jnp.concatenate(local_groups(px, lambda u: _pool_means_grid(u, rows)) + [gdn_x], axis=-1) @ w_out
    yc = (jnp.concatenate(local_groups(pc, _pool_means_seq) + [gdn_c], axis=-1) @ w_out) if need_ctx else None
    return yx, yc


def setup_inputs(seed: int = 0) -> dict:
    key = jax.random.key(seed)
    ks = jax.random.split(key, 32)
    f32 = jnp.float32
    L, D, F, G, H = DEPTH, D_MODEL, FFN_HIDDEN, GROUP_W, GDN_HEADS

    def nrm(k, shape, scale):
        return scale * jax.random.normal(k, shape, f32)

    dt = jnp.exp(jax.random.uniform(ks[26], (L, 2, H), f32, math.log(1e-3), math.log(1e-1)))
    return {
        'x': nrm(ks[0], (BATCH, SEQ, D), 1.0),
        'c': nrm(ks[1], (BATCH, D), 1.0),
        'ctx': nrm(ks[2], (BATCH, CTX_LEN, D), 1.0),
        'c_ctx': nrm(ks[3], (D,), 1.0),
        'mod_w': nrm(ks[4], (L, D, N_MOD * D), 0.5 * D ** -0.5),
        'mod_b': nrm(ks[5], (L, N_MOD * D), 0.02),
        'norm_w': 1.0 + nrm(ks[6], (L, 3, D), 0.02),
        'ffn1_wg': nrm(ks[7], (L, D, F), D ** -0.5),
        'ffn1_wu': nrm(ks[8], (L, D, F), D ** -0.5),
        'ffn1_wd': nrm(ks[9], (L, F, D), F ** -0.5),
        'ffn2_wg': nrm(ks[10], (L, D, F), D ** -0.5),
        'ffn2_wu': nrm(ks[11], (L, D, F), D ** -0.5),
        'ffn2_wd': nrm(ks[12], (L, F, D), F ** -0.5),
        'w_in': nrm(ks[13], (L, D, PROJ_W), D ** -0.5),
        'w_out': nrm(ks[14], (L, D_MIX, D), D_MIX ** -0.5),
        'pool_w': nrm(ks[15], (L, POOL_GROUPS, POOL_GW, POOL_GW), POOL_GW ** -0.5),
        'pool_scale': 1.0 + nrm(ks[16], (L, G), 0.1),
        'fourier_w': nrm(ks[17], (L, G, G), G ** -0.5),
        'conv_dw_w': nrm(ks[18], (L, CONV_K, G), CONV_K ** -0.5),
        'conv_dw_b': nrm(ks[19], (L, G), 0.02),
        'conv_ln_g': 1.0 + nrm(ks[20], (L, G), 0.02),
        'conv_ln_b': nrm(ks[21], (L, G), 0.02),
        'conv_pw_w': nrm(ks[22], (L, G, G), G ** -0.5),
        'gdn_conv_w': nrm(ks[23], (L, GDN_CONV, 3 * G), GDN_CONV ** -0.5),
        'gdn_a_log': jnp.log(jax.random.uniform(ks[24], (L, 2, H), f32, 1.0, 16.0)),
        'gdn_dt_bias': dt + jnp.log(-jnp.expm1(-dt)),
        'gdn_norm_w': 1.0 + nrm(ks[25], (L, GDN_HEAD_DIM), 0.02),
        'final_norm_w': 1.0 + nrm(ks[27], (D,), 0.02),
    }


def reference(x, c, ctx, c_ctx, mod_w, mod_b, norm_w, ffn1_wg, ffn1_wu, ffn1_wd,
              ffn2_wg, ffn2_wu, ffn2_wd, w_in, w_out, pool_w, pool_scale, fourier_w,
              conv_dw_w, conv_dw_b, conv_ln_g, conv_ln_b, conv_pw_w,
              gdn_conv_w, gdn_a_log, gdn_dt_bias, gdn_norm_w, final_norm_w):
    rows = x.shape[1] // GRID_W
    hx, hc = x, ctx
    cs = jax.nn.silu(c)
    ccs = jax.nn.silu(c_ctx)
    for l in range(DEPTH):
        last = l == DEPTH - 1
        mx = jnp.split((cs @ mod_w[l] + mod_b[l])[:, None, :], N_MOD, axis=-1)
        mc = jnp.split(ccs @ mod_w[l] + mod_b[l], N_MOD, axis=-1)
        f1 = (ffn1_wg[l], ffn1_wu[l], ffn1_wd[l])
        f2 = (ffn2_wg[l], ffn2_wu[l], ffn2_wd[l])
        hx = hx + 0.5 * mx[2] * _swiglu(_modulate(_rms(hx, norm_w[l, 0]), mx[0], mx[1]), *f1)
        hc = hc + 0.5 * mc[2] * _swiglu(_modulate(_rms(hc, norm_w[l, 0]), mc[0], mc[1]), *f1)
        nx = _modulate(_rms(hx, norm_w[l, 1]), mx[3], mx[4])
        nc = _modulate(_rms(hc, norm_w[l, 1]), mc[3], mc[4])
        yx, yc = _token_mixer(nx, nc, rows, not last, w_in[l], w_out[l], pool_w[l], pool_scale[l],
                              fourier_w[l], conv_dw_w[l], conv_dw_b[l], conv_ln_g[l], conv_ln_b[l],
                              conv_pw_w[l], gdn_conv_w[l], gdn_a_log[l], gdn_dt_bias[l], gdn_norm_w[l])
        hx = hx + mx[5] * yx
        hx = hx + 0.5 * mx[8] * _swiglu(_modulate(_rms(hx, norm_w[l, 2]), mx[6], mx[7]), *f2)
        if not last:
            hc = hc + mc[5] * yc
            hc = hc + 0.5 * mc[8] * _swiglu(_modulate(_rms(hc, norm_w[l, 2]), mc[6], mc[7]), *f2)
    return _rms(hx, final_norm_w)
```

```python
import functools
import math

import jax
import jax.numpy as jnp
import numpy as np
from jax import lax
from jax.experimental import pallas as pl
from jax.experimental.pallas import tpu as pltpu

D_MODEL = 1024
DEPTH = 4
GRID_W = 64
N_MIXERS = 4
GROUP_W = D_MODEL // N_MIXERS
POOL_WINDOWS = (2, 4, 8, 16)
POOL_GROUPS = 4
POOL_GW = GROUP_W // POOL_GROUPS
FOURIER_GROUPS = 4
FOURIER_GW = GROUP_W // FOURIER_GROUPS
CONV_K = 31
GDN_HEAD_DIM = 64
GDN_HEADS = GROUP_W // GDN_HEAD_DIM
GDN_CONV = 3
GDN_CHUNK = 64
FFN_HIDDEN = 128 * ((8 * D_MODEL // 3 + 127) // 128)
N_MOD = 9
EPS = 1e-6
POOL_OFF = 0
FOURIER_OFF = POOL_OFF + GROUP_W
CONV_OFF = FOURIER_OFF + GROUP_W
GDN_OFF = CONV_OFF + 2 * GROUP_W

LANES = 128
VMEM_LIMIT = 56 * 1024 * 1024
FFN_CHUNK = 256
AB_PAD = LANES

BF16 = jnp.bfloat16
F32 = jnp.float32


def _params(n_axes):
    return pltpu.CompilerParams(dimension_semantics=("parallel",) * n_axes,
                                vmem_limit_bytes=VMEM_LIMIT)


def _resident(shape):
    nd = len(shape)
    return pl.BlockSpec(shape, lambda *_: (0,) * nd, pipeline_mode=pl.Buffered(1))


def _rms_mod(h, nw, shift, scale):
    ms = jnp.mean(h * h, axis=-1, keepdims=True)
    n = h * lax.rsqrt(ms + EPS) * nw
    return n * (1.0 + scale) + shift


def _ffn_kernel(h_ref, nw_ref, sh_ref, sc_ref, gt_ref, wg_ref, wu_ref, wd_ref, o_ref):
    h = h_ref[0]
    nb = _rms_mod(h, nw_ref[...], sh_ref[0], sc_ref[0]).astype(BF16)
    acc = jnp.zeros(h.shape, F32)
    for f0 in range(0, FFN_HIDDEN, FFN_CHUNK):
        g = jnp.dot(nb, wg_ref[:, f0:f0 + FFN_CHUNK], preferred_element_type=F32)
        u = jnp.dot(nb, wu_ref[:, f0:f0 + FFN_CHUNK], preferred_element_type=F32)
        a = (g * jax.nn.sigmoid(g) * u).astype(BF16)
        acc = acc + jnp.dot(a, wd_ref[f0:f0 + FFN_CHUNK, :], preferred_element_type=F32)
    o_ref[0] = h + (0.5 * gt_ref[0]) * acc


def _ffn(h, nw, shift, scale, gate, wg, wu, wd, tm):
    b, t, d = h.shape
    vec = pl.BlockSpec((1, 1, d), lambda i, j: (i, 0, 0))
    tok = pl.BlockSpec((1, tm, d), lambda i, j: (i, j, 0))
    return pl.pallas_call(
        _ffn_kernel,
        grid=(b, t // tm),
        in_specs=[tok, _resident((1, d)), vec, vec, vec,
                  _resident(wg.shape), _resident(wu.shape), _resident(wd.shape)],
        out_specs=tok,
        out_shape=jax.ShapeDtypeStruct(h.shape, F32),
        compiler_params=_params(2),
        name="ffn",
    )(h, nw, shift, scale, gate, wg, wu, wd)


IN_SPLITS = (GROUP_W, GROUP_W, 2 * GROUP_W, 3 * GROUP_W, GROUP_W, AB_PAD)


def _inproj_kernel(h_ref, nw_ref, sh_ref, sc_ref, w_ref, *o_refs):
    nb = _rms_mod(h_ref[0], nw_ref[...], sh_ref[0], sc_ref[0]).astype(BF16)
    off = 0
    for o_ref in o_refs:
        wdt = o_ref.shape[-1]
        o_ref[0] = jnp.dot(nb, w_ref[:, off:off + wdt], preferred_element_type=F32)
        off += wdt


def _inproj(h, nw, shift, scale, w, splits, tm):
    b, t, d = h.shape
    vec = pl.BlockSpec((1, 1, d), lambda i, j: (i, 0, 0))
    tok = pl.BlockSpec((1, tm, d), lambda i, j: (i, j, 0))
    return pl.pallas_call(
        _inproj_kernel,
        grid=(b, t // tm),
        in_specs=[tok, _resident((1, d)), vec, vec, _resident(w.shape)],
        out_specs=[pl.BlockSpec((1, tm, s), lambda i, j: (i, j, 0)) for s in splits],
        out_shape=[jax.ShapeDtypeStruct((b, t, s), F32) for s in splits],
        compiler_params=_params(2),
        name="inproj",
    )(h, nw, shift, scale, w)


def _outproj_kernel(h_ref, y_ref, gt_ref, w_ref, o_ref):
    y = jnp.dot(y_ref[0].astype(BF16), w_ref[...], preferred_element_type=F32)
    o_ref[0] = h_ref[0] + gt_ref[0] * y


def _outproj(h, y, gate, w, tm):
    b, t, d = h.shape
    vec = pl.BlockSpec((1, 1, d), lambda i, j: (i, 0, 0))
    tok = pl.BlockSpec((1, tm, d), lambda i, j: (i, j, 0))
    return pl.pallas_call(
        _outproj_kernel,
        grid=(b, t // tm),
        in_specs=[tok, pl.BlockSpec((1, tm, y.shape[-1]), lambda i, j: (i, j, 0)), vec,
                  _resident(w.shape)],
        out_specs=tok,
        out_shape=jax.ShapeDtypeStruct(h.shape, F32),
        compiler_params=_params(2),
        name="outproj",
    )(h, y, gate, w)


def _rms_kernel(h_ref, nw_ref, o_ref):
    h = h_ref[0]
    ms = jnp.mean(h * h, axis=-1, keepdims=True)
    o_ref[0] = h * lax.rsqrt(ms + EPS) * nw_ref[...]


def _final_rms(h, nw, tm):
    b, t, d = h.shape
    tok = pl.BlockSpec((1, tm, d), lambda i, j: (i, j, 0))
    return pl.pallas_call(
        _rms_kernel,
        grid=(b, t // tm),
        in_specs=[tok, _resident((1, d))],
        out_specs=tok,
        out_shape=jax.ShapeDtypeStruct(h.shape, F32),
        compiler_params=_params(2),
        name="final_rms",
    )(h, nw)


def _dwconv(x, w, pad):
    ch = x.shape[-1]
    filt = w[:, None, :].astype(x.dtype)
    return lax.conv_general_dilated(x, filt, (1,), [(pad, pad)],
                                    dimension_numbers=('NWC', 'WIO', 'NWC'),
                                    feature_group_count=ch)


def _window_bounds(n, w):
    t = jnp.arange(n)
    return jnp.clip(t - w // 2, 0, n), jnp.clip(t + w - w // 2, 0, n)


def _pool_means_seq(u):
    n = u.shape[1]
    p = jnp.pad(jnp.cumsum(u.astype(F32), axis=1), ((0, 0), (1, 0), (0, 0)))
    outs = []
    for gi, w in enumerate(POOL_WINDOWS):
        pg = p[..., gi * POOL_GW:(gi + 1) * POOL_GW]
        lo, hi = _window_bounds(n, w)
        cnt = (hi - lo).astype(F32)
        outs.append((pg[:, hi] - pg[:, lo]) / cnt[None, :, None])
    return jnp.concatenate(outs, axis=-1)


def _pool_means_grid(u, rows):
    bsz, n, ch = u.shape
    uf = u.astype(F32).reshape(bsz, rows, GRID_W, ch)
    p = jnp.pad(jnp.cumsum(jnp.cumsum(uf, axis=1), axis=2), ((0, 0), (1, 0), (1, 0), (0, 0)))
    outs = []
    for gi, w in enumerate(POOL_WINDOWS):
        pg = p[..., gi * POOL_GW:(gi + 1) * POOL_GW]
        r0, r1 = _window_bounds(rows, w)
        c0, c1 = _window_bounds(GRID_W, w)
        s = (pg[:, r1][:, :, c1] - pg[:, r0][:, :, c1]
             - pg[:, r1][:, :, c0] + pg[:, r0][:, :, c0])
        cnt = ((r1 - r0)[:, None] * (c1 - c0)[None, :]).astype(F32)
        outs.append(s / cnt[None, :, :, None])
    return jnp.concatenate(outs, axis=-1).reshape(bsz, n, ch)


def _pool_mixer(u, mean, pool_w, pool_scale):
    bsz, n, _ = u.shape
    d = (mean - u.astype(F32)).astype(u.dtype).reshape(bsz, n, POOL_GROUPS, POOL_GW)
    y = jnp.einsum('bngc,gcd->bngd', d, pool_w).reshape(bsz, n, GROUP_W)
    return y * pool_scale


def _fourier_mixer(u, fourier_w):
    bsz, n, _ = u.shape
    uf = u.astype(F32).reshape(bsz, n, FOURIER_GROUPS, FOURIER_GW)
    f = jnp.fft.fft2(uf, axes=(1, 3), norm='ortho').real.reshape(bsz, n, GROUP_W)
    return f.astype(u.dtype) @ fourier_w


def _conformer_conv(u2, dw_w, dw_b, ln_g, ln_b, pw_w):
    a, b = jnp.split(u2, 2, axis=-1)
    h = _dwconv(a * jax.nn.sigmoid(b), dw_w, CONV_K // 2) + dw_b
    hf = h.astype(F32)
    mu = jnp.mean(hf, axis=-1, keepdims=True)
    var = jnp.mean(jnp.square(hf - mu), axis=-1, keepdims=True)
    h = ((hf - mu) * lax.rsqrt(var + EPS) * ln_g.astype(F32) + ln_b.astype(F32)).astype(u2.dtype)
    return jax.nn.silu(h) @ pw_w


def _l2norm(t):
    return t * lax.rsqrt(jnp.sum(t * t, axis=-1, keepdims=True) + EPS)


def _gated_delta_chunked(q, k, v, g, beta, s0):
    bsz, n, h, _ = q.shape
    dv = v.shape[-1]
    nc = n // GDN_CHUNK

    def blk(t):
        t = t.reshape((bsz, nc, GDN_CHUNK, h) + t.shape[3:])
        return jnp.moveaxis(t, (1, 3), (0, 2))

    qb, kb, vb, gb, bb = blk(q), blk(k), blk(v), blk(g), blk(beta)
    gc = jnp.cumsum(gb, axis=-1)
    idx = jnp.arange(GDN_CHUNK)
    incl = idx[:, None] >= idx[None, :]
    strict = idx[:, None] > idx[None, :]
    gam = jnp.exp(jnp.where(incl, gc[..., :, None] - gc[..., None, :], -jnp.inf))
    kbeta = kb * bb[..., None]
    m = jnp.where(strict, jnp.einsum('nbhik,nbhjk->nbhij', kbeta, kb) * gam, 0.0)
    a = m + jnp.eye(GDN_CHUNK, dtype=m.dtype)
    rhs = jnp.concatenate([vb * bb[..., None], kbeta * jnp.exp(gc)[..., None]], axis=-1)
    sol = lax.linalg.triangular_solve(a, rhs, left_side=True, lower=True, unit_diagonal=True)
    u, w = sol[..., :dv], sol[..., dv:]
    aqk = jnp.einsum('nbhik,nbhjk->nbhij', qb, kb) * gam
    qdec = qb * jnp.exp(gc)[..., None]
    glast = gc[..., -1]
    kdec = kb * jnp.exp(glast[..., None] - gc)[..., None]

    def step(s, xs):
        u_i, w_i, q_i, k_i, a_i, gl_i = xs
        v_new = u_i - jnp.einsum('bhck,bhkv->bhcv', w_i, s)
        o_i = jnp.einsum('bhck,bhkv->bhcv', q_i, s) + jnp.einsum('bhij,bhjv->bhiv', a_i, v_new)
        s = s * jnp.exp(gl_i)[..., None, None] + jnp.einsum('bhck,bhcv->bhkv', k_i, v_new)
        return s, o_i

    s_fin, o = lax.scan(step, s0, (u, w, qdec, kdec, aqk, glast))
    o = jnp.moveaxis(o, (0, 2), (1, 3)).reshape(bsz, n, h, dv)
    return o, s_fin


def _gdn_inputs(qkv_p, z, ab, conv_w, a_log, dt_bias):
    bsz, n, _ = qkv_p.shape
    qkv = jax.nn.silu(_dwconv(qkv_p, conv_w, GDN_CONV // 2)).astype(F32)
    q, k, v = jnp.split(qkv.reshape(bsz, n, 3, GDN_HEADS, GDN_HEAD_DIM), 3, axis=2)
    q = _l2norm(q[:, :, 0]) * (GDN_HEAD_DIM ** -0.5)
    k = _l2norm(k[:, :, 0])
    v = v[:, :, 0]
    a = ab[..., :2 * GDN_HEADS].reshape(bsz, n, 2, GDN_HEADS)
    b = ab[..., 2 * GDN_HEADS:4 * GDN_HEADS].reshape(bsz, n, 2, GDN_HEADS)
    g = -jnp.exp(a_log.astype(F32)) * jax.nn.softplus(a + dt_bias.astype(F32))
    return q, k, v, z, g, jax.nn.sigmoid(b)


def _gated_out(o, z, w):
    zf = z.astype(F32).reshape(o.shape)
    y = o * lax.rsqrt(jnp.mean(o * o, axis=-1, keepdims=True) + EPS) * w.astype(F32) * jax.nn.silu(zf)
    return y.reshape(o.shape[0], o.shape[1], -1).astype(z.dtype)


def _gdn_mixer(px, pc, need_ctx, conv_w, a_log, dt_bias, norm_w):
    qc, kc, vc, zc, gc, bc = _gdn_inputs(*pc, conv_w, a_log, dt_bias)
    qx, kx, vx, zx, gx, bx = _gdn_inputs(*px, conv_w, a_log, dt_bias)
    s0 = jnp.zeros((qx.shape[0], GDN_HEADS, GDN_HEAD_DIM, GDN_HEAD_DIM), F32)
    fl = lambda t: jnp.flip(t, axis=1)
    oc_f, sc_f = _gated_delta_chunked(qc, kc, vc, gc[:, :, 0], bc[:, :, 0], s0)
    oc_b, sc_b = _gated_delta_chunked(fl(qc), fl(kc), fl(vc), fl(gc[:, :, 1]), fl(bc[:, :, 1]), s0)
    ox_f, _ = _gated_delta_chunked(qx, kx, vx, gx[:, :, 0], bx[:, :, 0], sc_f)
    ox_b, _ = _gated_delta_chunked(fl(qx), fl(kx), fl(vx), fl(gx[:, :, 1]), fl(bx[:, :, 1]), sc_b)
    yx = _gated_out(ox_f + fl(ox_b), zx, norm_w)
    yc = _gated_out(oc_f + fl(oc_b), zc, norm_w) if need_ctx else None
    return yx, yc


def _pad_w_in(w_in):
    pad = AB_PAD - 4 * GDN_HEADS
    return jnp.pad(w_in, ((0, 0), (0, pad))).astype(BF16)


def kernel(x, c, ctx, c_ctx, mod_w, mod_b, norm_w, ffn1_wg, ffn1_wu, ffn1_wd, ffn2_wg, ffn2_wu,
           ffn2_wd, w_in, w_out, pool_w, pool_scale, fourier_w, conv_dw_w, conv_dw_b, conv_ln_g,
           conv_ln_b, conv_pw_w, gdn_conv_w, gdn_a_log, gdn_dt_bias, gdn_norm_w, final_norm_w):
    bsz, seq, d = x.shape
    n_ctx = ctx.shape[1]
    rows = seq // GRID_W
    tm_x, tm_c = 512, n_ctx
    hx, hc = x, ctx
    cs = jax.nn.silu(c)
    ccs = jax.nn.silu(c_ctx)
    for l in range(DEPTH):
        last = l == DEPTH - 1
        mx = jnp.split((cs @ mod_w[l] + mod_b[l])[:, None, :], N_MOD, axis=-1)
        mc = [jnp.broadcast_to(m[None, None, :], (bsz, 1, d))
              for m in jnp.split(ccs @ mod_w[l] + mod_b[l], N_MOD, axis=-1)]
        nw = norm_w[l][:, None, :]
        f1 = (ffn1_wg[l].astype(BF16), ffn1_wu[l].astype(BF16), ffn1_wd[l].astype(BF16))
        f2 = (ffn2_wg[l].astype(BF16), ffn2_wu[l].astype(BF16), ffn2_wd[l].astype(BF16))
        w_in_p = _pad_w_in(w_in[l])
        w_out_b = w_out[l].astype(BF16)

        hx = _ffn(hx, nw[0], mx[0], mx[1], mx[2], *f1, tm_x)
        hc = _ffn(hc, nw[0], mc[0], mc[1], mc[2], *f1, tm_c)

        px = _inproj(hx, nw[1], mx[3], mx[4], w_in_p, IN_SPLITS, tm_x)
        if last:
            pcs = _inproj(hc, nw[1], mc[3], mc[4], w_in_p[:, GDN_OFF:], IN_SPLITS[3:], tm_c)
            pc = (None, None, None) + tuple(pcs)
        else:
            pc = _inproj(hc, nw[1], mc[3], mc[4], w_in_p, IN_SPLITS, tm_c)

        gdn_x, gdn_c = _gdn_mixer(px[3:], pc[3:], not last, gdn_conv_w[l], gdn_a_log[l],
                                  gdn_dt_bias[l], gdn_norm_w[l])
        yx = jnp.concatenate([
            _pool_mixer(px[0], _pool_means_grid(px[0], rows), pool_w[l], pool_scale[l]),
            _fourier_mixer(px[1], fourier_w[l]),
            _conformer_conv(px[2], conv_dw_w[l], conv_dw_b[l], conv_ln_g[l], conv_ln_b[l],
                            conv_pw_w[l]),
            gdn_x], axis=-1)
        hx = _outproj(hx, yx, mx[5], w_out_b, tm_x)
        hx = _ffn(hx, nw[2], mx[6], mx[7], mx[8], *f2, tm_x)
        if not last:
            yc = jnp.concatenate([
                _pool_mixer(pc[0], _pool_means_seq(pc[0]), pool_w[l], pool_scale[l]),
                _fourier_mixer(pc[1], fourier_w[l]),
                _conformer_conv(pc[2], conv_dw_w[l], conv_dw_b[l], conv_ln_g[l], conv_ln_b[l],
                                conv_pw_w[l]),
                gdn_c], axis=-1)
            hc = _outproj(hc, yc, mc[5], w_out_b, tm_c)
            hc = _ffn(hc, nw[2], mc[6], mc[7], mc[8], *f2, tm_c)
    return _final_rms(hx, final_norm_w[None, :], tm_x)
```

```python
import functools
import math

import jax
import jax.numpy as jnp
import numpy as np
from jax import lax
from jax.experimental import pallas as pl
from jax.experimental.pallas import tpu as pltpu

D_MODEL = 1024
DEPTH = 4
GRID_W = 64
N_MIXERS = 4
GROUP_W = D_MODEL // N_MIXERS
POOL_WINDOWS = (2, 4, 8, 16)
POOL_GROUPS = 4
POOL_GW = GROUP_W // POOL_GROUPS
FOURIER_GROUPS = 4
FOURIER_GW = GROUP_W // FOURIER_GROUPS
CONV_K = 31
GDN_HEAD_DIM = 64
GDN_HEADS = GROUP_W // GDN_HEAD_DIM
GDN_CONV = 3
GDN_CHUNK = 64
FFN_HIDDEN = 128 * ((8 * D_MODEL // 3 + 127) // 128)
N_MOD = 9
EPS = 1e-6
POOL_OFF = 0
FOURIER_OFF = POOL_OFF + GROUP_W
CONV_OFF = FOURIER_OFF + GROUP_W
GDN_OFF = CONV_OFF + 2 * GROUP_W

LANES = 128
VMEM_LIMIT = 56 * 1024 * 1024
FFN_CHUNK = 256
AB_PAD = LANES

BF16 = jnp.bfloat16
F32 = jnp.float32


def _params(n_axes):
    return pltpu.CompilerParams(dimension_semantics=("parallel",) * n_axes,
                                vmem_limit_bytes=VMEM_LIMIT)


def _resident(shape):
    nd = len(shape)
    return pl.BlockSpec(shape, lambda *_: (0,) * nd, pipeline_mode=pl.Buffered(1))


def _rms_mod(h, nw, shift, scale):
    ms = jnp.mean(h * h, axis=-1, keepdims=True)
    n = h * lax.rsqrt(ms + EPS) * nw
    return n * (1.0 + scale) + shift


def _ffn_kernel(h_ref, nw_ref, sh_ref, sc_ref, gt_ref, wg_ref, wu_ref, wd_ref, o_ref):
    h = h_ref[0]
    nb = _rms_mod(h, nw_ref[...], sh_ref[0], sc_ref[0]).astype(BF16)
    acc = jnp.zeros(h.shape, F32)
    for f0 in range(0, FFN_HIDDEN, FFN_CHUNK):
        g = jnp.dot(nb, wg_ref[:, f0:f0 + FFN_CHUNK], preferred_element_type=F32)
        u = jnp.dot(nb, wu_ref[:, f0:f0 + FFN_CHUNK], preferred_element_type=F32)
        a = (g * jax.nn.sigmoid(g) * u).astype(BF16)
        acc = acc + jnp.dot(a, wd_ref[f0:f0 + FFN_CHUNK, :], preferred_element_type=F32)
    o_ref[0] = h + (0.5 * gt_ref[0]) * acc


def _ffn(h, nw, shift, scale, gate, wg, wu, wd, tm):
    b, t, d = h.shape
    vec = pl.BlockSpec((1, 1, d), lambda i, j: (i, 0, 0))
    tok = pl.BlockSpec((1, tm, d), lambda i, j: (i, j, 0))
    return pl.pallas_call(
        _ffn_kernel,
        grid=(b, t // tm),
        in_specs=[tok, _resident((1, d)), vec, vec, vec,
                  _resident(wg.shape), _resident(wu.shape), _resident(wd.shape)],
        out_specs=tok,
        out_shape=jax.ShapeDtypeStruct(h.shape, F32),
        compiler_params=_params(2),
        name="ffn",
    )(h, nw, shift, scale, gate, wg, wu, wd)


IN_SPLITS = (GROUP_W, GROUP_W, 2 * GROUP_W, 3 * GROUP_W, GROUP_W, AB_PAD)


def _inproj_kernel(h_ref, nw_ref, sh_ref, sc_ref, w_ref, *o_refs):
    nb = _rms_mod(h_ref[0], nw_ref[...], sh_ref[0], sc_ref[0]).astype(BF16)
    off = 0
    for o_ref in o_refs:
        wdt = o_ref.shape[-1]
        o_ref[0] = jnp.dot(nb, w_ref[:, off:off + wdt], preferred_element_type=F32)
        off += wdt


def _inproj(h, nw, shift, scale, w, splits, tm):
    b, t, d = h.shape
    vec = pl.BlockSpec((1, 1, d), lambda i, j: (i, 0, 0))
    tok = pl.BlockSpec((1, tm, d), lambda i, j: (i, j, 0))
    return pl.pallas_call(
        _inproj_kernel,
        grid=(b, t // tm),
        in_specs=[tok, _resident((1, d)), vec, vec, _resident(w.shape)],
        out_specs=[pl.BlockSpec((1, tm, s), lambda i, j: (i, j, 0)) for s in splits],
        out_shape=[jax.ShapeDtypeStruct((b, t, s), F32) for s in splits],
        compiler_params=_params(2),
        name="inproj",
    )(h, nw, shift, scale, w)


def _rms_kernel(h_ref, nw_ref, o_ref):
    h = h_ref[0]
    ms = jnp.mean(h * h, axis=-1, keepdims=True)
    o_ref[0] = h * lax.rsqrt(ms + EPS) * nw_ref[...]


def _final_rms(h, nw, tm):
    b, t, d = h.shape
    tok = pl.BlockSpec((1, tm, d), lambda i, j: (i, j, 0))
    return pl.pallas_call(
        _rms_kernel,
        grid=(b, t // tm),
        in_specs=[tok, _resident((1, d))],
        out_specs=tok,
        out_shape=jax.ShapeDtypeStruct(h.shape, F32),
        compiler_params=_params(2),
        name="final_rms",
    )(h, nw)


def _split3(x):
    hi = x.astype(BF16)
    r1 = x - hi.astype(F32)
    mid = r1.astype(BF16)
    lo = (r1 - mid.astype(F32)).astype(BF16)
    return hi, mid, lo


def _dot01(x, w01):
    return sum(jnp.dot(p, w01, preferred_element_type=F32) for p in _split3(x))


def _dot01_left(w01, x):
    return sum(jnp.dot(w01, p, preferred_element_type=F32) for p in _split3(x))


def _blockdiag(w):
    g, a, b = w.shape
    return jnp.einsum('gab,gh->gahb', w, jnp.eye(g, dtype=w.dtype)).reshape(g * a, g * b)


def _halo_specs(tm, halo, t, width):
    r = tm // halo
    last = t // halo - 1
    prev = pl.BlockSpec((1, halo, width), lambda i, j: (i, jnp.maximum(j * r - 1, 0), 0))
    cur = pl.BlockSpec((1, tm, width), lambda i, j: (i, j, 0))
    nxt = pl.BlockSpec((1, halo, width), lambda i, j: (i, jnp.minimum((j + 1) * r, last), 0))
    return prev, cur, nxt


def _edge_masked(prev_ref, next_ref):
    j, nj = pl.program_id(1), pl.num_programs(1)
    top = jnp.where(j > 0, prev_ref[0], 0.0)
    bot = jnp.where(j < nj - 1, next_ref[0], 0.0)
    return top, bot


POOL_TM = 1024
POOL_HALO = 512


def _pool_window_sums_1d(u, row_len):
    n = u.shape[0]
    col = lax.broadcasted_iota(jnp.int32, u.shape, 0) & (row_len - 1)
    grp = lax.broadcasted_iota(jnp.int32, u.shape, 1) // POOL_GW

    def back(x, s):
        return jnp.where(col >= s, pltpu.roll(x, s, 0), 0.0)

    def fwd(x, s):
        return jnp.where(col < row_len - s, pltpu.roll(x, n - s, 0), 0.0)

    b = back(u, 1)
    f = u
    out = b + f
    for gi in range(1, len(POOL_WINDOWS)):
        s = POOL_WINDOWS[gi] // 4
        b = b + back(b, s)
        f = f + fwd(f, s)
        out = jnp.where(grp >= gi, b + f, out)
    return out


def _pool_counts(idx, extent, halfw):
    return jnp.minimum(idx + halfw, extent) - jnp.maximum(idx - halfw, 0)


def _pool_finish(mean, u, wbd_ref, scale_ref, o_ref):
    d = (mean - u).astype(BF16)
    o_ref[0] = jnp.dot(d, wbd_ref[...], preferred_element_type=F32) * scale_ref[...]


def _pool_grid_kernel(prev_ref, cur_ref, next_ref, wbd_ref, scale_ref, o_ref, *, n_rows):
    tm = cur_ref.shape[1]
    top, bot = _edge_masked(prev_ref, next_ref)
    u = cur_ref[0]
    ext = jnp.concatenate([top, u, bot], axis=0)
    cw = _pool_window_sums_1d(ext, GRID_W)
    grp = lax.broadcasted_iota(jnp.int32, (tm, GROUP_W), 1) // POOL_GW
    arr, off = cw, 0
    tot = None
    for gi, w in enumerate(POOL_WINDOWS):
        sh = GRID_W * max(w // 4, 1) if gi else GRID_W
        if gi == 0:
            arr = arr[:-sh] + arr[sh:]
            off = sh
        else:
            arr = arr[:-2 * sh] + arr[2 * sh:]
            off = off + sh
        centre = arr[POOL_HALO - off:POOL_HALO - off + tm]
        tot = centre if tot is None else jnp.where(grp >= gi, centre, tot)
    tok = pl.program_id(1) * tm + lax.broadcasted_iota(jnp.int32, (tm, GROUP_W), 0)
    halfw = jnp.left_shift(1, grp)
    cnt = (_pool_counts(tok // GRID_W, n_rows, halfw) * _pool_counts(tok & (GRID_W - 1), GRID_W, halfw))
    _pool_finish(tot / cnt.astype(F32), u, wbd_ref, scale_ref, o_ref)


def _pool_seq_kernel(u_ref, wbd_ref, scale_ref, o_ref):
    u = u_ref[0]
    n = u.shape[0]
    tot = _pool_window_sums_1d(u, n)
    grp = lax.broadcasted_iota(jnp.int32, u.shape, 1) // POOL_GW
    tok = lax.broadcasted_iota(jnp.int32, u.shape, 0)
    cnt = _pool_counts(tok, n, jnp.left_shift(1, grp))
    _pool_finish(tot / cnt.astype(F32), u, wbd_ref, scale_ref, o_ref)


def _pool_mix(u, wbd, scale, grid_rows):
    b, t, w = u.shape
    out_shape = jax.ShapeDtypeStruct(u.shape, F32)
    if grid_rows is None:
        tok = pl.BlockSpec((1, t, w), lambda i: (i, 0, 0))
        return pl.pallas_call(
            _pool_seq_kernel, grid=(b,),
            in_specs=[tok, _resident(wbd.shape), _resident(scale.shape)],
            out_specs=tok, out_shape=out_shape, compiler_params=_params(1), name="pool_seq",
        )(u, wbd, scale)
    prev, cur, nxt = _halo_specs(POOL_TM, POOL_HALO, t, w)
    return pl.pallas_call(
        functools.partial(_pool_grid_kernel, n_rows=grid_rows),
        grid=(b, t // POOL_TM),
        in_specs=[prev, cur, nxt, _resident(wbd.shape), _resident(scale.shape)],
        out_specs=cur, out_shape=out_shape, compiler_params=_params(2), name="pool_grid",
    )(u, u, u, wbd, scale)


CONV_TM = 512
CONV_HALO = 16
CONV_SUB = 64


def _glu(x):
    return x[:, :GROUP_W] * jax.nn.sigmoid(x[:, GROUP_W:])


def _conv_kernel(prev_ref, cur_ref, next_ref, dww_ref, dwb_ref, lng_ref, lnb_ref, pw_ref, o_ref,
                 ext_ref):
    tm = cur_ref.shape[1]
    top, bot = _edge_masked(prev_ref, next_ref)
    ext_ref[0:CONV_HALO] = _glu(top)
    ext_ref[CONV_HALO:CONV_HALO + tm] = _glu(cur_ref[0])
    ext_ref[CONV_HALO + tm:] = _glu(bot)
    base = CONV_HALO - CONV_K // 2
    for r0 in range(0, tm, CONV_SUB):
        acc = jnp.zeros((CONV_SUB, GROUP_W), F32)
        for k in range(CONV_K):
            acc = acc + dww_ref[k:k + 1, :] * ext_ref[r0 + base + k:r0 + base + k + CONV_SUB, :]
        h = acc + dwb_ref[...]
        mu = jnp.mean(h, axis=-1, keepdims=True)
        var = jnp.mean(jnp.square(h - mu), axis=-1, keepdims=True)
        h = (h - mu) * lax.rsqrt(var + EPS) * lng_ref[...] + lnb_ref[...]
        h = (h * jax.nn.sigmoid(h)).astype(BF16)
        o_ref[0, r0:r0 + CONV_SUB, :] = jnp.dot(h, pw_ref[...], preferred_element_type=F32)


def _conv_mix(u2, dw_w, dw_b, ln_g, ln_b, pw_w, tm):
    b, t, w2 = u2.shape
    prev, cur, nxt = _halo_specs(tm, CONV_HALO, t, w2)
    row = _resident((1, GROUP_W))
    return pl.pallas_call(
        _conv_kernel,
        grid=(b, t // tm),
        in_specs=[prev, cur, nxt, _resident(dw_w.shape), row, row, row, _resident(pw_w.shape)],
        out_specs=pl.BlockSpec((1, tm, GROUP_W), lambda i, j: (i, j, 0)),
        out_shape=jax.ShapeDtypeStruct((b, t, GROUP_W), F32),
        scratch_shapes=[pltpu.VMEM((tm + 2 * CONV_HALO, GROUP_W), F32)],
        compiler_params=_params(2), name="conv_mix",
    )(u2, u2, u2, dw_w, dw_b, ln_g, ln_b, pw_w)


FOUR_N1 = 128
FOUR_TT = 8
FOUR_KT = 8


def _hilo(w):
    w = jnp.asarray(w, F32)
    hi = w.astype(BF16)
    return hi, (w - hi.astype(F32)).astype(BF16)


def _dot_hp(x, wh, wl):
    xh = x.astype(BF16)
    xl = (x - xh.astype(F32)).astype(BF16)
    return (jnp.dot(xh, wh, preferred_element_type=F32) + jnp.dot(xh, wl, preferred_element_type=F32)
            + jnp.dot(xl, wh, preferred_element_type=F32))


def _dot_hp_left(wh, wl, x):
    xh = x.astype(BF16)
    xl = (x - xh.astype(F32)).astype(BF16)
    return (jnp.dot(wh, xh, preferred_element_type=F32) + jnp.dot(wl, xh, preferred_element_type=F32)
            + jnp.dot(wh, xl, preferred_element_type=F32))


def _cos_sin(n, rows=None, cols=None):
    r = np.arange(n if rows is None else rows, dtype=np.int64)
    c = np.arange(n if cols is None else cols, dtype=np.int64)
    ang = 2.0 * np.pi * ((np.outer(r, c) % n).astype(np.float64) / n)
    return np.cos(ang), np.sin(ang)


def _channel_dft():
    c, s = _cos_sin(FOURIER_GW)
    eye = np.eye(FOURIER_GROUPS)
    return np.concatenate([np.kron(eye, c), -np.kron(eye, s)], axis=1)


def _fourier_a_kernel(u_ref, cdh_ref, cdl_ref, f1h_ref, f1l_ref, twc_ref, tws_ref, o_ref):
    for j in range(FOUR_TT):
        x = u_ref[0, :, j * GROUP_W:(j + 1) * GROUP_W]
        z = _dot_hp(x, cdh_ref[...], cdl_ref[...])
        zz = jnp.concatenate([z[:, :GROUP_W], z[:, GROUP_W:]], axis=0)
        a = _dot_hp_left(f1h_ref[...], f1l_ref[...], zz)
        ar, ai = a[:FOUR_N1], a[FOUR_N1:]
        c = jnp.concatenate([twc_ref[j]] * (GROUP_W // LANES), axis=1)
        s = jnp.concatenate([tws_ref[j]] * (GROUP_W // LANES), axis=1)
        o_ref[0, 0, j] = ar * c + ai * s
        o_ref[0, 1, j] = ai * c - ar * s


def _fourier_b_kernel(g_ref, f2h_ref, f2l_ref, fw_ref, o_ref, *, scale):
    for kk in range(FOUR_KT):
        sl = slice(kk * GROUP_W, (kk + 1) * GROUP_W)
        gm = jnp.concatenate([g_ref[0, 0, :, sl], g_ref[0, 1, :, sl]], axis=0)
        f = _dot_hp_left(f2h_ref[...], f2l_ref[...], gm) * scale
        o_ref[0, :, sl] = jnp.dot(f.astype(BF16), fw_ref[...], preferred_element_type=F32)


def _fourier_small_kernel(u_ref, cdh_ref, cdl_ref, fh_ref, fl_ref, fw_ref, o_ref, *, scale):
    z = _dot_hp(u_ref[0], cdh_ref[...], cdl_ref[...])
    zz = jnp.concatenate([z[:, :GROUP_W], z[:, GROUP_W:]], axis=0)
    f = _dot_hp_left(fh_ref[...], fl_ref[...], zz) * scale
    o_ref[0] = jnp.dot(f.astype(BF16), fw_ref[...], preferred_element_type=F32)


def _fourier_mix(u, fw):
    b, n, w = u.shape
    scale = 1.0 / math.sqrt(n * FOURIER_GW)
    cdh, cdl = _hilo(_channel_dft())
    if n != FOUR_N1 * FOUR_N1:
        c, s = _cos_sin(n)
        fh, fl = _hilo(np.concatenate([c, s], axis=1))
        tok = pl.BlockSpec((1, n, w), lambda i: (i, 0, 0))
        return pl.pallas_call(
            functools.partial(_fourier_small_kernel, scale=scale), grid=(b,),
            in_specs=[tok] + [_resident(a.shape) for a in (cdh, cdl, fh, fl, fw)],
            out_specs=tok, out_shape=jax.ShapeDtypeStruct(u.shape, F32),
            compiler_params=_params(1), name="fourier_small",
        )(u, cdh, cdl, fh, fl, fw)
    n1 = FOUR_N1
    c1, s1 = _cos_sin(n1)
    f1h, f1l = _hilo(np.block([[c1, s1], [-s1, c1]]))
    f2h, f2l = _hilo(np.concatenate([c1, s1], axis=1))
    twc, tws = _cos_sin(n, rows=n1, cols=n1)
    twc = jnp.broadcast_to(jnp.asarray(twc, F32)[:, :, None], (n1, n1, LANES))
    tws = jnp.broadcast_to(jnp.asarray(tws, F32)[:, :, None], (n1, n1, LANES))
    tw_spec = pl.BlockSpec((FOUR_TT, n1, LANES), lambda i, j: (j, 0, 0))
    g = pl.pallas_call(
        _fourier_a_kernel, grid=(b, n1 // FOUR_TT),
        in_specs=[pl.BlockSpec((1, n1, FOUR_TT * w), lambda i, j: (i, 0, j))]
        + [_resident(a.shape) for a in (cdh, cdl, f1h, f1l)] + [tw_spec, tw_spec],
        out_specs=pl.BlockSpec((1, 2, FOUR_TT, n1, w), lambda i, j: (i, 0, j, 0, 0)),
        out_shape=jax.ShapeDtypeStruct((b, 2, n1, n1, w), F32),
        compiler_params=_params(2), name="fourier_a",
    )(u.reshape(b, n1, n1 * w), cdh, cdl, f1h, f1l, twc, tws)
    y = pl.pallas_call(
        functools.partial(_fourier_b_kernel, scale=scale), grid=(b, n1 // FOUR_KT),
        in_specs=[pl.BlockSpec((1, 2, n1, FOUR_KT * w), lambda i, j: (i, 0, 0, j))]
        + [_resident(a.shape) for a in (f2h, f2l, fw)],
        out_specs=pl.BlockSpec((1, n1, FOUR_KT * w), lambda i, j: (i, 0, j)),
        out_shape=jax.ShapeDtypeStruct((b, n1, n1 * w), F32),
        compiler_params=_params(2), name="fourier_b",
    )(g.reshape(b, 2, n1, n1 * w), f2h, f2l, fw)
    return y.reshape(b, n, w)


GDN_C = GDN_CHUNK
GDN_PREP_TM = 512
GDN_SCAN_TM = 512
GDN_SUB = 64
N_GATE = 4 * GDN_HEADS


def _softplus(x):
    return jnp.maximum(x, 0.0) + jnp.log1p(jnp.exp(-jnp.abs(x)))


def _gdn_consts(tm):
    i = np.arange(tm)
    same = (i[:, None] // GDN_C) == (i[None, :] // GDN_C)
    lower = same & (i[None, :] <= i[:, None])
    upper = same & (i[None, :] >= i[:, None])
    expand = np.zeros((AB_PAD, 4 * GROUP_W), np.float32)
    for s in range(4):
        for h in range(GDN_HEADS):
            c0 = s * GROUP_W + h * GDN_HEAD_DIM
            expand[s * GDN_HEADS + h, c0:c0 + GDN_HEAD_DIM] = 1.0
    lane = np.arange(GROUP_W)
    headones = (lane[:, None] // GDN_HEAD_DIM) == (lane[None, :] // GDN_HEAD_DIM)
    as_bf = lambda a: jnp.asarray(a, F32).astype(BF16)
    return as_bf(lower), as_bf(upper), as_bf(same), as_bf(expand), as_bf(headones)


def _bd(x, bdmask):
    xb = x.astype(BF16)
    return jnp.where(bdmask, jnp.concatenate([xb] * GDN_HEADS, axis=0), jnp.zeros((), BF16))


def _bd_mask():
    return (lax.broadcasted_iota(jnp.int32, (GROUP_W, GROUP_W), 0) // GDN_HEAD_DIM
            == lax.broadcasted_iota(jnp.int32, (GROUP_W, GROUP_W), 1) // GDN_HEAD_DIM)


def _dot(a, b):
    return jnp.dot(a.astype(BF16), b.astype(BF16), preferred_element_type=F32)


def _diag_blocks(full):
    head = (lax.broadcasted_iota(jnp.int32, (GDN_HEAD_DIM, full.shape[1]), 1) // GDN_HEAD_DIM) % GDN_HEADS
    out = None
    for h in range(GDN_HEADS):
        blk = jnp.where(head == h, full[h * GDN_HEAD_DIM:(h + 1) * GDN_HEAD_DIM], 0.0)
        out = blk if out is None else out + blk
    return out


def _tri_inverse(ms, eye, level_masks, bdmask):
    ds = [eye - jnp.where(level_masks[0], m, 0.0) for m in ms]
    for mask in level_masks[1:]:
        es = [_dot(jnp.where(mask, m, 0.0), _bd(d, bdmask)) for m, d in zip(ms, ds)]
        fs = [_dot(d, _bd(e, bdmask)) for d, e in zip(ds, es)]
        ds = [d - f for d, f in zip(ds, fs)]
    return ds


def _gdn_prep_kernel(prev_ref, cur_ref, next_ref, ab_ref, cw_ref, alog_ref, dtb_ref,
                     lo_ref, up_ref, same_ref, exp_ref, ones_ref,
                     pf_ref, qpf_ref, bmf_ref, o0f_ref, df_ref, pb_ref, qpb_ref, bmb_ref, o0b_ref, db_ref,
                     ext_ref, q_scr, k_scr, v_scr):
    tm = cur_ref.shape[1]
    halo = prev_ref.shape[1]
    c = GDN_C
    top, bot = _edge_masked(prev_ref, next_ref)
    ext_ref[0:halo] = top
    ext_ref[halo:halo + tm] = cur_ref[0]
    ext_ref[halo + tm:] = bot
    base = halo - GDN_CONV // 2
    for r0 in range(0, tm, GDN_SUB):
        acc = jnp.zeros((GDN_SUB, 3 * GROUP_W), F32)
        for t in range(GDN_CONV):
            acc = acc + cw_ref[t:t + 1, :] * ext_ref[r0 + base + t:r0 + base + t + GDN_SUB, :]
        x = acc * jax.nn.sigmoid(acc)
        q, k = x[:, :GROUP_W], x[:, GROUP_W:2 * GROUP_W]
        rows = slice(r0, r0 + GDN_SUB)
        q_scr[rows, :] = q * lax.rsqrt(_dot01(q * q, ones_ref[...]) + EPS) * (GDN_HEAD_DIM ** -0.5)
        k_scr[rows, :] = k * lax.rsqrt(_dot01(k * k, ones_ref[...]) + EPS)
        v_scr[rows, :] = x[:, 2 * GROUP_W:]

    ab = ab_ref[0]
    lane = lax.broadcasted_iota(jnp.int32, ab.shape, 1)
    g = -jnp.exp(alog_ref[...]) * _softplus(ab + dtb_ref[...])
    gate = jnp.where(lane < 2 * GDN_HEADS, g, jax.nn.sigmoid(ab))
    cum = jnp.where(lane < GDN_HEADS, _dot01_left(lo_ref[...], gate), _dot01_left(up_ref[...], gate))
    wide = _dot01(jnp.where(lane < 2 * GDN_HEADS, cum, gate), exp_ref[...])
    gcs = (wide[:, :GROUP_W], wide[:, GROUP_W:2 * GROUP_W])
    betas = (wide[:, 2 * GROUP_W:3 * GROUP_W], wide[:, 3 * GROUP_W:])
    ti = lax.broadcasted_iota(jnp.int32, (tm, GROUP_W), 0) & (c - 1)
    tj = lax.broadcasted_iota(jnp.int32, (tm, GROUP_W), 1) & (c - 1)
    rows_gc = [_dot01_left(same_ref[...], jnp.where(ti == tj, gc, 0.0)) for gc in gcs]
    gams = (jnp.where(ti >= tj, jnp.exp(jnp.minimum(gcs[0] - rows_gc[0], 0.0)), 0.0),
            jnp.where(ti <= tj, jnp.exp(jnp.minimum(gcs[1] - rows_gc[1], 0.0)), 0.0))

    ii = lax.broadcasted_iota(jnp.int32, (c, GROUP_W), 0)
    jj = lax.broadcasted_iota(jnp.int32, (c, GROUP_W), 1) & (c - 1)
    eye = (ii == jj).astype(F32)
    stricts = (ii > jj, ii < jj)
    levels = [((ii // (2 * s)) == (jj // (2 * s))) & ((ii // s) != (jj // s)) for s in (1, 2, 4, 8, 16, 32)]
    bdmask = _bd_mask()
    outs = ((pf_ref, qpf_ref, bmf_ref, o0f_ref, df_ref), (pb_ref, qpb_ref, bmb_ref, o0b_ref, db_ref))

    chunks = range(tm // c)
    rs = [slice(ci * c, (ci + 1) * c) for ci in chunks]
    qs = [q_scr[r, :] for r in rs]
    ks = [k_scr[r, :] for r in rs]
    vs = [v_scr[r, :] for r in rs]
    kqs = [lax.dot_general(jnp.concatenate([k, q], axis=0).astype(BF16), _bd(k, bdmask),
                           (((1,), (1,)), ((), ())), preferred_element_type=F32) for k, q in zip(ks, qs)]
    for di in range(2):
        gc_l = [gcs[di][r] for r in rs]
        gam_l = [gams[di][r] for r in rs]
        beta_l = [betas[di][r] for r in rs]
        ms = [jnp.where(stricts[di], beta * kq[:c] * gam, 0.0) for beta, kq, gam in zip(beta_l, kqs, gam_l)]
        ts = _tri_inverse(ms, eye, levels, bdmask)
        egcs = [jnp.exp(gc) for gc in gc_l]
        uws = [_dot(t, jnp.concatenate([_bd(v * beta, bdmask), _bd(k * beta * egc, bdmask)], axis=1))
               for t, v, k, beta, egc in zip(ts, vs, ks, beta_l, egcs)]
        aqks = [kq[c:] * gam for kq, gam in zip(kqs, gam_l)]
        ows = [_dot(aqk, jnp.concatenate([_bd(uw[:, :GROUP_W], bdmask), _bd(uw[:, GROUP_W:], bdmask)], axis=1))
               for aqk, uw in zip(aqks, uws)]
        p_ref, qp_ref, bm_ref, o0_ref, d_ref = outs[di]
        for ci, r in enumerate(rs):
            gc, uw, ow = gc_l[ci], uws[ci], ows[ci]
            glast = gc[0:1] if di else gc[c - 1:c]
            kdec = ks[ci] * jnp.exp(glast - gc)
            bp = _diag_blocks(lax.dot_general(kdec.astype(BF16), uw.astype(BF16), (((0,), (0,)), ((), ())),
                                              preferred_element_type=F32))
            bm_ref[0, r, :] = bp[:, :GROUP_W]
            p_ref[0, r, :] = bp[:, GROUP_W:].astype(BF16)
            o0_ref[0, r, :] = ow[:, :GROUP_W]
            qp_ref[0, r, :] = (qs[ci] * egcs[ci] - ow[:, GROUP_W:]).astype(BF16)
            d_ref[0, ci:ci + 1, :] = jnp.exp(glast)


def _gdn_prep(qkv, ab, conv_w, a_log, dt_bias, tm):
    b, t, w3 = qkv.shape
    halo = 8
    nck = tm // GDN_C
    prev, cur, nxt = _halo_specs(tm, halo, t, w3)
    pad = lambda a: jnp.pad(a.reshape(1, -1), ((0, 0), (0, AB_PAD - a.size)))
    consts = _gdn_consts(tm)
    tok = pl.BlockSpec((1, tm, GROUP_W), lambda i, j: (i, j, 0))
    dec = pl.BlockSpec((1, nck, GROUP_W), lambda i, j: (i, j, 0))
    tok_shape = lambda dt: jax.ShapeDtypeStruct((b, t, GROUP_W), dt)
    dir_specs = [tok, tok, tok, tok, dec]
    dir_shapes = [tok_shape(BF16), tok_shape(BF16), tok_shape(F32), tok_shape(F32),
                  jax.ShapeDtypeStruct((b, t // GDN_C, GROUP_W), F32)]
    return pl.pallas_call(
        _gdn_prep_kernel,
        grid=(b, t // tm),
        in_specs=[prev, cur, nxt, pl.BlockSpec((1, tm, AB_PAD), lambda i, j: (i, j, 0)),
                  _resident(conv_w.shape), _resident((1, AB_PAD)), _resident((1, AB_PAD))]
        + [_resident(cst.shape) for cst in consts],
        out_specs=dir_specs * 2,
        out_shape=dir_shapes * 2,
        scratch_shapes=[pltpu.VMEM((tm + 2 * halo, w3), F32)] + [pltpu.VMEM((tm, GROUP_W), F32)] * 3,
        compiler_params=_params(2), name="gdn_prep",
    )(qkv, qkv, qkv, ab, conv_w, pad(a_log), pad(dt_bias), *consts)


def _gdn_scan_kernel(pf_ref, qpf_ref, bmf_ref, o0f_ref, df_ref, pb_ref, qpb_ref, bmb_ref, o0b_ref, db_ref,
                     s0f_ref, s0b_ref, of_ref, ob_ref, sf_ref, sb_ref):
    nb, tm = pf_ref.shape[0], pf_ref.shape[1]
    nc = tm // GDN_C

    @pl.when(pl.program_id(0) == 0)
    def _():
        sf_ref[...] = s0f_ref[...]
        sb_ref[...] = s0b_ref[...]

    bdmask = _bd_mask()
    fwd = (pf_ref, qpf_ref, bmf_ref, o0f_ref, df_ref, of_ref)
    bwd = (pb_ref, qpb_ref, bmb_ref, o0b_ref, db_ref, ob_ref)
    chains = [(fwd, bi, False) for bi in range(nb)] + [(bwd, bi, True) for bi in range(nb)]
    states = [sf_ref[bi] for bi in range(nb)] + [sb_ref[bi] for bi in range(nb)]
    for step in range(nc):
        lhs, rows = [], []
        for (p_ref, qp_ref, _, _, _, _), bi, rev in chains:
            ci = nc - 1 - step if rev else step
            r = slice(ci * GDN_C, (ci + 1) * GDN_C)
            rows.append((ci, r))
            lhs.append(jnp.concatenate([p_ref[bi, r, :], qp_ref[bi, r, :]], axis=0))
        res = [jnp.dot(a, _bd(s, bdmask), preferred_element_type=F32) for a, s in zip(lhs, states)]
        new_states = []
        for (_, _, bm_ref, o0_ref, d_ref, o_ref), bi, _ in chains:
            n = len(new_states)
            ci, r = rows[n]
            o_ref[bi, r, :] = o0_ref[bi, r, :] + res[n][GDN_C:]
            new_states.append(d_ref[bi, ci:ci + 1, :] * states[n] - res[n][:GDN_C] + bm_ref[bi, r, :])
        states = new_states
    for bi in range(nb):
        sf_ref[bi] = states[bi]
        sb_ref[bi] = states[nb + bi]


def _gdn_scan(prep, s0f, s0b, tm):
    b, t, w = prep[0].shape
    nj = t // tm
    nck = tm // GDN_C
    fwd = pl.BlockSpec((b, tm, w), lambda j: (0, j, 0))
    bwd = pl.BlockSpec((b, tm, w), lambda j: (0, nj - 1 - j, 0))
    dfwd = pl.BlockSpec((b, nck, w), lambda j: (0, j, 0))
    dbwd = pl.BlockSpec((b, nck, w), lambda j: (0, nj - 1 - j, 0))
    st = pl.BlockSpec((b, GDN_HEAD_DIM, w), lambda j: (0, 0, 0))
    st_shape = jax.ShapeDtypeStruct((b, GDN_HEAD_DIM, w), F32)
    return pl.pallas_call(
        _gdn_scan_kernel,
        grid=(nj,),
        in_specs=[fwd] * 4 + [dfwd] + [bwd] * 4 + [dbwd] + [st, st],
        out_specs=[fwd, bwd, st, st],
        out_shape=[jax.ShapeDtypeStruct((b, t, w), F32)] * 2 + [st_shape] * 2,
        compiler_params=pltpu.CompilerParams(dimension_semantics=("arbitrary",),
                                             vmem_limit_bytes=VMEM_LIMIT),
        name="gdn_scan",
    )(*prep, s0f, s0b)


def _mix_out_kernel(h_ref, yp_ref, yf_ref, yc_ref, of_ref, ob_ref, z_ref, gt_ref, nw_ref, ones_ref,
                    w_ref, o_ref):
    o = of_ref[0] + ob_ref[0]
    ms = _dot01(o * o, ones_ref[...]) * (1.0 / GDN_HEAD_DIM)
    z = z_ref[0]
    yg = o * lax.rsqrt(ms + EPS) * nw_ref[...] * (z * jax.nn.sigmoid(z))
    y = jnp.zeros(h_ref.shape[1:], F32)
    for gi, part in enumerate((yp_ref[0], yf_ref[0], yc_ref[0], yg)):
        y = y + jnp.dot(part.astype(BF16), w_ref[gi * GROUP_W:(gi + 1) * GROUP_W, :],
                        preferred_element_type=F32)
    o_ref[0] = h_ref[0] + gt_ref[0] * y


def _mix_out(h, ys, o_f, o_b, z, gate, gdn_nw, w_out, tm):
    b, t, d = h.shape
    tok = pl.BlockSpec((1, tm, d), lambda i, j: (i, j, 0))
    grp = pl.BlockSpec((1, tm, GROUP_W), lambda i, j: (i, j, 0))
    ones = _gdn_consts(GDN_C)[4]
    return pl.pallas_call(
        _mix_out_kernel,
        grid=(b, t // tm),
        in_specs=[tok] + [grp] * 6 + [pl.BlockSpec((1, 1, d), lambda i, j: (i, 0, 0)),
                                     _resident((1, GROUP_W)), _resident(ones.shape), _resident(w_out.shape)],
        out_specs=tok,
        out_shape=jax.ShapeDtypeStruct(h.shape, F32),
        compiler_params=_params(2), name="mix_out",
    )(h, *ys, o_f, o_b, z, gate, gdn_nw, ones, w_out)


MOD_ROWS = 8
MOD_TN = 1152


def _mod_kernel(c_ref, w_ref, b_ref, o_ref):
    cv = c_ref[...]
    a = (cv * jax.nn.sigmoid(cv)).astype(BF16)
    o_ref[0] = jnp.dot(a, w_ref[0].astype(BF16), preferred_element_type=F32) + b_ref[0]


def _modulation(c, c_ctx, mod_w, mod_b):
    nl, d, n = mod_w.shape
    cond = jnp.concatenate([c, c_ctx[None, :]], axis=0)
    cond = jnp.pad(cond, ((0, MOD_ROWS - cond.shape[0]), (0, 0)))
    return pl.pallas_call(
        _mod_kernel,
        grid=(nl, n // MOD_TN),
        in_specs=[_resident(cond.shape), pl.BlockSpec((1, d, MOD_TN), lambda i, j: (i, 0, j)),
                  pl.BlockSpec((1, 1, MOD_TN), lambda i, j: (i, 0, j))],
        out_specs=pl.BlockSpec((1, MOD_ROWS, MOD_TN), lambda i, j: (i, 0, j)),
        out_shape=jax.ShapeDtypeStruct((nl, MOD_ROWS, n), F32),
        compiler_params=_params(2), name="modulation",
    )(cond, mod_w, mod_b[:, None, :])


def _pad_w_in(w_in):
    pad = AB_PAD - N_GATE
    return jnp.pad(w_in, ((0, 0), (0, pad))).astype(BF16)


def _token_mix(p, gdn_state, grid_rows, tm, wl, need_out):
    p_pool, p_four, p_conv, p_qkv, p_z, p_ab = p
    prep = _gdn_prep(p_qkv, p_ab, wl["gdn_conv_w"], wl["gdn_a_log"], wl["gdn_dt_bias"], tm)
    o_f, o_b, s_f, s_b = _gdn_scan(prep, *gdn_state, tm)
    if not need_out:
        return None, (s_f, s_b)
    ys = (_pool_mix(p_pool, wl["pool_wbd"], wl["pool_scale"], grid_rows),
          _fourier_mix(p_four, wl["fourier_w"]),
          _conv_mix(p_conv, wl["conv_dw_w"], wl["conv_dw_b"], wl["conv_ln_g"], wl["conv_ln_b"],
                    wl["conv_pw_w"], tm))
    return (ys, o_f, o_b, p_z), (s_f, s_b)


def kernel(x, c, ctx, c_ctx, mod_w, mod_b, norm_w, ffn1_wg, ffn1_wu, ffn1_wd, ffn2_wg, ffn2_wu,
           ffn2_wd, w_in, w_out, pool_w, pool_scale, fourier_w, conv_dw_w, conv_dw_b, conv_ln_g,
           conv_ln_b, conv_pw_w, gdn_conv_w, gdn_a_log, gdn_dt_bias, gdn_norm_w, final_norm_w):
    bsz, seq, d = x.shape
    n_ctx = ctx.shape[1]
    rows = seq // GRID_W
    tm_x, tm_c = 512, n_ctx
    hx, hc = x, ctx
    mods = _modulation(c, c_ctx, mod_w, mod_b)
    zero_state = (jnp.zeros((bsz, GDN_HEAD_DIM, GROUP_W), F32),) * 2
    for l in range(DEPTH):
        last = l == DEPTH - 1
        mx = [m[:, None, :] for m in jnp.split(mods[l, :bsz], N_MOD, axis=-1)]
        mc = [jnp.broadcast_to(m[None], (bsz, 1, d)) for m in jnp.split(mods[l, bsz:bsz + 1], N_MOD, axis=-1)]
        nw = norm_w[l][:, None, :]
        f1 = (ffn1_wg[l].astype(BF16), ffn1_wu[l].astype(BF16), ffn1_wd[l].astype(BF16))
        f2 = (ffn2_wg[l].astype(BF16), ffn2_wu[l].astype(BF16), ffn2_wd[l].astype(BF16))
        w_in_p = _pad_w_in(w_in[l])
        w_out_b = w_out[l].astype(BF16)
        row = lambda a: a.reshape(1, -1)
        wl = dict(pool_wbd=_blockdiag(pool_w[l]).astype(BF16), pool_scale=row(pool_scale[l]),
                  fourier_w=fourier_w[l].astype(BF16), conv_dw_w=conv_dw_w[l], conv_dw_b=row(conv_dw_b[l]),
                  conv_ln_g=row(conv_ln_g[l]), conv_ln_b=row(conv_ln_b[l]),
                  conv_pw_w=conv_pw_w[l].astype(BF16), gdn_conv_w=gdn_conv_w[l],
                  gdn_a_log=gdn_a_log[l], gdn_dt_bias=gdn_dt_bias[l])
        gdn_nw = row(jnp.tile(gdn_norm_w[l], GDN_HEADS))

        hx = _ffn(hx, nw[0], mx[0], mx[1], mx[2], *f1, tm_x)
        hc = _ffn(hc, nw[0], mc[0], mc[1], mc[2], *f1, tm_c)

        px = _inproj(hx, nw[1], mx[3], mx[4], w_in_p, IN_SPLITS, tm_x)
        if last:
            pc = (None,) * 3 + tuple(_inproj(hc, nw[1], mc[3], mc[4], w_in_p[:, GDN_OFF:], IN_SPLITS[3:], tm_c))
        else:
            pc = _inproj(hc, nw[1], mc[3], mc[4], w_in_p, IN_SPLITS, tm_c)

        mix_c, ctx_state = _token_mix(pc, zero_state, None, tm_c, wl, not last)
        mix_x, _ = _token_mix(px, ctx_state, rows, tm_x, wl, True)
        hx = _mix_out(hx, *mix_x, mx[5], gdn_nw, w_out_b, tm_x)
        hx = _ffn(hx, nw[2], mx[6], mx[7], mx[8], *f2, tm_x)
        if not last:
            hc = _mix_out(hc, *mix_c, mc[5], gdn_nw, w_out_b, tm_c)
            hc = _ffn(hc, nw[2], mc[6], mc[7], mc[8], *f2, tm_c)
    return _final_rms(hx, final_norm_w[None, :], tm_x)
```

```python
import functools
import math

import jax
import jax.numpy as jnp
import numpy as np
from jax import lax
from jax.experimental import pallas as pl
from jax.experimental.pallas import tpu as pltpu

D_MODEL = 1024
DEPTH = 4
GRID_W = 64
N_MIXERS = 4
GROUP_W = D_MODEL // N_MIXERS
POOL_WINDOWS = (2, 4, 8, 16)
POOL_GROUPS = 4
POOL_GW = GROUP_W // POOL_GROUPS
FOURIER_GROUPS = 4
FOURIER_GW = GROUP_W // FOURIER_GROUPS
CONV_K = 31
GDN_HEAD_DIM = 64
GDN_HEADS = GROUP_W // GDN_HEAD_DIM
GDN_CONV = 3
GDN_CHUNK = 64
FFN_HIDDEN = 128 * ((8 * D_MODEL // 3 + 127) // 128)
N_MOD = 9
EPS = 1e-6
POOL_OFF = 0
FOURIER_OFF = POOL_OFF + GROUP_W
CONV_OFF = FOURIER_OFF + GROUP_W
GDN_OFF = CONV_OFF + 2 * GROUP_W

LANES = 128
SUBLANES = 8
VMEM_LIMIT = 56 * 1024 * 1024
FFN_CHUNK = 256
AB_PAD = LANES

BF16 = jnp.bfloat16
F32 = jnp.float32


def _params(n_axes):
    return pltpu.CompilerParams(dimension_semantics=("parallel",) * n_axes,
                                vmem_limit_bytes=VMEM_LIMIT)


def _resident(shape):
    nd = len(shape)
    return pl.BlockSpec(shape, lambda *_: (0,) * nd, pipeline_mode=pl.Buffered(1))


def _layer_resident(stack, layer, cols=None, col_block=0):
    _, r, c = stack.shape
    return pl.BlockSpec((None, r, cols or c), lambda *_: (layer, 0, col_block), pipeline_mode=pl.Buffered(1))


def _cast_kernel(x_ref, o_ref):
    o_ref[...] = x_ref[...].astype(BF16)


def _cast_bf16(w, cols=None, rows_per_step=256):
    nl, r, c = w.shape
    cols = cols or c
    spec = pl.BlockSpec((1, rows_per_step, cols), lambda i, j: (i, j, 0))
    return pl.pallas_call(
        _cast_kernel, grid=(nl, r // rows_per_step), in_specs=[spec], out_specs=spec,
        out_shape=jax.ShapeDtypeStruct((nl, r, cols), BF16), compiler_params=_params(2), name="cast_bf16",
    )(w)


def _rms_mod(h, nw, shift, scale):
    ms = jnp.mean(h * h, axis=-1, keepdims=True)
    n = h * lax.rsqrt(ms + EPS) * nw
    return n * (1.0 + scale) + shift


def _ffn_kernel(h_ref, nw_ref, sh_ref, sc_ref, gt_ref, wg_ref, wu_ref, wd_ref, *rest):
    o_ref = rest[-1]
    h = h_ref[0]
    nb = _rms_mod(h, nw_ref[...], sh_ref[0], sc_ref[0]).astype(BF16)
    acc = jnp.zeros(h.shape, F32)
    for f0 in range(0, FFN_HIDDEN, FFN_CHUNK):
        g = jnp.dot(nb, wg_ref[:, f0:f0 + FFN_CHUNK], preferred_element_type=F32)
        u = jnp.dot(nb, wu_ref[:, f0:f0 + FFN_CHUNK], preferred_element_type=F32)
        a = (g * jax.nn.sigmoid(g) * u).astype(BF16)
        acc = acc + jnp.dot(a, wd_ref[f0:f0 + FFN_CHUNK, :], preferred_element_type=F32)
    out = h + (0.5 * gt_ref[0]) * acc
    if len(rest) == 2:
        out = out * lax.rsqrt(jnp.mean(out * out, axis=-1, keepdims=True) + EPS) * rest[0][...]
    o_ref[0] = out


def _ffn(h, nw, shift, scale, gate, weights, layer, tm, final_nw=None):
    b, t, d = h.shape
    vec = pl.BlockSpec((1, 1, d), lambda i, j: (i, 0, 0))
    tok = pl.BlockSpec((1, tm, d), lambda i, j: (i, j, 0))
    extra = [] if final_nw is None else [final_nw]
    return pl.pallas_call(
        _ffn_kernel,
        grid=(b, t // tm),
        in_specs=[tok, _resident((1, d)), vec, vec, vec] + [_layer_resident(w, layer) for w in weights]
        + [_resident((1, d))] * len(extra),
        out_specs=tok,
        out_shape=jax.ShapeDtypeStruct(h.shape, F32),
        compiler_params=_params(2),
        name="ffn",
    )(h, nw, shift, scale, gate, *weights, *extra)


IN_SPLITS = (GROUP_W, GROUP_W, 2 * GROUP_W, 3 * GROUP_W, GROUP_W, AB_PAD)


def _inproj_kernel(h_ref, nw_ref, sh_ref, sc_ref, w_ref, wab_ref, *o_refs):
    nb = _rms_mod(h_ref[0], nw_ref[...], sh_ref[0], sc_ref[0]).astype(BF16)
    off = 0
    for o_ref in o_refs[:-1]:
        wdt = o_ref.shape[-1]
        o_ref[0] = jnp.dot(nb, w_ref[:, off:off + wdt], preferred_element_type=F32)
        off += wdt
    o_refs[-1][0] = jnp.dot(nb, wab_ref[...], preferred_element_type=F32)


def _inproj(h, nw, shift, scale, w_main, w_ab, layer, gdn_only, tm):
    b, t, d = h.shape
    vec = pl.BlockSpec((1, 1, d), lambda i, j: (i, 0, 0))
    tok = pl.BlockSpec((1, tm, d), lambda i, j: (i, j, 0))
    splits = IN_SPLITS[3:] if gdn_only else IN_SPLITS
    w_spec = (_layer_resident(w_main, layer, GDN_OFF, 1) if gdn_only else _layer_resident(w_main, layer))
    return pl.pallas_call(
        _inproj_kernel,
        grid=(b, t // tm),
        in_specs=[tok, _resident((1, d)), vec, vec, w_spec, _layer_resident(w_ab, layer)],
        out_specs=[pl.BlockSpec((1, tm, s), lambda i, j: (i, j, 0)) for s in splits],
        out_shape=[jax.ShapeDtypeStruct((b, t, s), F32) for s in splits],
        compiler_params=_params(2),
        name="inproj",
    )(h, nw, shift, scale, w_main, w_ab)


def _split_bf16(x, parts):
    out = []
    for _ in range(parts - 1):
        p = x.astype(BF16)
        out.append(p)
        x = x - p.astype(F32)
    return out + [x.astype(BF16)]


def _dot01(x, w01, parts=3):
    return sum(jnp.dot(p, w01, preferred_element_type=F32) for p in _split_bf16(x, parts))


def _dot01_left(w01, x, parts=3):
    return sum(jnp.dot(w01, p, preferred_element_type=F32) for p in _split_bf16(x, parts))


def _blockdiag(w):
    g, a, b = w.shape
    return jnp.einsum('gab,gh->gahb', w, jnp.eye(g, dtype=w.dtype)).reshape(g * a, g * b)


def _halo_specs(tm, halo, t, width):
    r = tm // halo
    last = t // halo - 1
    prev = pl.BlockSpec((1, halo, width), lambda i, j: (i, jnp.maximum(j * r - 1, 0), 0))
    cur = pl.BlockSpec((1, tm, width), lambda i, j: (i, j, 0))
    nxt = pl.BlockSpec((1, halo, width), lambda i, j: (i, jnp.minimum((j + 1) * r, last), 0))
    return prev, cur, nxt


def _edge_masked(prev_ref, next_ref):
    j, nj = pl.program_id(1), pl.num_programs(1)
    top = jnp.where(j > 0, prev_ref[0], 0.0)
    bot = jnp.where(j < nj - 1, next_ref[0], 0.0)
    return top, bot


POOL_TM = 1024
POOL_HALO = 512


def _pool_window_sums_1d(u, row_len):
    n = u.shape[0]
    col = lax.broadcasted_iota(jnp.int32, u.shape, 0) & (row_len - 1)
    grp = lax.broadcasted_iota(jnp.int32, u.shape, 1) // POOL_GW

    def back(x, s):
        return jnp.where(col >= s, pltpu.roll(x, s, 0), 0.0)

    def fwd(x, s):
        return jnp.where(col < row_len - s, pltpu.roll(x, n - s, 0), 0.0)

    b = back(u, 1)
    f = u
    out = b + f
    for gi in range(1, len(POOL_WINDOWS)):
        s = POOL_WINDOWS[gi] // 4
        b = b + back(b, s)
        f = f + fwd(f, s)
        out = jnp.where(grp >= gi, b + f, out)
    return out


def _pool_counts(idx, extent, halfw):
    return jnp.minimum(idx + halfw, extent) - jnp.maximum(idx - halfw, 0)


def _pool_finish(mean, u, wbd_ref, scale_ref, o_ref):
    d = (mean - u).astype(BF16)
    o_ref[0] = jnp.dot(d, wbd_ref[...], preferred_element_type=F32) * scale_ref[...]


def _pool_grid_kernel(prev_ref, cur_ref, next_ref, wbd_ref, scale_ref, o_ref, *, n_rows):
    tm = cur_ref.shape[1]
    top, bot = _edge_masked(prev_ref, next_ref)
    u = cur_ref[0]
    ext = jnp.concatenate([top, u, bot], axis=0)
    cw = _pool_window_sums_1d(ext, GRID_W)
    grp = lax.broadcasted_iota(jnp.int32, (tm, GROUP_W), 1) // POOL_GW
    arr, off = cw, 0
    tot = None
    for gi, w in enumerate(POOL_WINDOWS):
        sh = GRID_W * max(w // 4, 1) if gi else GRID_W
        if gi == 0:
            arr = arr[:-sh] + arr[sh:]
            off = sh
        else:
            arr = arr[:-2 * sh] + arr[2 * sh:]
            off = off + sh
        centre = arr[POOL_HALO - off:POOL_HALO - off + tm]
        tot = centre if tot is None else jnp.where(grp >= gi, centre, tot)
    tok = pl.program_id(1) * tm + lax.broadcasted_iota(jnp.int32, (tm, GROUP_W), 0)
    halfw = jnp.left_shift(1, grp)
    cnt = (_pool_counts(tok // GRID_W, n_rows, halfw) * _pool_counts(tok & (GRID_W - 1), GRID_W, halfw))
    _pool_finish(tot / cnt.astype(F32), u, wbd_ref, scale_ref, o_ref)


def _pool_seq_kernel(u_ref, wbd_ref, scale_ref, o_ref):
    u = u_ref[0]
    n = u.shape[0]
    tot = _pool_window_sums_1d(u, n)
    grp = lax.broadcasted_iota(jnp.int32, u.shape, 1) // POOL_GW
    tok = lax.broadcasted_iota(jnp.int32, u.shape, 0)
    cnt = _pool_counts(tok, n, jnp.left_shift(1, grp))
    _pool_finish(tot / cnt.astype(F32), u, wbd_ref, scale_ref, o_ref)


def _pool_mix(u, wbd, scale, grid_rows):
    b, t, w = u.shape
    out_shape = jax.ShapeDtypeStruct(u.shape, F32)
    if grid_rows is None:
        tok = pl.BlockSpec((1, t, w), lambda i: (i, 0, 0))
        return pl.pallas_call(
            _pool_seq_kernel, grid=(b,),
            in_specs=[tok, _resident(wbd.shape), _resident(scale.shape)],
            out_specs=tok, out_shape=out_shape, compiler_params=_params(1), name="pool_seq",
        )(u, wbd, scale)
    prev, cur, nxt = _halo_specs(POOL_TM, POOL_HALO, t, w)
    return pl.pallas_call(
        functools.partial(_pool_grid_kernel, n_rows=grid_rows),
        grid=(b, t // POOL_TM),
        in_specs=[prev, cur, nxt, _resident(wbd.shape), _resident(scale.shape)],
        out_specs=cur, out_shape=out_shape, compiler_params=_params(2), name="pool_grid",
    )(u, u, u, wbd, scale)


CONV_TM = 512
CONV_HALO = 16
CONV_SUB = 64


def _glu(x):
    return x[:, :GROUP_W] * jax.nn.sigmoid(x[:, GROUP_W:])


def _conv_kernel(prev_ref, cur_ref, next_ref, dww_ref, dwb_ref, lng_ref, lnb_ref, pw_ref, o_ref,
                 ext_ref, sh_ref):
    tm = cur_ref.shape[1]
    top, bot = _edge_masked(prev_ref, next_ref)
    ext_ref[0:CONV_HALO] = _glu(top)
    ext_ref[CONV_HALO:CONV_HALO + tm] = _glu(cur_ref[0])
    ext_ref[CONV_HALO + tm:] = _glu(bot)
    n_sh = sh_ref.shape[1]
    for s in range(SUBLANES):
        sh_ref[s] = ext_ref[s:s + n_sh, :]
    base = CONV_HALO - CONV_K // 2
    for r0 in range(0, tm, CONV_SUB):
        acc = jnp.zeros((CONV_SUB, GROUP_W), F32)
        for k in range(CONV_K):
            a, s = divmod(base + k, SUBLANES)
            acc = acc + dww_ref[k:k + 1, :] * sh_ref[s, r0 + a * SUBLANES:r0 + a * SUBLANES + CONV_SUB, :]
        h = acc + dwb_ref[...]
        mu = jnp.mean(h, axis=-1, keepdims=True)
        var = jnp.mean(jnp.square(h - mu), axis=-1, keepdims=True)
        h = (h - mu) * lax.rsqrt(var + EPS) * lng_ref[...] + lnb_ref[...]
        h = (h * jax.nn.sigmoid(h)).astype(BF16)
        o_ref[0, r0:r0 + CONV_SUB, :] = jnp.dot(h, pw_ref[...], preferred_element_type=F32)


def _conv_mix(u2, dw_w, dw_b, ln_g, ln_b, pw_w, tm):
    b, t, w2 = u2.shape
    prev, cur, nxt = _halo_specs(tm, CONV_HALO, t, w2)
    row = _resident((1, GROUP_W))
    return pl.pallas_call(
        _conv_kernel,
        grid=(b, t // tm),
        in_specs=[prev, cur, nxt, _resident(dw_w.shape), row, row, row, _resident(pw_w.shape)],
        out_specs=pl.BlockSpec((1, tm, GROUP_W), lambda i, j: (i, j, 0)),
        out_shape=jax.ShapeDtypeStruct((b, t, GROUP_W), F32),
        scratch_shapes=[pltpu.VMEM((tm + 2 * CONV_HALO, GROUP_W), F32),
                        pltpu.VMEM((SUBLANES, tm + 2 * CONV_HALO - SUBLANES, GROUP_W), F32)],
        compiler_params=_params(2), name="conv_mix",
    )(u2, u2, u2, dw_w, dw_b, ln_g, ln_b, pw_w)


FOUR_N1 = 128
FOUR_TT = 8
FOUR_KT = 8


def _hilo(w):
    w = jnp.asarray(w, F32)
    hi = w.astype(BF16)
    return hi, (w - hi.astype(F32)).astype(BF16)


def _dot_hp(x, wh, wl):
    xh = x.astype(BF16)
    xl = (x - xh.astype(F32)).astype(BF16)
    return (jnp.dot(xh, wh, preferred_element_type=F32) + jnp.dot(xh, wl, preferred_element_type=F32)
            + jnp.dot(xl, wh, preferred_element_type=F32))


def _dot_hp_left(wh, wl, x):
    xh = x.astype(BF16)
    xl = (x - xh.astype(F32)).astype(BF16)
    return (jnp.dot(wh, xh, preferred_element_type=F32) + jnp.dot(wl, xh, preferred_element_type=F32)
            + jnp.dot(wh, xl, preferred_element_type=F32))


def _cos_sin(n, rows=None, cols=None):
    r = np.arange(n if rows is None else rows, dtype=np.int64)
    c = np.arange(n if cols is None else cols, dtype=np.int64)
    ang = 2.0 * np.pi * ((np.outer(r, c) % n).astype(np.float64) / n)
    return np.cos(ang), np.sin(ang)


def _channel_dft():
    c, s = _cos_sin(FOURIER_GW)
    eye = np.eye(FOURIER_GROUPS)
    return np.concatenate([np.kron(eye, c), -np.kron(eye, s)], axis=1)


def _fourier_a_kernel(u_ref, cdh_ref, cdl_ref, f1h_ref, f1l_ref, twc_ref, tws_ref, o_ref):
    for j in range(FOUR_TT):
        x = u_ref[0, :, j * GROUP_W:(j + 1) * GROUP_W]
        z = _dot_hp(x, cdh_ref[...], cdl_ref[...])
        zz = jnp.concatenate([z[:, :GROUP_W], z[:, GROUP_W:]], axis=0)
        a = _dot_hp_left(f1h_ref[...], f1l_ref[...], zz)
        ar, ai = a[:FOUR_N1], a[FOUR_N1:]
        c = jnp.concatenate([twc_ref[j]] * (GROUP_W // LANES), axis=1)
        s = jnp.concatenate([tws_ref[j]] * (GROUP_W // LANES), axis=1)
        o_ref[0, 0, j] = ar * c + ai * s
        o_ref[0, 1, j] = ai * c - ar * s


def _fourier_b_kernel(g_ref, f2h_ref, f2l_ref, fw_ref, o_ref, *, scale):
    for kk in range(FOUR_KT):
        sl = slice(kk * GROUP_W, (kk + 1) * GROUP_W)
        gm = jnp.concatenate([g_ref[0, 0, :, sl], g_ref[0, 1, :, sl]], axis=0)
        f = _dot_hp_left(f2h_ref[...], f2l_ref[...], gm) * scale
        o_ref[0, :, sl] = jnp.dot(f.astype(BF16), fw_ref[...], preferred_element_type=F32)


def _fourier_small_kernel(u_ref, cdh_ref, cdl_ref, fh_ref, fl_ref, fw_ref, o_ref, *, scale):
    z = _dot_hp(u_ref[0], cdh_ref[...], cdl_ref[...])
    zz = jnp.concatenate([z[:, :GROUP_W], z[:, GROUP_W:]], axis=0)
    f = _dot_hp_left(fh_ref[...], fl_ref[...], zz) * scale
    o_ref[0] = jnp.dot(f.astype(BF16), fw_ref[...], preferred_element_type=F32)


def _fourier_mix(u, fw):
    b, n, w = u.shape
    scale = 1.0 / math.sqrt(n * FOURIER_GW)
    cdh, cdl = _hilo(_channel_dft())
    if n != FOUR_N1 * FOUR_N1:
        c, s = _cos_sin(n)
        fh, fl = _hilo(np.concatenate([c, s], axis=1))
        tok = pl.BlockSpec((1, n, w), lambda i: (i, 0, 0))
        return pl.pallas_call(
            functools.partial(_fourier_small_kernel, scale=scale), grid=(b,),
            in_specs=[tok] + [_resident(a.shape) for a in (cdh, cdl, fh, fl, fw)],
            out_specs=tok, out_shape=jax.ShapeDtypeStruct(u.shape, F32),
            compiler_params=_params(1), name="fourier_small",
        )(u, cdh, cdl, fh, fl, fw)
    n1 = FOUR_N1
    c1, s1 = _cos_sin(n1)
    f1h, f1l = _hilo(np.block([[c1, s1], [-s1, c1]]))
    f2h, f2l = _hilo(np.concatenate([c1, s1], axis=1))
    twc, tws = _cos_sin(n, rows=n1, cols=n1)
    twc = jnp.broadcast_to(jnp.asarray(twc, F32)[:, :, None], (n1, n1, LANES))
    tws = jnp.broadcast_to(jnp.asarray(tws, F32)[:, :, None], (n1, n1, LANES))
    tw_spec = pl.BlockSpec((FOUR_TT, n1, LANES), lambda i, j: (j, 0, 0))
    g = pl.pallas_call(
        _fourier_a_kernel, grid=(b, n1 // FOUR_TT),
        in_specs=[pl.BlockSpec((1, n1, FOUR_TT * w), lambda i, j: (i, 0, j))]
        + [_resident(a.shape) for a in (cdh, cdl, f1h, f1l)] + [tw_spec, tw_spec],
        out_specs=pl.BlockSpec((1, 2, FOUR_TT, n1, w), lambda i, j: (i, 0, j, 0, 0)),
        out_shape=jax.ShapeDtypeStruct((b, 2, n1, n1, w), F32),
        compiler_params=_params(2), name="fourier_a",
    )(u.reshape(b, n1, n1 * w), cdh, cdl, f1h, f1l, twc, tws)
    y = pl.pallas_call(
        functools.partial(_fourier_b_kernel, scale=scale), grid=(b, n1 // FOUR_KT),
        in_specs=[pl.BlockSpec((1, 2, n1, FOUR_KT * w), lambda i, j: (i, 0, 0, j))]
        + [_resident(a.shape) for a in (f2h, f2l, fw)],
        out_specs=pl.BlockSpec((1, n1, FOUR_KT * w), lambda i, j: (i, 0, j)),
        out_shape=jax.ShapeDtypeStruct((b, n1, n1 * w), F32),
        compiler_params=_params(2), name="fourier_b",
    )(g.reshape(b, 2, n1, n1 * w), f2h, f2l, fw)
    return y.reshape(b, n, w)


GDN_C = GDN_CHUNK
GDN_PREP_TM = 512
GDN_SCAN_TM = 512
GDN_SUB = 64
N_GATE = 4 * GDN_HEADS


def _softplus(x):
    return jnp.maximum(x, 0.0) + jnp.log1p(jnp.exp(-jnp.abs(x)))


def _gdn_consts():
    i = np.arange(GDN_C)
    lower = np.concatenate([i[None, :] <= i[:, None], i[None, :] >= i[:, None]], axis=0)
    same = np.ones((GDN_C, GDN_C), bool)
    expand = np.zeros((AB_PAD, 4 * GROUP_W), np.float32)
    for s in range(4):
        for h in range(GDN_HEADS):
            c0 = s * GROUP_W + h * GDN_HEAD_DIM
            expand[s * GDN_HEADS + h, c0:c0 + GDN_HEAD_DIM] = 1.0
    lane = np.arange(GROUP_W)
    headones = (lane[:, None] // GDN_HEAD_DIM) == (lane[None, :] // GDN_HEAD_DIM)
    as_bf = lambda a: jnp.asarray(a, F32).astype(BF16)
    return as_bf(lower), as_bf(same), as_bf(expand), as_bf(headones)


def _bd(x, bdmask):
    xb = x.astype(BF16)
    return jnp.where(bdmask, jnp.concatenate([xb] * GDN_HEADS, axis=0), jnp.zeros((), BF16))


def _bd_mask():
    return (lax.broadcasted_iota(jnp.int32, (GROUP_W, GROUP_W), 0) // GDN_HEAD_DIM
            == lax.broadcasted_iota(jnp.int32, (GROUP_W, GROUP_W), 1) // GDN_HEAD_DIM)


def _dot(a, b):
    return jnp.dot(a.astype(BF16), b.astype(BF16), preferred_element_type=F32)


def _diag_blocks(full):
    head = (lax.broadcasted_iota(jnp.int32, (GDN_HEAD_DIM, full.shape[1]), 1) // GDN_HEAD_DIM) % GDN_HEADS
    out = None
    for h in range(GDN_HEADS):
        blk = jnp.where(head == h, full[h * GDN_HEAD_DIM:(h + 1) * GDN_HEAD_DIM], 0.0)
        out = blk if out is None else out + blk
    return out


def _tri_inverse(ms, eye, level_masks, bdmask):
    ds = [eye - jnp.where(level_masks[0], m, 0.0) for m in ms]
    for mask in level_masks[1:]:
        es = [_dot(jnp.where(mask, m, 0.0), _bd(d, bdmask)) for m, d in zip(ms, ds)]
        fs = [_dot(d, _bd(e, bdmask)) for d, e in zip(ds, es)]
        ds = [d - f for d, f in zip(ds, fs)]
    return ds


def _gdn_prep_kernel(prev_ref, cur_ref, next_ref, ab_ref, cw_ref, alog_ref, dtb_ref,
                     tri_ref, same_ref, exp_ref, ones_ref,
                     pf_ref, qpf_ref, bmf_ref, o0f_ref, df_ref, pb_ref, qpb_ref, bmb_ref, o0b_ref, db_ref,
                     ext_ref, q_scr, k_scr, v_scr):
    tm = cur_ref.shape[1]
    halo = prev_ref.shape[1]
    c = GDN_C
    chunks = range(tm // c)
    rs = [slice(ci * c, (ci + 1) * c) for ci in chunks]
    top, bot = _edge_masked(prev_ref, next_ref)
    ext_ref[0:halo] = top
    ext_ref[halo:halo + tm] = cur_ref[0]
    ext_ref[halo + tm:] = bot
    base = halo - GDN_CONV // 2
    for r0 in range(0, tm, GDN_SUB):
        acc = jnp.zeros((GDN_SUB, 3 * GROUP_W), F32)
        for t in range(GDN_CONV):
            acc = acc + cw_ref[t:t + 1, :] * ext_ref[r0 + base + t:r0 + base + t + GDN_SUB, :]
        x = acc * jax.nn.sigmoid(acc)
        rows = slice(r0, r0 + GDN_SUB)
        q_scr[rows, :] = x[:, :GROUP_W]
        k_scr[rows, :] = x[:, GROUP_W:2 * GROUP_W]
        v_scr[rows, :] = x[:, 2 * GROUP_W:]
    qa, ka = q_scr[...], k_scr[...]
    q_scr[...] = qa * lax.rsqrt(_dot01(qa * qa, ones_ref[...], 2) + EPS) * (GDN_HEAD_DIM ** -0.5)
    k_scr[...] = ka * lax.rsqrt(_dot01(ka * ka, ones_ref[...], 2) + EPS)

    ab = ab_ref[0]
    lane = lax.broadcasted_iota(jnp.int32, ab.shape, 1)
    g = -jnp.exp(alog_ref[...]) * _softplus(ab + dtb_ref[...])
    gate = jnp.where(lane < 2 * GDN_HEADS, g, jax.nn.sigmoid(ab))
    cums = [_dot01_left(tri_ref[...], gate[r]) for r in rs]
    lane_c = lax.broadcasted_iota(jnp.int32, (c, AB_PAD), 1)
    cum = jnp.concatenate([jnp.where(lane_c < GDN_HEADS, cm[:c], cm[c:]) for cm in cums], axis=0)
    wide = _dot01(jnp.where(lane < 2 * GDN_HEADS, cum, gate), exp_ref[...])
    gcs = (wide[:, :GROUP_W], wide[:, GROUP_W:2 * GROUP_W])
    betas = (wide[:, 2 * GROUP_W:3 * GROUP_W], wide[:, 3 * GROUP_W:])

    ii = lax.broadcasted_iota(jnp.int32, (c, GROUP_W), 0)
    jj = lax.broadcasted_iota(jnp.int32, (c, GROUP_W), 1) & (c - 1)
    diag2 = jnp.concatenate([ii == jj] * 2, axis=1)
    rows_gc = [_dot01_left(same_ref[...], jnp.where(diag2, wide[r, :2 * GROUP_W], 0.0)) for r in rs]
    gams = ([jnp.where(ii >= jj, jnp.exp(jnp.minimum(gcs[0][r] - rg[:, :GROUP_W], 0.0)), 0.0)
             for r, rg in zip(rs, rows_gc)],
            [jnp.where(ii <= jj, jnp.exp(jnp.minimum(gcs[1][r] - rg[:, GROUP_W:], 0.0)), 0.0)
             for r, rg in zip(rs, rows_gc)])
    eye = (ii == jj).astype(F32)
    stricts = (ii > jj, ii < jj)
    levels = [((ii // (2 * s)) == (jj // (2 * s))) & ((ii // s) != (jj // s)) for s in (1, 2, 4, 8, 16, 32)]
    bdmask = _bd_mask()
    outs = ((pf_ref, qpf_ref, bmf_ref, o0f_ref, df_ref), (pb_ref, qpb_ref, bmb_ref, o0b_ref, db_ref))

    qs = [q_scr[r, :] for r in rs]
    ks = [k_scr[r, :] for r in rs]
    vs = [v_scr[r, :] for r in rs]
    kqs = [lax.dot_general(jnp.concatenate([k, q], axis=0).astype(BF16), _bd(k, bdmask),
                           (((1,), (1,)), ((), ())), preferred_element_type=F32) for k, q in zip(ks, qs)]
    for di in range(2):
        gc_l = [gcs[di][r] for r in rs]
        gam_l = gams[di]
        beta_l = [betas[di][r] for r in rs]
        ms = [jnp.where(stricts[di], beta * kq[:c] * gam, 0.0) for beta, kq, gam in zip(beta_l, kqs, gam_l)]
        ts = _tri_inverse(ms, eye, levels, bdmask)
        egcs = [jnp.exp(gc) for gc in gc_l]
        uws = [_dot(t, jnp.concatenate([_bd(v * beta, bdmask), _bd(k * beta * egc, bdmask)], axis=1))
               for t, v, k, beta, egc in zip(ts, vs, ks, beta_l, egcs)]
        aqks = [kq[c:] * gam for kq, gam in zip(kqs, gam_l)]
        ows = [_dot(aqk, jnp.concatenate([_bd(uw[:, :GROUP_W], bdmask), _bd(uw[:, GROUP_W:], bdmask)], axis=1))
               for aqk, uw in zip(aqks, uws)]
        p_ref, qp_ref, bm_ref, o0_ref, d_ref = outs[di]
        for ci, r in enumerate(rs):
            gc, uw, ow = gc_l[ci], uws[ci], ows[ci]
            glast = gc[0:1] if di else gc[c - 1:c]
            kdec = ks[ci] * jnp.exp(glast - gc)
            bp = _diag_blocks(lax.dot_general(kdec.astype(BF16), uw.astype(BF16), (((0,), (0,)), ((), ())),
                                              preferred_element_type=F32))
            bm_ref[0, r, :] = bp[:, :GROUP_W]
            p_ref[0, r, :] = bp[:, GROUP_W:].astype(BF16)
            o0_ref[0, r, :] = ow[:, :GROUP_W]
            qp_ref[0, r, :] = (qs[ci] * egcs[ci] - ow[:, GROUP_W:]).astype(BF16)
            d_ref[0, ci:ci + 1, :] = jnp.exp(glast)


def _gdn_prep(qkv, ab, conv_w, a_log, dt_bias, tm):
    b, t, w3 = qkv.shape
    halo = 8
    nck = tm // GDN_C
    prev, cur, nxt = _halo_specs(tm, halo, t, w3)
    pad = lambda a: jnp.pad(a.reshape(1, -1), ((0, 0), (0, AB_PAD - a.size)))
    consts = _gdn_consts()
    tok = pl.BlockSpec((1, tm, GROUP_W), lambda i, j: (i, j, 0))
    dec = pl.BlockSpec((1, nck, GROUP_W), lambda i, j: (i, j, 0))
    tok_shape = lambda dt: jax.ShapeDtypeStruct((b, t, GROUP_W), dt)
    dir_specs = [tok, tok, tok, tok, dec]
    dir_shapes = [tok_shape(BF16), tok_shape(BF16), tok_shape(F32), tok_shape(F32),
                  jax.ShapeDtypeStruct((b, t // GDN_C, GROUP_W), F32)]
    return pl.pallas_call(
        _gdn_prep_kernel,
        grid=(b, t // tm),
        in_specs=[prev, cur, nxt, pl.BlockSpec((1, tm, AB_PAD), lambda i, j: (i, j, 0)),
                  _resident(conv_w.shape), _resident((1, AB_PAD)), _resident((1, AB_PAD))]
        + [_resident(cst.shape) for cst in consts],
        out_specs=dir_specs * 2,
        out_shape=dir_shapes * 2,
        scratch_shapes=[pltpu.VMEM((tm + 2 * halo, w3), F32)] + [pltpu.VMEM((tm, GROUP_W), F32)] * 3,
        compiler_params=_params(2), name="gdn_prep",
    )(qkv, qkv, qkv, ab, conv_w, pad(a_log), pad(dt_bias), *consts)


def _gdn_scan_kernel(pf_ref, qpf_ref, bmf_ref, o0f_ref, df_ref, pb_ref, qpb_ref, bmb_ref, o0b_ref, db_ref,
                     s0f_ref, s0b_ref, of_ref, ob_ref, sf_ref, sb_ref):
    nb, tm = pf_ref.shape[0], pf_ref.shape[1]
    nc = tm // GDN_C

    @pl.when(pl.program_id(0) == 0)
    def _():
        sf_ref[...] = s0f_ref[...]
        sb_ref[...] = s0b_ref[...]

    bdmask = _bd_mask()
    fwd = (pf_ref, qpf_ref, bmf_ref, o0f_ref, df_ref, of_ref)
    bwd = (pb_ref, qpb_ref, bmb_ref, o0b_ref, db_ref, ob_ref)
    chains = [(fwd, bi, False) for bi in range(nb)] + [(bwd, bi, True) for bi in range(nb)]
    states = [sf_ref[bi] for bi in range(nb)] + [sb_ref[bi] for bi in range(nb)]
    for step in range(nc):
        lhs, rows = [], []
        for (p_ref, qp_ref, _, _, _, _), bi, rev in chains:
            ci = nc - 1 - step if rev else step
            r = slice(ci * GDN_C, (ci + 1) * GDN_C)
            rows.append((ci, r))
            lhs.append(jnp.concatenate([p_ref[bi, r, :], qp_ref[bi, r, :]], axis=0))
        res = [jnp.dot(a, _bd(s, bdmask), preferred_element_type=F32) for a, s in zip(lhs, states)]
        new_states = []
        for (_, _, bm_ref, o0_ref, d_ref, o_ref), bi, _ in chains:
            n = len(new_states)
            ci, r = rows[n]
            o_ref[bi, r, :] = o0_ref[bi, r, :] + res[n][GDN_C:]
            new_states.append(d_ref[bi, ci:ci + 1, :] * states[n] - res[n][:GDN_C] + bm_ref[bi, r, :])
        states = new_states
    for bi in range(nb):
        sf_ref[bi] = states[bi]
        sb_ref[bi] = states[nb + bi]


def _gdn_scan(prep, s0f, s0b, tm):
    b, t, w = prep[0].shape
    nj = t // tm
    nck = tm // GDN_C
    fwd = pl.BlockSpec((b, tm, w), lambda j: (0, j, 0))
    bwd = pl.BlockSpec((b, tm, w), lambda j: (0, nj - 1 - j, 0))
    dfwd = pl.BlockSpec((b, nck, w), lambda j: (0, j, 0))
    dbwd = pl.BlockSpec((b, nck, w), lambda j: (0, nj - 1 - j, 0))
    st = pl.BlockSpec((b, GDN_HEAD_DIM, w), lambda j: (0, 0, 0))
    st_shape = jax.ShapeDtypeStruct((b, GDN_HEAD_DIM, w), F32)
    return pl.pallas_call(
        _gdn_scan_kernel,
        grid=(nj,),
        in_specs=[fwd] * 4 + [dfwd] + [bwd] * 4 + [dbwd] + [st, st],
        out_specs=[fwd, bwd, st, st],
        out_shape=[jax.ShapeDtypeStruct((b, t, w), F32)] * 2 + [st_shape] * 2,
        compiler_params=pltpu.CompilerParams(dimension_semantics=("arbitrary",),
                                             vmem_limit_bytes=VMEM_LIMIT),
        name="gdn_scan",
    )(*prep, s0f, s0b)


def _mix_out_kernel(h_ref, yp_ref, yf_ref, yc_ref, of_ref, ob_ref, z_ref, gt_ref, nw_ref, ones_ref,
                    w_ref, o_ref):
    o = of_ref[0] + ob_ref[0]
    ms = _dot01(o * o, ones_ref[...], 2) * (1.0 / GDN_HEAD_DIM)
    z = z_ref[0]
    yg = o * lax.rsqrt(ms + EPS) * nw_ref[...] * (z * jax.nn.sigmoid(z))
    y = jnp.zeros(h_ref.shape[1:], F32)
    for gi, part in enumerate((yp_ref[0], yf_ref[0], yc_ref[0], yg)):
        y = y + jnp.dot(part.astype(BF16), w_ref[gi * GROUP_W:(gi + 1) * GROUP_W, :],
                        preferred_element_type=F32)
    o_ref[0] = h_ref[0] + gt_ref[0] * y


def _mix_out(h, ys, o_f, o_b, z, gate, gdn_nw, w_out, layer, tm):
    b, t, d = h.shape
    tok = pl.BlockSpec((1, tm, d), lambda i, j: (i, j, 0))
    grp = pl.BlockSpec((1, tm, GROUP_W), lambda i, j: (i, j, 0))
    ones = _gdn_consts()[3]
    return pl.pallas_call(
        _mix_out_kernel,
        grid=(b, t // tm),
        in_specs=[tok] + [grp] * 6 + [pl.BlockSpec((1, 1, d), lambda i, j: (i, 0, 0)),
                                     _resident((1, GROUP_W)), _resident(ones.shape), _layer_resident(w_out, layer)],
        out_specs=tok,
        out_shape=jax.ShapeDtypeStruct(h.shape, F32),
        compiler_params=_params(2), name="mix_out",
    )(h, *ys, o_f, o_b, z, gate, gdn_nw, ones, w_out)


MOD_ROWS = 8
MOD_TN = 1152


def _mod_kernel(c_ref, w_ref, b_ref, o_ref):
    cv = c_ref[...]
    a = (cv * jax.nn.sigmoid(cv)).astype(BF16)
    o_ref[0] = jnp.dot(a, w_ref[0].astype(BF16), preferred_element_type=F32) + b_ref[0]


def _modulation(c, c_ctx, mod_w, mod_b):
    nl, d, n = mod_w.shape
    cond = jnp.concatenate([c, c_ctx[None, :]], axis=0)
    cond = jnp.pad(cond, ((0, MOD_ROWS - cond.shape[0]), (0, 0)))
    return pl.pallas_call(
        _mod_kernel,
        grid=(nl, n // MOD_TN),
        in_specs=[_resident(cond.shape), pl.BlockSpec((1, d, MOD_TN), lambda i, j: (i, 0, j)),
                  pl.BlockSpec((1, 1, MOD_TN), lambda i, j: (i, 0, j))],
        out_specs=pl.BlockSpec((1, MOD_ROWS, MOD_TN), lambda i, j: (i, 0, j)),
        out_shape=jax.ShapeDtypeStruct((nl, MOD_ROWS, n), F32),
        compiler_params=_params(2), name="modulation",
    )(cond, mod_w, mod_b[:, None, :])


def _token_mix(p, gdn_state, grid_rows, tm, wl, need_out):
    p_pool, p_four, p_conv, p_qkv, p_z, p_ab = p
    prep = _gdn_prep(p_qkv, p_ab, wl["gdn_conv_w"], wl["gdn_a_log"], wl["gdn_dt_bias"], tm)
    o_f, o_b, s_f, s_b = _gdn_scan(prep, *gdn_state, tm)
    if not need_out:
        return None, (s_f, s_b)
    ys = (_pool_mix(p_pool, wl["pool_wbd"], wl["pool_scale"], grid_rows),
          _fourier_mix(p_four, wl["fourier_w"]),
          _conv_mix(p_conv, wl["conv_dw_w"], wl["conv_dw_b"], wl["conv_ln_g"], wl["conv_ln_b"],
                    wl["conv_pw_w"], tm))
    return (ys, o_f, o_b, p_z), (s_f, s_b)


def kernel(x, c, ctx, c_ctx, mod_w, mod_b, norm_w, ffn1_wg, ffn1_wu, ffn1_wd, ffn2_wg, ffn2_wu,
           ffn2_wd, w_in, w_out, pool_w, pool_scale, fourier_w, conv_dw_w, conv_dw_b, conv_ln_g,
           conv_ln_b, conv_pw_w, gdn_conv_w, gdn_a_log, gdn_dt_bias, gdn_norm_w, final_norm_w):
    bsz, seq, d = x.shape
    n_ctx = ctx.shape[1]
    rows = seq // GRID_W
    tm_x, tm_c = 512, n_ctx
    hx, hc = x, ctx
    mods = _modulation(c, c_ctx, mod_w, mod_b)
    zero_state = (jnp.zeros((bsz, GDN_HEAD_DIM, GROUP_W), F32),) * 2
    f1 = tuple(_cast_bf16(w) for w in (ffn1_wg, ffn1_wu, ffn1_wd))
    f2 = tuple(_cast_bf16(w) for w in (ffn2_wg, ffn2_wu, ffn2_wd))
    w_main = _cast_bf16(w_in, cols=2 * GDN_OFF)
    w_ab = jnp.pad(w_in[:, :, 2 * GDN_OFF:], ((0, 0), (0, 0), (0, AB_PAD - N_GATE))).astype(BF16)
    w_out_b = _cast_bf16(w_out)
    for l in range(DEPTH):
        last = l == DEPTH - 1
        mx = [m[:, None, :] for m in jnp.split(mods[l, :bsz], N_MOD, axis=-1)]
        mc = [jnp.broadcast_to(m[None], (bsz, 1, d)) for m in jnp.split(mods[l, bsz:bsz + 1], N_MOD, axis=-1)]
        nw = norm_w[l][:, None, :]
        row = lambda a: a.reshape(1, -1)
        wl = dict(pool_wbd=_blockdiag(pool_w[l]).astype(BF16), pool_scale=row(pool_scale[l]),
                  fourier_w=fourier_w[l].astype(BF16), conv_dw_w=conv_dw_w[l], conv_dw_b=row(conv_dw_b[l]),
                  conv_ln_g=row(conv_ln_g[l]), conv_ln_b=row(conv_ln_b[l]),
                  conv_pw_w=conv_pw_w[l].astype(BF16), gdn_conv_w=gdn_conv_w[l],
                  gdn_a_log=gdn_a_log[l], gdn_dt_bias=gdn_dt_bias[l])
        gdn_nw = row(jnp.tile(gdn_norm_w[l], GDN_HEADS))

        hx = _ffn(hx, nw[0], mx[0], mx[1], mx[2], f1, l, tm_x)
        hc = _ffn(hc, nw[0], mc[0], mc[1], mc[2], f1, l, tm_c)

        px = _inproj(hx, nw[1], mx[3], mx[4], w_main, w_ab, l, False, tm_x)
        pc = _inproj(hc, nw[1], mc[3], mc[4], w_main, w_ab, l, last, tm_c)
        if last:
            pc = (None,) * 3 + tuple(pc)

        mix_c, ctx_state = _token_mix(pc, zero_state, None, tm_c, wl, not last)
        mix_x, _ = _token_mix(px, ctx_state, rows, tm_x, wl, True)
        hx = _mix_out(hx, *mix_x, mx[5], gdn_nw, w_out_b, l, tm_x)
        hx = _ffn(hx, nw[2], mx[6], mx[7], mx[8], f2, l, tm_x, final_norm_w[None, :] if last else None)
        if not last:
            hc = _mix_out(hc, *mix_c, mc[5], gdn_nw, w_out_b, l, tm_c)
            hc = _ffn(hc, nw[2], mc[6], mc[7], mc[8], f2, l, tm_c)
    return hx
```

```python
import functools
import math

import jax
import jax.numpy as jnp
import numpy as np
from jax import lax
from jax.experimental import pallas as pl
from jax.experimental.pallas import tpu as pltpu

D_MODEL = 1024
DEPTH = 4
GRID_W = 64
N_MIXERS = 4
GROUP_W = D_MODEL // N_MIXERS
POOL_WINDOWS = (2, 4, 8, 16)
POOL_GROUPS = 4
POOL_GW = GROUP_W // POOL_GROUPS
FOURIER_GROUPS = 4
FOURIER_GW = GROUP_W // FOURIER_GROUPS
CONV_K = 31
GDN_HEAD_DIM = 64
GDN_HEADS = GROUP_W // GDN_HEAD_DIM
GDN_CONV = 3
GDN_CHUNK = 64
FFN_HIDDEN = 128 * ((8 * D_MODEL // 3 + 127) // 128)
N_MOD = 9
EPS = 1e-6
POOL_OFF = 0
FOURIER_OFF = POOL_OFF + GROUP_W
CONV_OFF = FOURIER_OFF + GROUP_W
GDN_OFF = CONV_OFF + 2 * GROUP_W

LANES = 128
SUBLANES = 8
VMEM_LIMIT = 56 * 1024 * 1024
FFN_CHUNK = 256
AB_PAD = LANES

BF16 = jnp.bfloat16
F32 = jnp.float32


def _params(n_axes):
    return pltpu.CompilerParams(dimension_semantics=("parallel",) * n_axes,
                                vmem_limit_bytes=VMEM_LIMIT)


def _resident(shape):
    nd = len(shape)
    return pl.BlockSpec(shape, lambda *_: (0,) * nd, pipeline_mode=pl.Buffered(1))


def _layer_resident(stack, layer, cols=None, col_block=0):
    _, r, c = stack.shape
    return pl.BlockSpec((None, r, cols or c), lambda *_: (layer, 0, col_block), pipeline_mode=pl.Buffered(1))


def _cast_kernel(x_ref, o_ref):
    o_ref[...] = x_ref[...].astype(BF16)


def _cast_bf16(w, cols=None, rows_per_step=256):
    nl, r, c = w.shape
    cols = cols or c
    spec = pl.BlockSpec((1, rows_per_step, cols), lambda i, j: (i, j, 0))
    return pl.pallas_call(
        _cast_kernel, grid=(nl, r // rows_per_step), in_specs=[spec], out_specs=spec,
        out_shape=jax.ShapeDtypeStruct((nl, r, cols), BF16), compiler_params=_params(2), name="cast_bf16",
    )(w)


def _rms_mod(h, nw, shift, scale):
    ms = jnp.mean(h * h, axis=-1, keepdims=True)
    n = h * lax.rsqrt(ms + EPS) * nw
    return n * (1.0 + scale) + shift


def _ffn_apply(h, nw_ref, sh_ref, sc_ref, gt_ref, wg_ref, wu_ref, wd_ref, final_nw_ref=None):
    nb = _rms_mod(h, nw_ref[...], sh_ref[0], sc_ref[0]).astype(BF16)
    acc = jnp.zeros(h.shape, F32)
    for f0 in range(0, FFN_HIDDEN, FFN_CHUNK):
        g = jnp.dot(nb, wg_ref[:, f0:f0 + FFN_CHUNK], preferred_element_type=F32)
        u = jnp.dot(nb, wu_ref[:, f0:f0 + FFN_CHUNK], preferred_element_type=F32)
        a = (g * jax.nn.sigmoid(g) * u).astype(BF16)
        acc = acc + jnp.dot(a, wd_ref[f0:f0 + FFN_CHUNK, :], preferred_element_type=F32)
    out = h + (0.5 * gt_ref[0]) * acc
    if final_nw_ref is not None:
        out = out * lax.rsqrt(jnp.mean(out * out, axis=-1, keepdims=True) + EPS) * final_nw_ref[...]
    return out


IN_SPLITS = (GROUP_W, GROUP_W, 2 * GROUP_W, 3 * GROUP_W, GROUP_W, AB_PAD)


def _ffn_inproj_kernel(h_ref, *refs):
    ffn_refs, (nw_ref, sh_ref, sc_ref, w_ref, wab_ref, h_out_ref), o_refs = refs[:7], refs[7:13], refs[13:]
    h1 = _ffn_apply(h_ref[0], *ffn_refs)
    h_out_ref[0] = h1
    nb = _rms_mod(h1, nw_ref[...], sh_ref[0], sc_ref[0]).astype(BF16)
    off = 0
    for o_ref in o_refs[:-1]:
        wdt = o_ref.shape[-1]
        o_ref[0] = jnp.dot(nb, w_ref[:, off:off + wdt], preferred_element_type=F32)
        off += wdt
    o_refs[-1][0] = jnp.dot(nb, wab_ref[...], preferred_element_type=F32)


def _ffn_inproj(h, ffn_args, weights, nw, shift, scale, w_main, w_ab, layer, gdn_only, tm):
    b, t, d = h.shape
    vec = pl.BlockSpec((1, 1, d), lambda i, j: (i, 0, 0))
    tok = pl.BlockSpec((1, tm, d), lambda i, j: (i, j, 0))
    splits = IN_SPLITS[3:] if gdn_only else IN_SPLITS
    w_spec = (_layer_resident(w_main, layer, GDN_OFF, 1) if gdn_only else _layer_resident(w_main, layer))
    outs = pl.pallas_call(
        _ffn_inproj_kernel,
        grid=(b, t // tm),
        in_specs=[tok, _resident((1, d)), vec, vec, vec] + [_layer_resident(w, layer) for w in weights]
        + [_resident((1, d)), vec, vec, w_spec, _layer_resident(w_ab, layer)],
        out_specs=[tok] + [pl.BlockSpec((1, tm, s), lambda i, j: (i, j, 0)) for s in splits],
        out_shape=[jax.ShapeDtypeStruct(h.shape, F32)] + [jax.ShapeDtypeStruct((b, t, s), F32) for s in splits],
        compiler_params=_params(2),
        name="ffn_inproj",
    )(h, *ffn_args, *weights, nw, shift, scale, w_main, w_ab)
    return outs[0], outs[1:]


def _split_bf16(x, parts):
    out = []
    for _ in range(parts - 1):
        p = x.astype(BF16)
        out.append(p)
        x = x - p.astype(F32)
    return out + [x.astype(BF16)]


def _dot01(x, w01, parts=3):
    return sum(jnp.dot(p, w01, preferred_element_type=F32) for p in _split_bf16(x, parts))


def _dot01_left(w01, x, parts=3):
    return sum(jnp.dot(w01, p, preferred_element_type=F32) for p in _split_bf16(x, parts))


def _blockdiag(w):
    g, a, b = w.shape
    return jnp.einsum('gab,gh->gahb', w, jnp.eye(g, dtype=w.dtype)).reshape(g * a, g * b)


def _halo_specs(tm, halo, t, width):
    r = tm // halo
    last = t // halo - 1
    prev = pl.BlockSpec((1, halo, width), lambda i, j: (i, jnp.maximum(j * r - 1, 0), 0))
    cur = pl.BlockSpec((1, tm, width), lambda i, j: (i, j, 0))
    nxt = pl.BlockSpec((1, halo, width), lambda i, j: (i, jnp.minimum((j + 1) * r, last), 0))
    return prev, cur, nxt


def _edge_masked(prev_ref, next_ref):
    j, nj = pl.program_id(1), pl.num_programs(1)
    top = jnp.where(j > 0, prev_ref[0], 0.0)
    bot = jnp.where(j < nj - 1, next_ref[0], 0.0)
    return top, bot


POOL_TM = 1024
POOL_HALO = 512


def _pool_window_sums_1d(u, row_len):
    n = u.shape[0]
    col = lax.broadcasted_iota(jnp.int32, u.shape, 0) & (row_len - 1)
    grp = lax.broadcasted_iota(jnp.int32, u.shape, 1) // POOL_GW

    def back(x, s):
        return jnp.where(col >= s, pltpu.roll(x, s, 0), 0.0)

    def fwd(x, s):
        return jnp.where(col < row_len - s, pltpu.roll(x, n - s, 0), 0.0)

    b = back(u, 1)
    f = u
    out = b + f
    for gi in range(1, len(POOL_WINDOWS)):
        s = POOL_WINDOWS[gi] // 4
        b = b + back(b, s)
        f = f + fwd(f, s)
        out = jnp.where(grp >= gi, b + f, out)
    return out


def _pool_counts(idx, extent, halfw):
    return jnp.minimum(idx + halfw, extent) - jnp.maximum(idx - halfw, 0)


def _pool_finish(mean, u, wbd_ref, scale_ref, o_ref):
    d = (mean - u).astype(BF16)
    o_ref[0] = jnp.dot(d, wbd_ref[...], preferred_element_type=F32) * scale_ref[...]


def _pool_grid_kernel(prev_ref, cur_ref, next_ref, wbd_ref, scale_ref, o_ref, *, n_rows):
    tm = cur_ref.shape[1]
    top, bot = _edge_masked(prev_ref, next_ref)
    u = cur_ref[0]
    grp = lax.broadcasted_iota(jnp.int32, (tm, GROUP_W), 1) // POOL_GW
    arr, off = jnp.concatenate([top, u, bot], axis=0), 0
    tot = None
    for gi, w in enumerate(POOL_WINDOWS):
        sh = GRID_W * max(w // 4, 1) if gi else GRID_W
        if gi == 0:
            arr = arr[:-sh] + arr[sh:]
            off = sh
        else:
            arr = arr[:-2 * sh] + arr[2 * sh:]
            off = off + sh
        centre = arr[POOL_HALO - off:POOL_HALO - off + tm]
        tot = centre if tot is None else jnp.where(grp >= gi, centre, tot)
    tot = _pool_window_sums_1d(tot, GRID_W)
    tok = pl.program_id(1) * tm + lax.broadcasted_iota(jnp.int32, (tm, GROUP_W), 0)
    halfw = jnp.left_shift(1, grp)
    cnt = (_pool_counts(tok // GRID_W, n_rows, halfw) * _pool_counts(tok & (GRID_W - 1), GRID_W, halfw))
    _pool_finish(tot / cnt.astype(F32), u, wbd_ref, scale_ref, o_ref)


def _pool_seq_kernel(u_ref, wbd_ref, scale_ref, o_ref):
    u = u_ref[0]
    n = u.shape[0]
    tot = _pool_window_sums_1d(u, n)
    grp = lax.broadcasted_iota(jnp.int32, u.shape, 1) // POOL_GW
    tok = lax.broadcasted_iota(jnp.int32, u.shape, 0)
    cnt = _pool_counts(tok, n, jnp.left_shift(1, grp))
    _pool_finish(tot / cnt.astype(F32), u, wbd_ref, scale_ref, o_ref)


def _pool_mix(u, wbd, scale, grid_rows):
    b, t, w = u.shape
    out_shape = jax.ShapeDtypeStruct(u.shape, F32)
    if grid_rows is None:
        tok = pl.BlockSpec((1, t, w), lambda i: (i, 0, 0))
        return pl.pallas_call(
            _pool_seq_kernel, grid=(b,),
            in_specs=[tok, _resident(wbd.shape), _resident(scale.shape)],
            out_specs=tok, out_shape=out_shape, compiler_params=_params(1), name="pool_seq",
        )(u, wbd, scale)
    prev, cur, nxt = _halo_specs(POOL_TM, POOL_HALO, t, w)
    return pl.pallas_call(
        functools.partial(_pool_grid_kernel, n_rows=grid_rows),
        grid=(b, t // POOL_TM),
        in_specs=[prev, cur, nxt, _resident(wbd.shape), _resident(scale.shape)],
        out_specs=cur, out_shape=out_shape, compiler_params=_params(2), name="pool_grid",
    )(u, u, u, wbd, scale)


CONV_TM = 512
CONV_HALO = 16
CONV_SUB = 64


def _glu(x):
    return x[:, :GROUP_W] * jax.nn.sigmoid(x[:, GROUP_W:])


def _conv_kernel(prev_ref, cur_ref, next_ref, dww_ref, dwb_ref, lng_ref, lnb_ref, pw_ref, o_ref,
                 ext_ref, sh_ref):
    tm = cur_ref.shape[1]
    top, bot = _edge_masked(prev_ref, next_ref)
    ext_ref[0:CONV_HALO] = _glu(top)
    ext_ref[CONV_HALO:CONV_HALO + tm] = _glu(cur_ref[0])
    ext_ref[CONV_HALO + tm:] = _glu(bot)
    n_sh = sh_ref.shape[1]
    for s in range(SUBLANES):
        sh_ref[s] = ext_ref[s:s + n_sh, :]
    base = CONV_HALO - CONV_K // 2
    for r0 in range(0, tm, CONV_SUB):
        acc = jnp.zeros((CONV_SUB, GROUP_W), F32)
        for k in range(CONV_K):
            a, s = divmod(base + k, SUBLANES)
            acc = acc + dww_ref[k:k + 1, :] * sh_ref[s, r0 + a * SUBLANES:r0 + a * SUBLANES + CONV_SUB, :]
        h = acc + dwb_ref[...]
        mu = jnp.mean(h, axis=-1, keepdims=True)
        var = jnp.mean(jnp.square(h - mu), axis=-1, keepdims=True)
        h = (h - mu) * lax.rsqrt(var + EPS) * lng_ref[...] + lnb_ref[...]
        h = (h * jax.nn.sigmoid(h)).astype(BF16)
        o_ref[0, r0:r0 + CONV_SUB, :] = jnp.dot(h, pw_ref[...], preferred_element_type=F32)


def _conv_mix(u2, dw_w, dw_b, ln_g, ln_b, pw_w, tm):
    b, t, w2 = u2.shape
    prev, cur, nxt = _halo_specs(tm, CONV_HALO, t, w2)
    row = _resident((1, GROUP_W))
    return pl.pallas_call(
        _conv_kernel,
        grid=(b, t // tm),
        in_specs=[prev, cur, nxt, _resident(dw_w.shape), row, row, row, _resident(pw_w.shape)],
        out_specs=pl.BlockSpec((1, tm, GROUP_W), lambda i, j: (i, j, 0)),
        out_shape=jax.ShapeDtypeStruct((b, t, GROUP_W), F32),
        scratch_shapes=[pltpu.VMEM((tm + 2 * CONV_HALO, GROUP_W), F32),
                        pltpu.VMEM((SUBLANES, tm + 2 * CONV_HALO - SUBLANES, GROUP_W), F32)],
        compiler_params=_params(2), name="conv_mix",
    )(u2, u2, u2, dw_w, dw_b, ln_g, ln_b, pw_w)


FOUR_N1 = 128
FOUR_TT = 8
FOUR_KT = 8


def _hilo(w):
    w = jnp.asarray(w, F32)
    hi = w.astype(BF16)
    return hi, (w - hi.astype(F32)).astype(BF16)


def _dot_hp(x, wh, wl):
    xh = x.astype(BF16)
    xl = (x - xh.astype(F32)).astype(BF16)
    return (jnp.dot(xh, wh, preferred_element_type=F32) + jnp.dot(xh, wl, preferred_element_type=F32)
            + jnp.dot(xl, wh, preferred_element_type=F32))


def _dot_hp_left(wh, wl, x):
    xh = x.astype(BF16)
    xl = (x - xh.astype(F32)).astype(BF16)
    return (jnp.dot(wh, xh, preferred_element_type=F32) + jnp.dot(wl, xh, preferred_element_type=F32)
            + jnp.dot(wh, xl, preferred_element_type=F32))


def _cos_sin(n, rows=None, cols=None):
    r = np.arange(n if rows is None else rows, dtype=np.int64)
    c = np.arange(n if cols is None else cols, dtype=np.int64)
    ang = 2.0 * np.pi * ((np.outer(r, c) % n).astype(np.float64) / n)
    return np.cos(ang), np.sin(ang)


def _channel_dft():
    c, s = _cos_sin(FOURIER_GW)
    eye = np.eye(FOURIER_GROUPS)
    return np.concatenate([np.kron(eye, c), -np.kron(eye, s)], axis=1)


def _fourier_a_kernel(u_ref, cdh_ref, cdl_ref, f1h_ref, f1l_ref, twc_ref, tws_ref, o_ref):
    for j in range(FOUR_TT):
        x = u_ref[0, :, j * GROUP_W:(j + 1) * GROUP_W]
        z = _dot_hp(x, cdh_ref[...], cdl_ref[...])
        zz = jnp.concatenate([z[:, :GROUP_W], z[:, GROUP_W:]], axis=0)
        a = _dot_hp_left(f1h_ref[...], f1l_ref[...], zz)
        ar, ai = a[:FOUR_N1], a[FOUR_N1:]
        c = jnp.concatenate([twc_ref[j]] * (GROUP_W // LANES), axis=1)
        s = jnp.concatenate([tws_ref[j]] * (GROUP_W // LANES), axis=1)
        o_ref[0, 0, j] = ar * c + ai * s
        o_ref[0, 1, j] = ai * c - ar * s


def _fourier_b_kernel(g_ref, f2h_ref, f2l_ref, fw_ref, o_ref, *, scale):
    for kk in range(FOUR_KT):
        sl = slice(kk * GROUP_W, (kk + 1) * GROUP_W)
        gm = jnp.concatenate([g_ref[0, 0, :, sl], g_ref[0, 1, :, sl]], axis=0)
        f = _dot_hp_left(f2h_ref[...], f2l_ref[...], gm) * scale
        o_ref[0, :, sl] = jnp.dot(f.astype(BF16), fw_ref[...], preferred_element_type=F32)


def _fourier_small_kernel(u_ref, cdh_ref, cdl_ref, fh_ref, fl_ref, fw_ref, o_ref, *, scale):
    z = _dot_hp(u_ref[0], cdh_ref[...], cdl_ref[...])
    zz = jnp.concatenate([z[:, :GROUP_W], z[:, GROUP_W:]], axis=0)
    f = _dot_hp_left(fh_ref[...], fl_ref[...], zz) * scale
    o_ref[0] = jnp.dot(f.astype(BF16), fw_ref[...], preferred_element_type=F32)


def _fourier_mix(u, fw):
    b, n, w = u.shape
    scale = 1.0 / math.sqrt(n * FOURIER_GW)
    cdh, cdl = _hilo(_channel_dft())
    if n != FOUR_N1 * FOUR_N1:
        c, s = _cos_sin(n)
        fh, fl = _hilo(np.concatenate([c, s], axis=1))
        tok = pl.BlockSpec((1, n, w), lambda i: (i, 0, 0))
        return pl.pallas_call(
            functools.partial(_fourier_small_kernel, scale=scale), grid=(b,),
            in_specs=[tok] + [_resident(a.shape) for a in (cdh, cdl, fh, fl, fw)],
            out_specs=tok, out_shape=jax.ShapeDtypeStruct(u.shape, F32),
            compiler_params=_params(1), name="fourier_small",
        )(u, cdh, cdl, fh, fl, fw)
    n1 = FOUR_N1
    c1, s1 = _cos_sin(n1)
    f1h, f1l = _hilo(np.block([[c1, s1], [-s1, c1]]))
    f2h, f2l = _hilo(np.concatenate([c1, s1], axis=1))
    twc, tws = _cos_sin(n, rows=n1, cols=n1)
    twc = jnp.broadcast_to(jnp.asarray(twc, F32)[:, :, None], (n1, n1, LANES))
    tws = jnp.broadcast_to(jnp.asarray(tws, F32)[:, :, None], (n1, n1, LANES))
    tw_spec = pl.BlockSpec((FOUR_TT, n1, LANES), lambda i, j: (j, 0, 0))
    g = pl.pallas_call(
        _fourier_a_kernel, grid=(b, n1 // FOUR_TT),
        in_specs=[pl.BlockSpec((1, n1, FOUR_TT * w), lambda i, j: (i, 0, j))]
        + [_resident(a.shape) for a in (cdh, cdl, f1h, f1l)] + [tw_spec, tw_spec],
        out_specs=pl.BlockSpec((1, 2, FOUR_TT, n1, w), lambda i, j: (i, 0, j, 0, 0)),
        out_shape=jax.ShapeDtypeStruct((b, 2, n1, n1, w), F32),
        compiler_params=_params(2), name="fourier_a",
    )(u.reshape(b, n1, n1 * w), cdh, cdl, f1h, f1l, twc, tws)
    y = pl.pallas_call(
        functools.partial(_fourier_b_kernel, scale=scale), grid=(b, n1 // FOUR_KT),
        in_specs=[pl.BlockSpec((1, 2, n1, FOUR_KT * w), lambda i, j: (i, 0, 0, j))]
        + [_resident(a.shape) for a in (f2h, f2l, fw)],
        out_specs=pl.BlockSpec((1, n1, FOUR_KT * w), lambda i, j: (i, 0, j)),
        out_shape=jax.ShapeDtypeStruct((b, n1, n1 * w), F32),
        compiler_params=_params(2), name="fourier_b",
    )(g.reshape(b, 2, n1, n1 * w), f2h, f2l, fw)
    return y.reshape(b, n, w)


GDN_C = GDN_CHUNK
GDN_PREP_TM = 512
GDN_SCAN_TM = 512
GDN_SUB = 64
GDN_GROUP = 512
N_GATE = 4 * GDN_HEADS


def _softplus(x):
    return jnp.maximum(x, 0.0) + jnp.log1p(jnp.exp(-jnp.abs(x)))


def _gdn_consts():
    i = np.arange(GDN_C)
    lower = np.concatenate([i[None, :] <= i[:, None], i[None, :] >= i[:, None]], axis=0)
    same = np.ones((GDN_C, GDN_C), bool)
    expand = np.zeros((AB_PAD, 4 * GROUP_W), np.float32)
    for s in range(4):
        for h in range(GDN_HEADS):
            c0 = s * GROUP_W + h * GDN_HEAD_DIM
            expand[s * GDN_HEADS + h, c0:c0 + GDN_HEAD_DIM] = 1.0
    lane = np.arange(GROUP_W)
    headones = (lane[:, None] // GDN_HEAD_DIM) == (lane[None, :] // GDN_HEAD_DIM)
    as_bf = lambda a: jnp.asarray(a, F32).astype(BF16)
    return as_bf(lower), as_bf(same), as_bf(expand), as_bf(headones)


def _bd(x, bdmask):
    xb = x.astype(BF16)
    return jnp.where(bdmask, jnp.concatenate([xb] * GDN_HEADS, axis=0), jnp.zeros((), BF16))


def _bd_mask():
    return (lax.broadcasted_iota(jnp.int32, (GROUP_W, GROUP_W), 0) // GDN_HEAD_DIM
            == lax.broadcasted_iota(jnp.int32, (GROUP_W, GROUP_W), 1) // GDN_HEAD_DIM)


def _dot(a, b):
    return jnp.dot(a.astype(BF16), b.astype(BF16), preferred_element_type=F32)


def _diag_blocks(full):
    head = (lax.broadcasted_iota(jnp.int32, (GDN_HEAD_DIM, full.shape[1]), 1) // GDN_HEAD_DIM) % GDN_HEADS
    out = None
    for h in range(GDN_HEADS):
        blk = jnp.where(head == h, full[h * GDN_HEAD_DIM:(h + 1) * GDN_HEAD_DIM], 0.0)
        out = blk if out is None else out + blk
    return out


def _tri_inverse(ms, eye, level_masks, bdmask):
    ds = [eye - jnp.where(level_masks[0], m, 0.0) for m in ms]
    for mask in level_masks[1:]:
        es = [_dot(jnp.where(mask, m, 0.0), _bd(d, bdmask)) for m, d in zip(ms, ds)]
        fs = [_dot(d, _bd(e, bdmask)) for d, e in zip(ds, es)]
        ds = [d - f for d, f in zip(ds, fs)]
    return ds


def _gdn_prep_kernel(prev_ref, cur_ref, next_ref, *refs):
    tm = cur_ref.shape[1]
    halo = prev_ref.shape[1]
    ext_ref = refs[-4]
    top, bot = _edge_masked(prev_ref, next_ref)
    ext_ref[0:halo] = top
    ext_ref[halo:halo + tm] = cur_ref[0]
    ext_ref[halo + tm:] = bot
    grp = min(tm, GDN_GROUP)
    for g0 in range(0, tm, grp):
        _gdn_prep_group(g0, grp, halo, *refs)


def _gdn_prep_group(g0, grp, halo, ab_ref, cw_ref, alog_ref, dtb_ref,
                    tri_ref, same_ref, exp_ref, ones_ref,
                    pf_ref, qpf_ref, bmf_ref, o0f_ref, df_ref, pb_ref, qpb_ref, bmb_ref, o0b_ref, db_ref,
                    ext_ref, q_scr, k_scr, v_scr):
    c = GDN_C
    ls = [slice(ci * c, (ci + 1) * c) for ci in range(grp // c)]
    rs = [slice(g0 + ci * c, g0 + (ci + 1) * c) for ci in range(grp // c)]
    rows_g = slice(g0, g0 + grp)
    base = halo - GDN_CONV // 2
    for r0 in range(g0, g0 + grp, GDN_SUB):
        acc = jnp.zeros((GDN_SUB, 3 * GROUP_W), F32)
        for t in range(GDN_CONV):
            acc = acc + cw_ref[t:t + 1, :] * ext_ref[r0 + base + t:r0 + base + t + GDN_SUB, :]
        x = acc * jax.nn.sigmoid(acc)
        rows = slice(r0, r0 + GDN_SUB)
        q_scr[rows, :] = x[:, :GROUP_W]
        k_scr[rows, :] = x[:, GROUP_W:2 * GROUP_W]
        v_scr[rows, :] = x[:, 2 * GROUP_W:]
    qa, ka = q_scr[rows_g, :], k_scr[rows_g, :]
    q_scr[rows_g, :] = qa * lax.rsqrt(_dot01(qa * qa, ones_ref[...], 2) + EPS) * (GDN_HEAD_DIM ** -0.5)
    k_scr[rows_g, :] = ka * lax.rsqrt(_dot01(ka * ka, ones_ref[...], 2) + EPS)

    ab = ab_ref[0, rows_g, :]
    lane = lax.broadcasted_iota(jnp.int32, ab.shape, 1)
    g = -jnp.exp(alog_ref[...]) * _softplus(ab + dtb_ref[...])
    gate = jnp.where(lane < 2 * GDN_HEADS, g, jax.nn.sigmoid(ab))
    cums = [_dot01_left(tri_ref[...], gate[r], 2) for r in ls]
    lane_c = lax.broadcasted_iota(jnp.int32, (c, AB_PAD), 1)
    cum = jnp.concatenate([jnp.where(lane_c < GDN_HEADS, cm[:c], cm[c:]) for cm in cums], axis=0)
    wide = _dot01(jnp.where(lane < 2 * GDN_HEADS, cum, gate), exp_ref[...], 2)
    gcs = (wide[:, :GROUP_W], wide[:, GROUP_W:2 * GROUP_W])
    betas = (wide[:, 2 * GROUP_W:3 * GROUP_W], wide[:, 3 * GROUP_W:])

    ii = lax.broadcasted_iota(jnp.int32, (c, GROUP_W), 0)
    jj = lax.broadcasted_iota(jnp.int32, (c, GROUP_W), 1) & (c - 1)
    diag2 = jnp.concatenate([ii == jj] * 2, axis=1)
    rows_gc = [_dot01_left(same_ref[...], jnp.where(diag2, wide[r, :2 * GROUP_W], 0.0), 2) for r in ls]
    gams = ([jnp.where(ii >= jj, jnp.exp(jnp.minimum(gcs[0][r] - rg[:, :GROUP_W], 0.0)), 0.0)
             for r, rg in zip(ls, rows_gc)],
            [jnp.where(ii <= jj, jnp.exp(jnp.minimum(gcs[1][r] - rg[:, GROUP_W:], 0.0)), 0.0)
             for r, rg in zip(ls, rows_gc)])
    eye = (ii == jj).astype(F32)
    stricts = (ii > jj, ii < jj)
    levels = [((ii // (2 * s)) == (jj // (2 * s))) & ((ii // s) != (jj // s)) for s in (1, 2, 4, 8, 16, 32)]
    bdmask = _bd_mask()
    outs = ((pf_ref, qpf_ref, bmf_ref, o0f_ref, df_ref), (pb_ref, qpb_ref, bmb_ref, o0b_ref, db_ref))

    qs = [q_scr[r, :] for r in rs]
    ks = [k_scr[r, :] for r in rs]
    vs = [v_scr[r, :] for r in rs]
    kqs = [lax.dot_general(jnp.concatenate([k, q], axis=0).astype(BF16), _bd(k, bdmask),
                           (((1,), (1,)), ((), ())), preferred_element_type=F32) for k, q in zip(ks, qs)]
    for di in range(2):
        gc_l = [gcs[di][r] for r in ls]
        gam_l = gams[di]
        beta_l = [betas[di][r] for r in ls]
        ms = [jnp.where(stricts[di], beta * kq[:c] * gam, 0.0) for beta, kq, gam in zip(beta_l, kqs, gam_l)]
        ts = _tri_inverse(ms, eye, levels, bdmask)
        egcs = [jnp.exp(gc) for gc in gc_l]
        uws = [_dot(t, jnp.concatenate([_bd(v * beta, bdmask), _bd(k * beta * egc, bdmask)], axis=1))
               for t, v, k, beta, egc in zip(ts, vs, ks, beta_l, egcs)]
        aqks = [kq[c:] * gam for kq, gam in zip(kqs, gam_l)]
        ows = [_dot(aqk, jnp.concatenate([_bd(uw[:, :GROUP_W], bdmask), _bd(uw[:, GROUP_W:], bdmask)], axis=1))
               for aqk, uw in zip(aqks, uws)]
        p_ref, qp_ref, bm_ref, o0_ref, d_ref = outs[di]
        for ci, r in enumerate(rs):
            gc, uw, ow = gc_l[ci], uws[ci], ows[ci]
            glast = gc[0:1] if di else gc[c - 1:c]
            kdec = ks[ci] * jnp.exp(glast - gc)
            bp = _diag_blocks(lax.dot_general(kdec.astype(BF16), uw.astype(BF16), (((0,), (0,)), ((), ())),
                                              preferred_element_type=F32))
            bm_ref[0, r, :] = bp[:, :GROUP_W]
            p_ref[0, r, :] = bp[:, GROUP_W:].astype(BF16)
            o0_ref[0, r, :] = ow[:, :GROUP_W]
            qp_ref[0, r, :] = (qs[ci] * egcs[ci] - ow[:, GROUP_W:]).astype(BF16)
            d_ref[0, g0 // c + ci:g0 // c + ci + 1, :] = jnp.exp(glast)


def _gdn_prep(qkv, ab, conv_w, a_log, dt_bias, tm):
    b, t, w3 = qkv.shape
    halo = 8
    nck = tm // GDN_C
    prev, cur, nxt = _halo_specs(tm, halo, t, w3)
    pad = lambda a: jnp.pad(a.reshape(1, -1), ((0, 0), (0, AB_PAD - a.size)))
    consts = _gdn_consts()
    tok = pl.BlockSpec((1, tm, GROUP_W), lambda i, j: (i, j, 0))
    dec = pl.BlockSpec((1, nck, GROUP_W), lambda i, j: (i, j, 0))
    tok_shape = lambda dt: jax.ShapeDtypeStruct((b, t, GROUP_W), dt)
    dir_specs = [tok, tok, tok, tok, dec]
    dir_shapes = [tok_shape(BF16), tok_shape(BF16), tok_shape(F32), tok_shape(F32),
                  jax.ShapeDtypeStruct((b, t // GDN_C, GROUP_W), F32)]
    return pl.pallas_call(
        _gdn_prep_kernel,
        grid=(b, t // tm),
        in_specs=[prev, cur, nxt, pl.BlockSpec((1, tm, AB_PAD), lambda i, j: (i, j, 0)),
                  _resident(conv_w.shape), _resident((1, AB_PAD)), _resident((1, AB_PAD))]
        + [_resident(cst.shape) for cst in consts],
        out_specs=dir_specs * 2,
        out_shape=dir_shapes * 2,
        scratch_shapes=[pltpu.VMEM((tm + 2 * halo, w3), F32)] + [pltpu.VMEM((tm, GROUP_W), F32)] * 3,
        compiler_params=_params(2), name="gdn_prep",
    )(qkv, qkv, qkv, ab, conv_w, pad(a_log), pad(dt_bias), *consts)


def _gdn_scan_kernel(pf_ref, qpf_ref, bmf_ref, o0f_ref, df_ref, pb_ref, qpb_ref, bmb_ref, o0b_ref, db_ref,
                     s0f_ref, s0b_ref, of_ref, ob_ref, sf_ref, sb_ref):
    nb, tm = pf_ref.shape[0], pf_ref.shape[1]
    nc = tm // GDN_C

    @pl.when(pl.program_id(0) == 0)
    def _():
        sf_ref[...] = s0f_ref[...]
        sb_ref[...] = s0b_ref[...]

    bdmask = _bd_mask()
    fwd = (pf_ref, qpf_ref, bmf_ref, o0f_ref, df_ref, of_ref)
    bwd = (pb_ref, qpb_ref, bmb_ref, o0b_ref, db_ref, ob_ref)
    chains = [(fwd, bi, False) for bi in range(nb)] + [(bwd, bi, True) for bi in range(nb)]
    states = [sf_ref[bi] for bi in range(nb)] + [sb_ref[bi] for bi in range(nb)]
    for step in range(nc):
        lhs, rows = [], []
        for (p_ref, qp_ref, _, _, _, _), bi, rev in chains:
            ci = nc - 1 - step if rev else step
            r = slice(ci * GDN_C, (ci + 1) * GDN_C)
            rows.append((ci, r))
            lhs.append(jnp.concatenate([p_ref[bi, r, :], qp_ref[bi, r, :]], axis=0))
        res = [jnp.dot(a, _bd(s, bdmask), preferred_element_type=F32) for a, s in zip(lhs, states)]
        new_states = []
        for (_, _, bm_ref, o0_ref, d_ref, o_ref), bi, _ in chains:
            n = len(new_states)
            ci, r = rows[n]
            o_ref[bi, r, :] = o0_ref[bi, r, :] + res[n][GDN_C:]
            new_states.append(d_ref[bi, ci:ci + 1, :] * states[n] - res[n][:GDN_C] + bm_ref[bi, r, :])
        states = new_states
    for bi in range(nb):
        sf_ref[bi] = states[bi]
        sb_ref[bi] = states[nb + bi]


def _gdn_scan(prep, s0f, s0b, tm):
    b, t, w = prep[0].shape
    nj = t // tm
    nck = tm // GDN_C
    fwd = pl.BlockSpec((b, tm, w), lambda j: (0, j, 0))
    bwd = pl.BlockSpec((b, tm, w), lambda j: (0, nj - 1 - j, 0))
    dfwd = pl.BlockSpec((b, nck, w), lambda j: (0, j, 0))
    dbwd = pl.BlockSpec((b, nck, w), lambda j: (0, nj - 1 - j, 0))
    st = pl.BlockSpec((b, GDN_HEAD_DIM, w), lambda j: (0, 0, 0))
    st_shape = jax.ShapeDtypeStruct((b, GDN_HEAD_DIM, w), F32)
    return pl.pallas_call(
        _gdn_scan_kernel,
        grid=(nj,),
        in_specs=[fwd] * 4 + [dfwd] + [bwd] * 4 + [dbwd] + [st, st],
        out_specs=[fwd, bwd, st, st],
        out_shape=[jax.ShapeDtypeStruct((b, t, w), F32)] * 2 + [st_shape] * 2,
        compiler_params=pltpu.CompilerParams(dimension_semantics=("arbitrary",),
                                             vmem_limit_bytes=VMEM_LIMIT),
        name="gdn_scan",
    )(*prep, s0f, s0b)


def _mix_ffn_kernel(h_ref, yp_ref, yf_ref, yc_ref, of_ref, ob_ref, z_ref, gt_ref, nw_ref, ones_ref,
                    w_ref, *rest):
    o = of_ref[0] + ob_ref[0]
    ms = _dot01(o * o, ones_ref[...], 2) * (1.0 / GDN_HEAD_DIM)
    z = z_ref[0]
    yg = o * lax.rsqrt(ms + EPS) * nw_ref[...] * (z * jax.nn.sigmoid(z))
    y = jnp.zeros(h_ref.shape[1:], F32)
    for gi, part in enumerate((yp_ref[0], yf_ref[0], yc_ref[0], yg)):
        y = y + jnp.dot(part.astype(BF16), w_ref[gi * GROUP_W:(gi + 1) * GROUP_W, :],
                        preferred_element_type=F32)
    rest[-1][0] = _ffn_apply(h_ref[0] + gt_ref[0] * y, *rest[:-1])


def _mix_ffn(h, ys, o_f, o_b, z, gate, gdn_nw, w_out, ffn_args, weights, layer, tm, final_nw=None):
    b, t, d = h.shape
    tok = pl.BlockSpec((1, tm, d), lambda i, j: (i, j, 0))
    grp = pl.BlockSpec((1, tm, GROUP_W), lambda i, j: (i, j, 0))
    vec = pl.BlockSpec((1, 1, d), lambda i, j: (i, 0, 0))
    ones = _gdn_consts()[3]
    extra = [] if final_nw is None else [final_nw]
    return pl.pallas_call(
        _mix_ffn_kernel,
        grid=(b, t // tm),
        in_specs=[tok] + [grp] * 6 + [vec, _resident((1, GROUP_W)), _resident(ones.shape),
                                     _layer_resident(w_out, layer), _resident((1, d)), vec, vec, vec]
        + [_layer_resident(w, layer) for w in weights] + [_resident((1, d))] * len(extra),
        out_specs=tok,
        out_shape=jax.ShapeDtypeStruct(h.shape, F32),
        compiler_params=_params(2), name="mix_ffn",
    )(h, *ys, o_f, o_b, z, gate, gdn_nw, ones, w_out, *ffn_args, *weights, *extra)


MOD_ROWS = 8
MOD_TN = 1152


def _mod_kernel(c_ref, w_ref, b_ref, o_ref):
    cv = c_ref[...]
    a = (cv * jax.nn.sigmoid(cv)).astype(BF16)
    o_ref[0] = jnp.dot(a, w_ref[0].astype(BF16), preferred_element_type=F32) + b_ref[0]


def _modulation(c, c_ctx, mod_w, mod_b):
    nl, d, n = mod_w.shape
    cond = jnp.concatenate([c, c_ctx[None, :]], axis=0)
    cond = jnp.pad(cond, ((0, MOD_ROWS - cond.shape[0]), (0, 0)))
    return pl.pallas_call(
        _mod_kernel,
        grid=(nl, n // MOD_TN),
        in_specs=[_resident(cond.shape), pl.BlockSpec((1, d, MOD_TN), lambda i, j: (i, 0, j)),
                  pl.BlockSpec((1, 1, MOD_TN), lambda i, j: (i, 0, j))],
        out_specs=pl.BlockSpec((1, MOD_ROWS, MOD_TN), lambda i, j: (i, 0, j)),
        out_shape=jax.ShapeDtypeStruct((nl, MOD_ROWS, n), F32),
        compiler_params=_params(2), name="modulation",
    )(cond, mod_w, mod_b[:, None, :])


def _token_mix(p, gdn_state, grid_rows, tm, wl, need_out):
    p_pool, p_four, p_conv, p_qkv, p_z, p_ab = p
    prep = _gdn_prep(p_qkv, p_ab, wl["gdn_conv_w"], wl["gdn_a_log"], wl["gdn_dt_bias"], tm)
    o_f, o_b, s_f, s_b = _gdn_scan(prep, *gdn_state, tm)
    if not need_out:
        return None, (s_f, s_b)
    ys = (_pool_mix(p_pool, wl["pool_wbd"], wl["pool_scale"], grid_rows),
          _fourier_mix(p_four, wl["fourier_w"]),
          _conv_mix(p_conv, wl["conv_dw_w"], wl["conv_dw_b"], wl["conv_ln_g"], wl["conv_ln_b"],
                    wl["conv_pw_w"], tm))
    return (ys, o_f, o_b, p_z), (s_f, s_b)


def kernel(x, c, ctx, c_ctx, mod_w, mod_b, norm_w, ffn1_wg, ffn1_wu, ffn1_wd, ffn2_wg, ffn2_wu,
           ffn2_wd, w_in, w_out, pool_w, pool_scale, fourier_w, conv_dw_w, conv_dw_b, conv_ln_g,
           conv_ln_b, conv_pw_w, gdn_conv_w, gdn_a_log, gdn_dt_bias, gdn_norm_w, final_norm_w):
    bsz, seq, d = x.shape
    n_ctx = ctx.shape[1]
    rows = seq // GRID_W
    tm_x, tm_c = 512, n_ctx
    hx, hc = x, ctx
    mods = _modulation(c, c_ctx, mod_w, mod_b)
    zero_state = (jnp.zeros((bsz, GDN_HEAD_DIM, GROUP_W), F32),) * 2
    f1 = tuple(_cast_bf16(w) for w in (ffn1_wg, ffn1_wu, ffn1_wd))
    f2 = tuple(_cast_bf16(w) for w in (ffn2_wg, ffn2_wu, ffn2_wd))
    w_main = _cast_bf16(w_in, cols=2 * GDN_OFF)
    w_ab = jnp.pad(w_in[:, :, 2 * GDN_OFF:], ((0, 0), (0, 0), (0, AB_PAD - N_GATE))).astype(BF16)
    w_out_b = _cast_bf16(w_out)
    for l in range(DEPTH):
        last = l == DEPTH - 1
        mx = [m[:, None, :] for m in jnp.split(mods[l, :bsz], N_MOD, axis=-1)]
        mc = [jnp.broadcast_to(m[None], (bsz, 1, d)) for m in jnp.split(mods[l, bsz:bsz + 1], N_MOD, axis=-1)]
        nw = norm_w[l][:, None, :]
        row = lambda a: a.reshape(1, -1)
        wl = dict(pool_wbd=_blockdiag(pool_w[l]).astype(BF16), pool_scale=row(pool_scale[l]),
                  fourier_w=fourier_w[l].astype(BF16), conv_dw_w=conv_dw_w[l], conv_dw_b=row(conv_dw_b[l]),
                  conv_ln_g=row(conv_ln_g[l]), conv_ln_b=row(conv_ln_b[l]),
                  conv_pw_w=conv_pw_w[l].astype(BF16), gdn_conv_w=gdn_conv_w[l],
                  gdn_a_log=gdn_a_log[l], gdn_dt_bias=gdn_dt_bias[l])
        gdn_nw = row(jnp.tile(gdn_norm_w[l], GDN_HEADS))

        hx, px = _ffn_inproj(hx, (nw[0], mx[0], mx[1], mx[2]), f1, nw[1], mx[3], mx[4], w_main, w_ab, l,
                             False, tm_x)
        hc, pc = _ffn_inproj(hc, (nw[0], mc[0], mc[1], mc[2]), f1, nw[1], mc[3], mc[4], w_main, w_ab, l,
                             last, tm_c)
        if last:
            pc = (None,) * 3 + tuple(pc)

        mix_c, ctx_state = _token_mix(pc, zero_state, None, tm_c, wl, not last)
        mix_x, _ = _token_mix(px, ctx_state, rows, tm_x, wl, True)
        hx = _mix_ffn(hx, *mix_x, mx[5], gdn_nw, w_out_b, (nw[2], mx[6], mx[7], mx[8]), f2, l, tm_x,
                      final_norm_w[None, :] if last else None)
        if not last:
            hc = _mix_ffn(hc, *mix_c, mc[5], gdn_nw, w_out_b, (nw[2], mc[6], mc[7], mc[8]), f2, l, tm_c)
    return hx
```

```python
import functools
import math

import jax
import jax.numpy as jnp
import numpy as np
from jax import lax
from jax.experimental import pallas as pl
from jax.experimental.pallas import tpu as pltpu

D_MODEL = 1024
DEPTH = 4
GRID_W = 64
N_MIXERS = 4
GROUP_W = D_MODEL // N_MIXERS
POOL_WINDOWS = (2, 4, 8, 16)
POOL_GROUPS = 4
POOL_GW = GROUP_W // POOL_GROUPS
FOURIER_GROUPS = 4
FOURIER_GW = GROUP_W // FOURIER_GROUPS
CONV_K = 31
GDN_HEAD_DIM = 64
GDN_HEADS = GROUP_W // GDN_HEAD_DIM
GDN_CONV = 3
GDN_CHUNK = 64
FFN_HIDDEN = 128 * ((8 * D_MODEL // 3 + 127) // 128)
N_MOD = 9
EPS = 1e-6
POOL_OFF = 0
FOURIER_OFF = POOL_OFF + GROUP_W
CONV_OFF = FOURIER_OFF + GROUP_W
GDN_OFF = CONV_OFF + 2 * GROUP_W

LANES = 128
SUBLANES = 8
VMEM_LIMIT = 56 * 1024 * 1024
FFN_CHUNK = 256
AB_PAD = LANES

BF16 = jnp.bfloat16
F32 = jnp.float32


def _params(n_axes):
    return pltpu.CompilerParams(dimension_semantics=("parallel",) * n_axes,
                                vmem_limit_bytes=VMEM_LIMIT)


def _resident(shape):
    nd = len(shape)
    return pl.BlockSpec(shape, lambda *_: (0,) * nd, pipeline_mode=pl.Buffered(1))


def _layer_resident(stack, layer, cols=None, col_block=0):
    _, r, c = stack.shape
    return pl.BlockSpec((None, r, cols or c), lambda *_: (layer, 0, col_block), pipeline_mode=pl.Buffered(1))


def _cast_kernel(x_ref, o_ref):
    o_ref[...] = x_ref[...].astype(BF16)


def _cast_bf16(w, cols=None, rows_per_step=256):
    nl, r, c = w.shape
    cols = cols or c
    spec = pl.BlockSpec((1, rows_per_step, cols), lambda i, j: (i, j, 0))
    return pl.pallas_call(
        _cast_kernel, grid=(nl, r // rows_per_step), in_specs=[spec], out_specs=spec,
        out_shape=jax.ShapeDtypeStruct((nl, r, cols), BF16), compiler_params=_params(2), name="cast_bf16",
    )(w)


def _rms_mod(h, nw, shift, scale):
    ms = jnp.mean(h * h, axis=-1, keepdims=True)
    n = h * lax.rsqrt(ms + EPS) * nw
    return n * (1.0 + scale) + shift


def _ffn_apply(h, nw_ref, sh_ref, sc_ref, gt_ref, wg_ref, wu_ref, wd_ref, final_nw_ref=None):
    nb = _rms_mod(h, nw_ref[...], sh_ref[0], sc_ref[0]).astype(BF16)
    acc = jnp.zeros(h.shape, F32)
    for f0 in range(0, FFN_HIDDEN, FFN_CHUNK):
        g = jnp.dot(nb, wg_ref[:, f0:f0 + FFN_CHUNK], preferred_element_type=F32)
        u = jnp.dot(nb, wu_ref[:, f0:f0 + FFN_CHUNK], preferred_element_type=F32)
        a = (g * jax.nn.sigmoid(g) * u).astype(BF16)
        acc = acc + jnp.dot(a, wd_ref[f0:f0 + FFN_CHUNK, :], preferred_element_type=F32)
    out = h + (0.5 * gt_ref[0]) * acc
    if final_nw_ref is not None:
        out = out * lax.rsqrt(jnp.mean(out * out, axis=-1, keepdims=True) + EPS) * final_nw_ref[...]
    return out


def _ffn_kernel(h_ref, *rest):
    rest[-1][0] = _ffn_apply(h_ref[0], *rest[:-1])


def _ffn(h, nw, shift, scale, gate, weights, layer, tm, final_nw=None):
    b, t, d = h.shape
    vec = pl.BlockSpec((1, 1, d), lambda i, j: (i, 0, 0))
    tok = pl.BlockSpec((1, tm, d), lambda i, j: (i, j, 0))
    extra = [] if final_nw is None else [final_nw]
    return pl.pallas_call(
        _ffn_kernel,
        grid=(b, t // tm),
        in_specs=[tok, _resident((1, d)), vec, vec, vec] + [_layer_resident(w, layer) for w in weights]
        + [_resident((1, d))] * len(extra),
        out_specs=tok,
        out_shape=jax.ShapeDtypeStruct(h.shape, F32),
        compiler_params=_params(2),
        name="ffn",
    )(h, nw, shift, scale, gate, *weights, *extra)


IN_SPLITS = (GROUP_W, GROUP_W, 2 * GROUP_W, 3 * GROUP_W, GROUP_W, AB_PAD)


def _inproj_kernel(h_ref, nw_ref, sh_ref, sc_ref, w_ref, wab_ref, *o_refs):
    nb = _rms_mod(h_ref[0], nw_ref[...], sh_ref[0], sc_ref[0]).astype(BF16)
    off = 0
    for o_ref in o_refs[:-1]:
        wdt = o_ref.shape[-1]
        o_ref[0] = jnp.dot(nb, w_ref[:, off:off + wdt], preferred_element_type=F32)
        off += wdt
    o_refs[-1][0] = jnp.dot(nb, wab_ref[...], preferred_element_type=F32)


def _inproj(h, nw, shift, scale, w_main, w_ab, layer, gdn_only, tm):
    b, t, d = h.shape
    vec = pl.BlockSpec((1, 1, d), lambda i, j: (i, 0, 0))
    tok = pl.BlockSpec((1, tm, d), lambda i, j: (i, j, 0))
    splits = IN_SPLITS[3:] if gdn_only else IN_SPLITS
    w_spec = (_layer_resident(w_main, layer, GDN_OFF, 1) if gdn_only else _layer_resident(w_main, layer))
    return pl.pallas_call(
        _inproj_kernel,
        grid=(b, t // tm),
        in_specs=[tok, _resident((1, d)), vec, vec, w_spec, _layer_resident(w_ab, layer)],
        out_specs=[pl.BlockSpec((1, tm, s), lambda i, j: (i, j, 0)) for s in splits],
        out_shape=[jax.ShapeDtypeStruct((b, t, s), F32) for s in splits],
        compiler_params=_params(2),
        name="inproj",
    )(h, nw, shift, scale, w_main, w_ab)


def _split_bf16(x, parts):
    out = []
    for _ in range(parts - 1):
        p = x.astype(BF16)
        out.append(p)
        x = x - p.astype(F32)
    return out + [x.astype(BF16)]


def _dot01(x, w01, parts=3):
    return sum(jnp.dot(p, w01, preferred_element_type=F32) for p in _split_bf16(x, parts))


def _dot01_left(w01, x, parts=3):
    return sum(jnp.dot(w01, p, preferred_element_type=F32) for p in _split_bf16(x, parts))


def _blockdiag(w):
    g, a, b = w.shape
    return jnp.einsum('gab,gh->gahb', w, jnp.eye(g, dtype=w.dtype)).reshape(g * a, g * b)


def _halo_specs(tm, halo, t, width):
    r = tm // halo
    last = t // halo - 1
    prev = pl.BlockSpec((1, halo, width), lambda i, j: (i, jnp.maximum(j * r - 1, 0), 0))
    cur = pl.BlockSpec((1, tm, width), lambda i, j: (i, j, 0))
    nxt = pl.BlockSpec((1, halo, width), lambda i, j: (i, jnp.minimum((j + 1) * r, last), 0))
    return prev, cur, nxt


def _edge_masked(prev_ref, next_ref):
    j, nj = pl.program_id(1), pl.num_programs(1)
    top = jnp.where(j > 0, prev_ref[0], 0.0)
    bot = jnp.where(j < nj - 1, next_ref[0], 0.0)
    return top, bot


POOL_TM = 1024
POOL_HALO = 512


def _pool_window_sums_1d(u, row_len):
    n = u.shape[0]
    col = lax.broadcasted_iota(jnp.int32, u.shape, 0) & (row_len - 1)
    grp = lax.broadcasted_iota(jnp.int32, u.shape, 1) // POOL_GW

    def back(x, s):
        return jnp.where(col >= s, pltpu.roll(x, s, 0), 0.0)

    def fwd(x, s):
        return jnp.where(col < row_len - s, pltpu.roll(x, n - s, 0), 0.0)

    b = back(u, 1)
    f = u
    out = b + f
    for gi in range(1, len(POOL_WINDOWS)):
        s = POOL_WINDOWS[gi] // 4
        b = b + back(b, s)
        f = f + fwd(f, s)
        out = jnp.where(grp >= gi, b + f, out)
    return out


def _pool_counts(idx, extent, halfw):
    return jnp.minimum(idx + halfw, extent) - jnp.maximum(idx - halfw, 0)


def _pool_finish(mean, u, wbd_ref, scale_ref, o_ref):
    d = (mean - u).astype(BF16)
    o_ref[0] = jnp.dot(d, wbd_ref[...], preferred_element_type=F32) * scale_ref[...]


def _pool_grid_kernel(prev_ref, cur_ref, next_ref, wbd_ref, scale_ref, o_ref, *, n_rows):
    tm = cur_ref.shape[1]
    top, bot = _edge_masked(prev_ref, next_ref)
    u = cur_ref[0]
    grp = lax.broadcasted_iota(jnp.int32, (tm, GROUP_W), 1) // POOL_GW
    arr, off = jnp.concatenate([top, u, bot], axis=0), 0
    tot = None
    for gi, w in enumerate(POOL_WINDOWS):
        sh = GRID_W * max(w // 4, 1) if gi else GRID_W
        if gi == 0:
            arr = arr[:-sh] + arr[sh:]
            off = sh
        else:
            arr = arr[:-2 * sh] + arr[2 * sh:]
            off = off + sh
        centre = arr[POOL_HALO - off:POOL_HALO - off + tm]
        tot = centre if tot is None else jnp.where(grp >= gi, centre, tot)
    tot = _pool_window_sums_1d(tot, GRID_W)
    tok = pl.program_id(1) * tm + lax.broadcasted_iota(jnp.int32, (tm, GROUP_W), 0)
    halfw = jnp.left_shift(1, grp)
    cnt = (_pool_counts(tok // GRID_W, n_rows, halfw) * _pool_counts(tok & (GRID_W - 1), GRID_W, halfw))
    _pool_finish(tot / cnt.astype(F32), u, wbd_ref, scale_ref, o_ref)


def _pool_seq_kernel(u_ref, wbd_ref, scale_ref, o_ref):
    u = u_ref[0]
    n = u.shape[0]
    tot = _pool_window_sums_1d(u, n)
    grp = lax.broadcasted_iota(jnp.int32, u.shape, 1) // POOL_GW
    tok = lax.broadcasted_iota(jnp.int32, u.shape, 0)
    cnt = _pool_counts(tok, n, jnp.left_shift(1, grp))
    _pool_finish(tot / cnt.astype(F32), u, wbd_ref, scale_ref, o_ref)


def _pool_mix(u, wbd, scale, grid_rows):
    b, t, w = u.shape
    out_shape = jax.ShapeDtypeStruct(u.shape, F32)
    if grid_rows is None:
        tok = pl.BlockSpec((1, t, w), lambda i: (i, 0, 0))
        return pl.pallas_call(
            _pool_seq_kernel, grid=(b,),
            in_specs=[tok, _resident(wbd.shape), _resident(scale.shape)],
            out_specs=tok, out_shape=out_shape, compiler_params=_params(1), name="pool_seq",
        )(u, wbd, scale)
    prev, cur, nxt = _halo_specs(POOL_TM, POOL_HALO, t, w)
    return pl.pallas_call(
        functools.partial(_pool_grid_kernel, n_rows=grid_rows),
        grid=(b, t // POOL_TM),
        in_specs=[prev, cur, nxt, _resident(wbd.shape), _resident(scale.shape)],
        out_specs=cur, out_shape=out_shape, compiler_params=_params(2), name="pool_grid",
    )(u, u, u, wbd, scale)


CONV_TM = 512
CONV_HALO = 16
CONV_SUB = 64


def _glu(x):
    return x[:, :GROUP_W] * jax.nn.sigmoid(x[:, GROUP_W:])


def _conv_kernel(prev_ref, cur_ref, next_ref, dww_ref, dwb_ref, lng_ref, lnb_ref, pw_ref, o_ref,
                 ext_ref, sh_ref):
    tm = cur_ref.shape[1]
    top, bot = _edge_masked(prev_ref, next_ref)
    ext_ref[0:CONV_HALO] = _glu(top)
    ext_ref[CONV_HALO:CONV_HALO + tm] = _glu(cur_ref[0])
    ext_ref[CONV_HALO + tm:] = _glu(bot)
    n_sh = sh_ref.shape[1]
    for s in range(SUBLANES):
        sh_ref[s] = ext_ref[s:s + n_sh, :]
    base = CONV_HALO - CONV_K // 2
    for r0 in range(0, tm, CONV_SUB):
        acc = jnp.zeros((CONV_SUB, GROUP_W), F32)
        for k in range(CONV_K):
            a, s = divmod(base + k, SUBLANES)
            acc = acc + dww_ref[k:k + 1, :] * sh_ref[s, r0 + a * SUBLANES:r0 + a * SUBLANES + CONV_SUB, :]
        h = acc + dwb_ref[...]
        mu = jnp.mean(h, axis=-1, keepdims=True)
        var = jnp.mean(jnp.square(h - mu), axis=-1, keepdims=True)
        h = (h - mu) * lax.rsqrt(var + EPS) * lng_ref[...] + lnb_ref[...]
        h = (h * jax.nn.sigmoid(h)).astype(BF16)
        o_ref[0, r0:r0 + CONV_SUB, :] = jnp.dot(h, pw_ref[...], preferred_element_type=F32)


def _conv_mix(u2, dw_w, dw_b, ln_g, ln_b, pw_w, tm):
    b, t, w2 = u2.shape
    prev, cur, nxt = _halo_specs(tm, CONV_HALO, t, w2)
    row = _resident((1, GROUP_W))
    return pl.pallas_call(
        _conv_kernel,
        grid=(b, t // tm),
        in_specs=[prev, cur, nxt, _resident(dw_w.shape), row, row, row, _resident(pw_w.shape)],
        out_specs=pl.BlockSpec((1, tm, GROUP_W), lambda i, j: (i, j, 0)),
        out_shape=jax.ShapeDtypeStruct((b, t, GROUP_W), F32),
        scratch_shapes=[pltpu.VMEM((tm + 2 * CONV_HALO, GROUP_W), F32),
                        pltpu.VMEM((SUBLANES, tm + 2 * CONV_HALO - SUBLANES, GROUP_W), F32)],
        compiler_params=_params(2), name="conv_mix",
    )(u2, u2, u2, dw_w, dw_b, ln_g, ln_b, pw_w)


FOUR_N1 = 128
FOUR_TT = 8
FOUR_KT = 8


def _hilo(w):
    w = jnp.asarray(w, F32)
    hi = w.astype(BF16)
    return hi, (w - hi.astype(F32)).astype(BF16)


def _dot_hp(x, wh, wl):
    xh = x.astype(BF16)
    xl = (x - xh.astype(F32)).astype(BF16)
    return (jnp.dot(xh, wh, preferred_element_type=F32) + jnp.dot(xh, wl, preferred_element_type=F32)
            + jnp.dot(xl, wh, preferred_element_type=F32))


def _dot_hp_left(wh, wl, x):
    xh = x.astype(BF16)
    xl = (x - xh.astype(F32)).astype(BF16)
    return (jnp.dot(wh, xh, preferred_element_type=F32) + jnp.dot(wl, xh, preferred_element_type=F32)
            + jnp.dot(wh, xl, preferred_element_type=F32))


def _cos_sin(n, rows=None, cols=None):
    r = np.arange(n if rows is None else rows, dtype=np.int64)
    c = np.arange(n if cols is None else cols, dtype=np.int64)
    ang = 2.0 * np.pi * ((np.outer(r, c) % n).astype(np.float64) / n)
    return np.cos(ang), np.sin(ang)


def _channel_dft():
    c, s = _cos_sin(FOURIER_GW)
    eye = np.eye(FOURIER_GROUPS)
    return np.concatenate([np.kron(eye, c), -np.kron(eye, s)], axis=1)


def _fourier_a_kernel(ua_ref, ub_ref, cdh_ref, cdl_ref, f1h_ref, f1l_ref, twc_ref, tws_ref, o_ref):
    for j in range(FOUR_TT):
        x = jnp.concatenate([ua_ref[:, j, :], ub_ref[:, j, :]], axis=1)
        z = _dot_hp(x, cdh_ref[...], cdl_ref[...])
        zz = jnp.concatenate([z[:, :GROUP_W], z[:, GROUP_W:]], axis=0)
        a = _dot_hp_left(f1h_ref[...], f1l_ref[...], zz)
        ar, ai = a[:FOUR_N1], a[FOUR_N1:]
        c = jnp.concatenate([twc_ref[j]] * (GROUP_W // LANES), axis=1)
        s = jnp.concatenate([tws_ref[j]] * (GROUP_W // LANES), axis=1)
        o_ref[0, 0, j] = ar * c + ai * s
        o_ref[0, 1, j] = ai * c - ar * s


def _fourier_b_kernel(ga_ref, gb_ref, f2h_ref, f2l_ref, fw_ref, oa_ref, ob_ref, *, scale):
    for kk in range(FOUR_KT):
        gm = jnp.concatenate([ga_ref[:, :, kk, :], gb_ref[:, :, kk, :]], axis=-1)
        gm = gm.reshape(2 * FOUR_N1, GROUP_W)
        f = _dot_hp_left(f2h_ref[...], f2l_ref[...], gm) * scale
        y = jnp.dot(f.astype(BF16), fw_ref[...], preferred_element_type=F32)
        oa_ref[:, kk, :] = y[:, :LANES]
        ob_ref[:, kk, :] = y[:, LANES:]


def _fourier_small_kernel(u_ref, cdh_ref, cdl_ref, fh_ref, fl_ref, fw_ref, oa_ref, ob_ref, *, scale):
    z = _dot_hp(u_ref[0], cdh_ref[...], cdl_ref[...])
    zz = jnp.concatenate([z[:, :GROUP_W], z[:, GROUP_W:]], axis=0)
    f = _dot_hp_left(fh_ref[...], fl_ref[...], zz) * scale
    y = jnp.dot(f.astype(BF16), fw_ref[...], preferred_element_type=F32)
    oa_ref[0] = y[:, :LANES]
    ob_ref[0] = y[:, LANES:]


def _fourier_mix(u, fw):
    b, n, w = u.shape
    scale = 1.0 / math.sqrt(n * FOURIER_GW)
    cdh, cdl = _hilo(_channel_dft())
    if n != FOUR_N1 * FOUR_N1:
        c, s = _cos_sin(n)
        fh, fl = _hilo(np.concatenate([c, s], axis=1))
        tok = pl.BlockSpec((1, n, w), lambda i: (i, 0, 0))
        half = pl.BlockSpec((1, n, LANES), lambda i: (i, 0, 0))
        return pl.pallas_call(
            functools.partial(_fourier_small_kernel, scale=scale), grid=(b,),
            in_specs=[tok] + [_resident(a.shape) for a in (cdh, cdl, fh, fl, fw)],
            out_specs=[half, half], out_shape=[jax.ShapeDtypeStruct((b, n, LANES), F32)] * 2,
            compiler_params=_params(1), name="fourier_small",
        )(u, cdh, cdl, fh, fl, fw)
    n1 = FOUR_N1
    c1, s1 = _cos_sin(n1)
    f1h, f1l = _hilo(np.block([[c1, s1], [-s1, c1]]))
    f2h, f2l = _hilo(np.concatenate([c1, s1], axis=1))
    twc, tws = _cos_sin(n, rows=n1, cols=n1)
    twc = jnp.broadcast_to(jnp.asarray(twc, F32)[:, :, None], (n1, n1, LANES))
    tws = jnp.broadcast_to(jnp.asarray(tws, F32)[:, :, None], (n1, n1, LANES))
    tw_spec = pl.BlockSpec((FOUR_TT, n1, LANES), lambda i, j: (j, 0, 0))
    u4 = u.reshape(b, n1, n1, w)
    g = pl.pallas_call(
        _fourier_a_kernel, grid=(b, n1 // FOUR_TT),
        in_specs=[pl.BlockSpec((None, n1, FOUR_TT, LANES), lambda i, j: (i, 0, j, 0)),
                  pl.BlockSpec((None, n1, FOUR_TT, LANES), lambda i, j: (i, 0, j, 1))]
        + [_resident(a.shape) for a in (cdh, cdl, f1h, f1l)] + [tw_spec, tw_spec],
        out_specs=pl.BlockSpec((1, 2, FOUR_TT, n1, w), lambda i, j: (i, 0, j, 0, 0)),
        out_shape=jax.ShapeDtypeStruct((b, 2, n1, n1, w), F32),
        compiler_params=_params(2), name="fourier_a",
    )(u4, u4, cdh, cdl, f1h, f1l, twc, tws)
    half_out = pl.BlockSpec((None, n1, FOUR_KT, LANES), lambda i, j: (i, 0, j, 0))
    ya, yb = pl.pallas_call(
        functools.partial(_fourier_b_kernel, scale=scale), grid=(b, n1 // FOUR_KT),
        in_specs=[pl.BlockSpec((None, 2, n1, FOUR_KT, LANES), lambda i, j: (i, 0, 0, j, 0)),
                  pl.BlockSpec((None, 2, n1, FOUR_KT, LANES), lambda i, j: (i, 0, 0, j, 1))]
        + [_resident(a.shape) for a in (f2h, f2l, fw)],
        out_specs=[half_out, half_out],
        out_shape=[jax.ShapeDtypeStruct((b, n1, n1, LANES), F32)] * 2,
        compiler_params=_params(2), name="fourier_b",
    )(g, g, f2h, f2l, fw)
    return ya.reshape(b, n, LANES), yb.reshape(b, n, LANES)


GDN_C = GDN_CHUNK
GDN_PREP_TM = 512
GDN_SCAN_TM = 512
GDN_SUB = 64
GDN_GROUP = 512
N_GATE = 4 * GDN_HEADS


def _softplus(x):
    return jnp.maximum(x, 0.0) + jnp.log1p(jnp.exp(-jnp.abs(x)))


def _gdn_consts():
    i = np.arange(GDN_C)
    lower = np.concatenate([i[None, :] <= i[:, None], i[None, :] >= i[:, None]], axis=0)
    same = np.ones((GDN_C, GDN_C), bool)
    expand = np.zeros((AB_PAD, 4 * GROUP_W), np.float32)
    for s in range(4):
        for h in range(GDN_HEADS):
            c0 = s * GROUP_W + h * GDN_HEAD_DIM
            expand[s * GDN_HEADS + h, c0:c0 + GDN_HEAD_DIM] = 1.0
    lane = np.arange(GROUP_W)
    headones = (lane[:, None] // GDN_HEAD_DIM) == (lane[None, :] // GDN_HEAD_DIM)
    as_bf = lambda a: jnp.asarray(a, F32).astype(BF16)
    return as_bf(lower), as_bf(same), as_bf(expand), as_bf(headones)


def _bd(x, bdmask):
    xb = x.astype(BF16)
    return jnp.where(bdmask, jnp.concatenate([xb] * GDN_HEADS, axis=0), jnp.zeros((), BF16))


def _bd_mask():
    return (lax.broadcasted_iota(jnp.int32, (GROUP_W, GROUP_W), 0) // GDN_HEAD_DIM
            == lax.broadcasted_iota(jnp.int32, (GROUP_W, GROUP_W), 1) // GDN_HEAD_DIM)


def _dot(a, b):
    return jnp.dot(a.astype(BF16), b.astype(BF16), preferred_element_type=F32)


def _diag_blocks(full):
    head = (lax.broadcasted_iota(jnp.int32, (GDN_HEAD_DIM, full.shape[1]), 1) // GDN_HEAD_DIM) % GDN_HEADS
    out = None
    for h in range(GDN_HEADS):
        blk = jnp.where(head == h, full[h * GDN_HEAD_DIM:(h + 1) * GDN_HEAD_DIM], 0.0)
        out = blk if out is None else out + blk
    return out


def _tri_inverse(ms, eye, level_masks, bdmask):
    ds = [eye - jnp.where(level_masks[0], m, 0.0) for m in ms]
    for mask in level_masks[1:]:
        es = [_dot(jnp.where(mask, m, 0.0), _bd(d, bdmask)) for m, d in zip(ms, ds)]
        fs = [_dot(d, _bd(e, bdmask)) for d, e in zip(ds, es)]
        ds = [d - f for d, f in zip(ds, fs)]
    return ds


def _gdn_prep_kernel(prev_ref, cur_ref, next_ref, *refs):
    tm = cur_ref.shape[1]
    halo = prev_ref.shape[1]
    ext_ref = refs[-4]
    top, bot = _edge_masked(prev_ref, next_ref)
    ext_ref[0:halo] = top
    ext_ref[halo:halo + tm] = cur_ref[0]
    ext_ref[halo + tm:] = bot
    grp = min(tm, GDN_GROUP)
    for g0 in range(0, tm, grp):
        _gdn_prep_group(g0, grp, halo, *refs)


def _gdn_prep_group(g0, grp, halo, ab_ref, cw_ref, alog_ref, dtb_ref,
                    tri_ref, same_ref, exp_ref, ones_ref,
                    pf_ref, qpf_ref, bmf_ref, o0f_ref, df_ref, pb_ref, qpb_ref, bmb_ref, o0b_ref, db_ref,
                    ext_ref, q_scr, k_scr, v_scr):
    c = GDN_C
    ls = [slice(ci * c, (ci + 1) * c) for ci in range(grp // c)]
    rs = [slice(g0 + ci * c, g0 + (ci + 1) * c) for ci in range(grp // c)]
    rows_g = slice(g0, g0 + grp)
    base = halo - GDN_CONV // 2
    for r0 in range(g0, g0 + grp, GDN_SUB):
        acc = jnp.zeros((GDN_SUB, 3 * GROUP_W), F32)
        for t in range(GDN_CONV):
            acc = acc + cw_ref[t:t + 1, :] * ext_ref[r0 + base + t:r0 + base + t + GDN_SUB, :]
        x = acc * jax.nn.sigmoid(acc)
        rows = slice(r0, r0 + GDN_SUB)
        q_scr[rows, :] = x[:, :GROUP_W]
        k_scr[rows, :] = x[:, GROUP_W:2 * GROUP_W]
        v_scr[rows, :] = x[:, 2 * GROUP_W:]
    qa, ka = q_scr[rows_g, :], k_scr[rows_g, :]
    q_scr[rows_g, :] = qa * lax.rsqrt(_dot01(qa * qa, ones_ref[...], 2) + EPS) * (GDN_HEAD_DIM ** -0.5)
    k_scr[rows_g, :] = ka * lax.rsqrt(_dot01(ka * ka, ones_ref[...], 2) + EPS)

    ab = ab_ref[0, rows_g, :]
    lane = lax.broadcasted_iota(jnp.int32, ab.shape, 1)
    g = -jnp.exp(alog_ref[...]) * _softplus(ab + dtb_ref[...])
    gate = jnp.where(lane < 2 * GDN_HEADS, g, jax.nn.sigmoid(ab))
    cums = [_dot01_left(tri_ref[...], gate[r], 2) for r in ls]
    lane_c = lax.broadcasted_iota(jnp.int32, (c, AB_PAD), 1)
    cum = jnp.concatenate([jnp.where(lane_c < GDN_HEADS, cm[:c], cm[c:]) for cm in cums], axis=0)
    wide = _dot01(jnp.where(lane < 2 * GDN_HEADS, cum, gate), exp_ref[...], 2)
    gcs = (wide[:, :GROUP_W], wide[:, GROUP_W:2 * GROUP_W])
    betas = (wide[:, 2 * GROUP_W:3 * GROUP_W], wide[:, 3 * GROUP_W:])

    ii = lax.broadcasted_iota(jnp.int32, (c, GROUP_W), 0)
    jj = lax.broadcasted_iota(jnp.int32, (c, GROUP_W), 1) & (c - 1)
    diag2 = jnp.concatenate([ii == jj] * 2, axis=1)
    rows_gc = [_dot01_left(same_ref[...], jnp.where(diag2, wide[r, :2 * GROUP_W], 0.0), 2) for r in ls]
    gams = ([jnp.where(ii >= jj, jnp.exp(jnp.minimum(gcs[0][r] - rg[:, :GROUP_W], 0.0)), 0.0)
             for r, rg in zip(ls, rows_gc)],
            [jnp.where(ii <= jj, jnp.exp(jnp.minimum(gcs[1][r] - rg[:, GROUP_W:], 0.0)), 0.0)
             for r, rg in zip(ls, rows_gc)])
    eye = (ii == jj).astype(F32)
    stricts = (ii > jj, ii < jj)
    levels = [((ii // (2 * s)) == (jj // (2 * s))) & ((ii // s) != (jj // s)) for s in (1, 2, 4, 8, 16, 32)]
    bdmask = _bd_mask()
    outs = ((pf_ref, qpf_ref, bmf_ref, o0f_ref, df_ref), (pb_ref, qpb_ref, bmb_ref, o0b_ref, db_ref))

    qs = [q_scr[r, :] for r in rs]
    ks = [k_scr[r, :] for r in rs]
    vs = [v_scr[r, :] for r in rs]
    kqs = [lax.dot_general(jnp.concatenate([k, q], axis=0).astype(BF16), _bd(k, bdmask),
                           (((1,), (1,)), ((), ())), preferred_element_type=F32) for k, q in zip(ks, qs)]
    for di in range(2):
        gc_l = [gcs[di][r] for r in ls]
        gam_l = gams[di]
        beta_l = [betas[di][r] for r in ls]
        ms = [jnp.where(stricts[di], beta * kq[:c] * gam, 0.0) for beta, kq, gam in zip(beta_l, kqs, gam_l)]
        ts = _tri_inverse(ms, eye, levels, bdmask)
        egcs = [jnp.exp(gc) for gc in gc_l]
        uws = [_dot(t, jnp.concatenate([_bd(v * beta, bdmask), _bd(k * beta * egc, bdmask)], axis=1))
               for t, v, k, beta, egc in zip(ts, vs, ks, beta_l, egcs)]
        aqks = [kq[c:] * gam for kq, gam in zip(kqs, gam_l)]
        ows = [_dot(aqk, jnp.concatenate([_bd(uw[:, :GROUP_W], bdmask), _bd(uw[:, GROUP_W:], bdmask)], axis=1))
               for aqk, uw in zip(aqks, uws)]
        p_ref, qp_ref, bm_ref, o0_ref, d_ref = outs[di]
        for ci, r in enumerate(rs):
            gc, uw, ow = gc_l[ci], uws[ci], ows[ci]
            glast = gc[0:1] if di else gc[c - 1:c]
            kdec = ks[ci] * jnp.exp(glast - gc)
            bp = _diag_blocks(lax.dot_general(kdec.astype(BF16), uw.astype(BF16), (((0,), (0,)), ((), ())),
                                              preferred_element_type=F32))
            bm_ref[0, r, :] = bp[:, :GROUP_W]
            p_ref[0, r, :] = bp[:, GROUP_W:].astype(BF16)
            o0_ref[0, r, :] = ow[:, :GROUP_W]
            qp_ref[0, r, :] = (qs[ci] * egcs[ci] - ow[:, GROUP_W:]).astype(BF16)
            d_ref[0, g0 // c + ci:g0 // c + ci + 1, :] = jnp.exp(glast)


def _gdn_prep(qkv, ab, conv_w, a_log, dt_bias, tm):
    b, t, w3 = qkv.shape
    halo = 8
    nck = tm // GDN_C
    prev, cur, nxt = _halo_specs(tm, halo, t, w3)
    pad = lambda a: jnp.pad(a.reshape(1, -1), ((0, 0), (0, AB_PAD - a.size)))
    consts = _gdn_consts()
    tok = pl.BlockSpec((1, tm, GROUP_W), lambda i, j: (i, j, 0))
    dec = pl.BlockSpec((1, nck, GROUP_W), lambda i, j: (i, j, 0))
    tok_shape = lambda dt: jax.ShapeDtypeStruct((b, t, GROUP_W), dt)
    dir_specs = [tok, tok, tok, tok, dec]
    dir_shapes = [tok_shape(BF16), tok_shape(BF16), tok_shape(F32), tok_shape(F32),
                  jax.ShapeDtypeStruct((b, t // GDN_C, GROUP_W), F32)]
    return pl.pallas_call(
        _gdn_prep_kernel,
        grid=(b, t // tm),
        in_specs=[prev, cur, nxt, pl.BlockSpec((1, tm, AB_PAD), lambda i, j: (i, j, 0)),
                  _resident(conv_w.shape), _resident((1, AB_PAD)), _resident((1, AB_PAD))]
        + [_resident(cst.shape) for cst in consts],
        out_specs=dir_specs * 2,
        out_shape=dir_shapes * 2,
        scratch_shapes=[pltpu.VMEM((tm + 2 * halo, w3), F32)] + [pltpu.VMEM((tm, GROUP_W), F32)] * 3,
        compiler_params=_params(2), name="gdn_prep",
    )(qkv, qkv, qkv, ab, conv_w, pad(a_log), pad(dt_bias), *consts)


def _gdn_scan_kernel(pf_ref, qpf_ref, bmf_ref, o0f_ref, df_ref, pb_ref, qpb_ref, bmb_ref, o0b_ref, db_ref,
                     s0f_ref, s0b_ref, of_ref, ob_ref, sf_ref, sb_ref):
    nb, tm = pf_ref.shape[0], pf_ref.shape[1]
    nc = tm // GDN_C

    @pl.when(pl.program_id(0) == 0)
    def _():
        sf_ref[...] = s0f_ref[...]
        sb_ref[...] = s0b_ref[...]

    bdmask = _bd_mask()
    fwd = (pf_ref, qpf_ref, bmf_ref, o0f_ref, df_ref, of_ref)
    bwd = (pb_ref, qpb_ref, bmb_ref, o0b_ref, db_ref, ob_ref)
    chains = [(fwd, bi, False) for bi in range(nb)] + [(bwd, bi, True) for bi in range(nb)]
    states = [sf_ref[bi] for bi in range(nb)] + [sb_ref[bi] for bi in range(nb)]
    for step in range(nc):
        lhs, rows = [], []
        for (p_ref, qp_ref, _, _, _, _), bi, rev in chains:
            ci = nc - 1 - step if rev else step
            r = slice(ci * GDN_C, (ci + 1) * GDN_C)
            rows.append((ci, r))
            lhs.append(jnp.concatenate([p_ref[bi, r, :], qp_ref[bi, r, :]], axis=0))
        res = [jnp.dot(a, _bd(s, bdmask), preferred_element_type=F32) for a, s in zip(lhs, states)]
        new_states = []
        for (_, _, bm_ref, o0_ref, d_ref, o_ref), bi, _ in chains:
            n = len(new_states)
            ci, r = rows[n]
            o_ref[bi, r, :] = o0_ref[bi, r, :] + res[n][GDN_C:]
            new_states.append(d_ref[bi, ci:ci + 1, :] * states[n] - res[n][:GDN_C] + bm_ref[bi, r, :])
        states = new_states
    for bi in range(nb):
        sf_ref[bi] = states[bi]
        sb_ref[bi] = states[nb + bi]


def _gdn_scan(prep, s0f, s0b, tm):
    b, t, w = prep[0].shape
    nj = t // tm
    nck = tm // GDN_C
    fwd = pl.BlockSpec((b, tm, w), lambda j: (0, j, 0))
    bwd = pl.BlockSpec((b, tm, w), lambda j: (0, nj - 1 - j, 0))
    dfwd = pl.BlockSpec((b, nck, w), lambda j: (0, j, 0))
    dbwd = pl.BlockSpec((b, nck, w), lambda j: (0, nj - 1 - j, 0))
    st = pl.BlockSpec((b, GDN_HEAD_DIM, w), lambda j: (0, 0, 0))
    st_shape = jax.ShapeDtypeStruct((b, GDN_HEAD_DIM, w), F32)
    return pl.pallas_call(
        _gdn_scan_kernel,
        grid=(nj,),
        in_specs=[fwd] * 4 + [dfwd] + [bwd] * 4 + [dbwd] + [st, st],
        out_specs=[fwd, bwd, st, st],
        out_shape=[jax.ShapeDtypeStruct((b, t, w), F32)] * 2 + [st_shape] * 2,
        compiler_params=pltpu.CompilerParams(dimension_semantics=("arbitrary",),
                                             vmem_limit_bytes=VMEM_LIMIT),
        name="gdn_scan",
    )(*prep, s0f, s0b)


def _mix_out_kernel(h_ref, yp_ref, yfa_ref, yfb_ref, yc_ref, of_ref, ob_ref, z_ref, gt_ref, nw_ref, ones_ref,
                    w_ref, o_ref):
    o = of_ref[0] + ob_ref[0]
    ms = _dot01(o * o, ones_ref[...], 2) * (1.0 / GDN_HEAD_DIM)
    z = z_ref[0]
    yg = o * lax.rsqrt(ms + EPS) * nw_ref[...] * (z * jax.nn.sigmoid(z))
    y = jnp.zeros(h_ref.shape[1:], F32)
    yf = jnp.concatenate([yfa_ref[0], yfb_ref[0]], axis=-1)
    for gi, part in enumerate((yp_ref[0], yf, yc_ref[0], yg)):
        y = y + jnp.dot(part.astype(BF16), w_ref[gi * GROUP_W:(gi + 1) * GROUP_W, :],
                        preferred_element_type=F32)
    o_ref[0] = h_ref[0] + gt_ref[0] * y


def _mix_out(h, ys, o_f, o_b, z, gate, gdn_nw, w_out, layer, tm):
    b, t, d = h.shape
    tok = pl.BlockSpec((1, tm, d), lambda i, j: (i, j, 0))
    grp = pl.BlockSpec((1, tm, GROUP_W), lambda i, j: (i, j, 0))
    half = pl.BlockSpec((1, tm, LANES), lambda i, j: (i, j, 0))
    ones = _gdn_consts()[3]
    return pl.pallas_call(
        _mix_out_kernel,
        grid=(b, t // tm),
        in_specs=[tok, grp, half, half] + [grp] * 4 + [pl.BlockSpec((1, 1, d), lambda i, j: (i, 0, 0)),
                                     _resident((1, GROUP_W)), _resident(ones.shape), _layer_resident(w_out, layer)],
        out_specs=tok,
        out_shape=jax.ShapeDtypeStruct(h.shape, F32),
        compiler_params=_params(2), name="mix_out",
    )(h, *ys, o_f, o_b, z, gate, gdn_nw, ones, w_out)


MOD_ROWS = 8
MOD_TN = 1152


def _mod_kernel(c_ref, w_ref, b_ref, o_ref):
    cv = c_ref[...]
    a = (cv * jax.nn.sigmoid(cv)).astype(BF16)
    o_ref[0] = jnp.dot(a, w_ref[0].astype(BF16), preferred_element_type=F32) + b_ref[0]


def _modulation(c, c_ctx, mod_w, mod_b):
    nl, d, n = mod_w.shape
    cond = jnp.concatenate([c, c_ctx[None, :]], axis=0)
    cond = jnp.pad(cond, ((0, MOD_ROWS - cond.shape[0]), (0, 0)))
    return pl.pallas_call(
        _mod_kernel,
        grid=(nl, n // MOD_TN),
        in_specs=[_resident(cond.shape), pl.BlockSpec((1, d, MOD_TN), lambda i, j: (i, 0, j)),
                  pl.BlockSpec((1, 1, MOD_TN), lambda i, j: (i, 0, j))],
        out_specs=pl.BlockSpec((1, MOD_ROWS, MOD_TN), lambda i, j: (i, 0, j)),
        out_shape=jax.ShapeDtypeStruct((nl, MOD_ROWS, n), F32),
        compiler_params=_params(2), name="modulation",
    )(cond, mod_w, mod_b[:, None, :])


def _token_mix(p, gdn_state, grid_rows, tm, wl, need_out):
    p_pool, p_four, p_conv, p_qkv, p_z, p_ab = p
    prep = _gdn_prep(p_qkv, p_ab, wl["gdn_conv_w"], wl["gdn_a_log"], wl["gdn_dt_bias"], tm)
    o_f, o_b, s_f, s_b = _gdn_scan(prep, *gdn_state, tm)
    if not need_out:
        return None, (s_f, s_b)
    ys = (_pool_mix(p_pool, wl["pool_wbd"], wl["pool_scale"], grid_rows),
          *_fourier_mix(p_four, wl["fourier_w"]),
          _conv_mix(p_conv, wl["conv_dw_w"], wl["conv_dw_b"], wl["conv_ln_g"], wl["conv_ln_b"],
                    wl["conv_pw_w"], tm))
    return (ys, o_f, o_b, p_z), (s_f, s_b)


def kernel(x, c, ctx, c_ctx, mod_w, mod_b, norm_w, ffn1_wg, ffn1_wu, ffn1_wd, ffn2_wg, ffn2_wu,
           ffn2_wd, w_in, w_out, pool_w, pool_scale, fourier_w, conv_dw_w, conv_dw_b, conv_ln_g,
           conv_ln_b, conv_pw_w, gdn_conv_w, gdn_a_log, gdn_dt_bias, gdn_norm_w, final_norm_w):
    bsz, seq, d = x.shape
    n_ctx = ctx.shape[1]
    rows = seq // GRID_W
    tm_x, tm_c = 512, n_ctx
    hx, hc = x, ctx
    mods = _modulation(c, c_ctx, mod_w, mod_b)
    zero_state = (jnp.zeros((bsz, GDN_HEAD_DIM, GROUP_W), F32),) * 2
    f1 = tuple(_cast_bf16(w) for w in (ffn1_wg, ffn1_wu, ffn1_wd))
    f2 = tuple(_cast_bf16(w) for w in (ffn2_wg, ffn2_wu, ffn2_wd))
    w_main = _cast_bf16(w_in, cols=2 * GDN_OFF)
    w_ab = jnp.pad(w_in[:, :, 2 * GDN_OFF:], ((0, 0), (0, 0), (0, AB_PAD - N_GATE))).astype(BF16)
    w_out_b = _cast_bf16(w_out)
    for l in range(DEPTH):
        last = l == DEPTH - 1
        mx = [m[:, None, :] for m in jnp.split(mods[l, :bsz], N_MOD, axis=-1)]
        mc = [jnp.broadcast_to(m[None], (bsz, 1, d)) for m in jnp.split(mods[l, bsz:bsz + 1], N_MOD, axis=-1)]
        nw = norm_w[l][:, None, :]
        row = lambda a: a.reshape(1, -1)
        wl = dict(pool_wbd=_blockdiag(pool_w[l]).astype(BF16), pool_scale=row(pool_scale[l]),
                  fourier_w=fourier_w[l].astype(BF16), conv_dw_w=conv_dw_w[l], conv_dw_b=row(conv_dw_b[l]),
                  conv_ln_g=row(conv_ln_g[l]), conv_ln_b=row(conv_ln_b[l]),
                  conv_pw_w=conv_pw_w[l].astype(BF16), gdn_conv_w=gdn_conv_w[l],
                  gdn_a_log=gdn_a_log[l], gdn_dt_bias=gdn_dt_bias[l])
        gdn_nw = row(jnp.tile(gdn_norm_w[l], GDN_HEADS))

        hx = _ffn(hx, nw[0], mx[0], mx[1], mx[2], f1, l, tm_x)
        hc = _ffn(hc, nw[0], mc[0], mc[1], mc[2], f1, l, tm_c)

        px = _inproj(hx, nw[1], mx[3], mx[4], w_main, w_ab, l, False, tm_x)
        pc = _inproj(hc, nw[1], mc[3], mc[4], w_main, w_ab, l, last, tm_c)
        if last:
            pc = (None,) * 3 + tuple(pc)

        mix_c, ctx_state = _token_mix(pc, zero_state, None, tm_c, wl, not last)
        mix_x, _ = _token_mix(px, ctx_state, rows, tm_x, wl, True)
        hx = _mix_out(hx, *mix_x, mx[5], gdn_nw, w_out_b, l, tm_x)
        hx = _ffn(hx, nw[2], mx[6], mx[7], mx[8], f2, l, tm_x, final_norm_w[None, :] if last else None)
        if not last:
            hc = _mix_out(hc, *mix_c, mc[5], gdn_nw, w_out_b, l, tm_c)
            hc = _ffn(hc, nw[2], mc[6], mc[7], mc[8], f2, l, tm_c)
    return hx
```

```python
import functools
import math

import jax
import jax.numpy as jnp
import numpy as np
from jax import lax
from jax.experimental import pallas as pl
from jax.experimental.pallas import tpu as pltpu

D_MODEL = 1024
DEPTH = 4
GRID_W = 64
N_MIXERS = 4
GROUP_W = D_MODEL // N_MIXERS
POOL_WINDOWS = (2, 4, 8, 16)
POOL_GROUPS = 4
POOL_GW = GROUP_W // POOL_GROUPS
FOURIER_GROUPS = 4
FOURIER_GW = GROUP_W // FOURIER_GROUPS
CONV_K = 31
GDN_HEAD_DIM = 64
GDN_HEADS = GROUP_W // GDN_HEAD_DIM
GDN_CONV = 3
GDN_CHUNK = 64
FFN_HIDDEN = 128 * ((8 * D_MODEL // 3 + 127) // 128)
N_MOD = 9
EPS = 1e-6
POOL_OFF = 0
FOURIER_OFF = POOL_OFF + GROUP_W
CONV_OFF = FOURIER_OFF + GROUP_W
GDN_OFF = CONV_OFF + 2 * GROUP_W

LANES = 128
SUBLANES = 8
VMEM_LIMIT = 56 * 1024 * 1024
FFN_CHUNK = 256
AB_PAD = LANES

BF16 = jnp.bfloat16
F32 = jnp.float32


def _params(n_axes):
    return pltpu.CompilerParams(dimension_semantics=("parallel",) * n_axes,
                                vmem_limit_bytes=VMEM_LIMIT)


def _resident(shape):
    nd = len(shape)
    return pl.BlockSpec(shape, lambda *_: (0,) * nd, pipeline_mode=pl.Buffered(1))


def _layer_resident(stack, layer, cols=None, col_block=0):
    _, r, c = stack.shape
    return pl.BlockSpec((None, r, cols or c), lambda *_: (layer, 0, col_block), pipeline_mode=pl.Buffered(1))


def _cast_kernel(x_ref, o_ref):
    o_ref[...] = x_ref[...].astype(BF16)


def _cast_bf16(w, cols=None, rows_per_step=256):
    nl, r, c = w.shape
    cols = cols or c
    spec = pl.BlockSpec((1, rows_per_step, cols), lambda i, j: (i, j, 0))
    return pl.pallas_call(
        _cast_kernel, grid=(nl, r // rows_per_step), in_specs=[spec], out_specs=spec,
        out_shape=jax.ShapeDtypeStruct((nl, r, cols), BF16), compiler_params=_params(2), name="cast_bf16",
    )(w)


def _rms_mod(h, nw, shift, scale):
    ms = jnp.mean(h * h, axis=-1, keepdims=True)
    n = h * lax.rsqrt(ms + EPS) * nw
    return n * (1.0 + scale) + shift


def _ffn_apply(h, nw_ref, sh_ref, sc_ref, gt_ref, wg_ref, wu_ref, wd_ref, final_nw_ref=None):
    nb = _rms_mod(h, nw_ref[...], sh_ref[0], sc_ref[0]).astype(BF16)
    acc = jnp.zeros(h.shape, F32)
    for f0 in range(0, FFN_HIDDEN, FFN_CHUNK):
        g = jnp.dot(nb, wg_ref[:, f0:f0 + FFN_CHUNK], preferred_element_type=F32)
        u = jnp.dot(nb, wu_ref[:, f0:f0 + FFN_CHUNK], preferred_element_type=F32)
        a = (g * jax.nn.sigmoid(g) * u).astype(BF16)
        acc = acc + jnp.dot(a, wd_ref[f0:f0 + FFN_CHUNK, :], preferred_element_type=F32)
    out = h + (0.5 * gt_ref[0]) * acc
    if final_nw_ref is not None:
        out = out * lax.rsqrt(jnp.mean(out * out, axis=-1, keepdims=True) + EPS) * final_nw_ref[...]
    return out


def _ffn_kernel(h_ref, *rest):
    rest[-1][0] = _ffn_apply(h_ref[0], *rest[:-1])


def _ffn(h, nw, shift, scale, gate, weights, layer, tm, final_nw=None):
    b, t, d = h.shape
    vec = pl.BlockSpec((1, 1, d), lambda i, j: (i, 0, 0))
    tok = pl.BlockSpec((1, tm, d), lambda i, j: (i, j, 0))
    extra = [] if final_nw is None else [final_nw]
    return pl.pallas_call(
        _ffn_kernel,
        grid=(b, t // tm),
        in_specs=[tok, _resident((1, d)), vec, vec, vec] + [_layer_resident(w, layer) for w in weights]
        + [_resident((1, d))] * len(extra),
        out_specs=tok,
        out_shape=jax.ShapeDtypeStruct(h.shape, F32),
        compiler_params=_params(2),
        name="ffn",
    )(h, nw, shift, scale, gate, *weights, *extra)


IN_SPLITS = (GROUP_W, GROUP_W, 2 * GROUP_W, 3 * GROUP_W, GROUP_W, AB_PAD)


def _inproj_kernel(h_ref, nw_ref, sh_ref, sc_ref, w_ref, wab_ref, *o_refs):
    tm = h_ref.shape[1]
    half = tm // 2 if tm % (2 * SUBLANES) == 0 else tm
    for r0 in range(0, tm, half):
        rows = slice(r0, r0 + half)
        nb = _rms_mod(h_ref[0, rows, :], nw_ref[...], sh_ref[0], sc_ref[0]).astype(BF16)
        off = 0
        for o_ref in o_refs[:-1]:
            wdt = o_ref.shape[-1]
            o_ref[0, rows, :] = jnp.dot(nb, w_ref[:, off:off + wdt], preferred_element_type=F32)
            off += wdt
        o_refs[-1][0, rows, :] = jnp.dot(nb, wab_ref[...], preferred_element_type=F32)


def _inproj(h, nw, shift, scale, w_main, w_ab, layer, gdn_only, tm):
    b, t, d = h.shape
    vec = pl.BlockSpec((1, 1, d), lambda i, j: (i, 0, 0))
    tok = pl.BlockSpec((1, tm, d), lambda i, j: (i, j, 0))
    splits = IN_SPLITS[3:] if gdn_only else IN_SPLITS
    w_spec = (_layer_resident(w_main, layer, GDN_OFF, 1) if gdn_only else _layer_resident(w_main, layer))
    return pl.pallas_call(
        _inproj_kernel,
        grid=(b, t // tm),
        in_specs=[tok, _resident((1, d)), vec, vec, w_spec, _layer_resident(w_ab, layer)],
        out_specs=[pl.BlockSpec((1, tm, s), lambda i, j: (i, j, 0)) for s in splits],
        out_shape=[jax.ShapeDtypeStruct((b, t, s), F32) for s in splits],
        compiler_params=_params(2),
        name="inproj",
    )(h, nw, shift, scale, w_main, w_ab)


def _split_bf16(x, parts):
    out = []
    for _ in range(parts - 1):
        p = x.astype(BF16)
        out.append(p)
        x = x - p.astype(F32)
    return out + [x.astype(BF16)]


def _dot01(x, w01, parts=3):
    return sum(jnp.dot(p, w01, preferred_element_type=F32) for p in _split_bf16(x, parts))


def _dot01_left(w01, x, parts=3):
    return sum(jnp.dot(w01, p, preferred_element_type=F32) for p in _split_bf16(x, parts))


def _blockdiag(w):
    g, a, b = w.shape
    return jnp.einsum('gab,gh->gahb', w, jnp.eye(g, dtype=w.dtype)).reshape(g * a, g * b)


def _halo_specs(tm, halo, t, width):
    r = tm // halo
    last = t // halo - 1
    prev = pl.BlockSpec((1, halo, width), lambda i, j: (i, jnp.maximum(j * r - 1, 0), 0))
    cur = pl.BlockSpec((1, tm, width), lambda i, j: (i, j, 0))
    nxt = pl.BlockSpec((1, halo, width), lambda i, j: (i, jnp.minimum((j + 1) * r, last), 0))
    return prev, cur, nxt


def _edge_masked(prev_ref, next_ref):
    j, nj = pl.program_id(1), pl.num_programs(1)
    top = jnp.where(j > 0, prev_ref[0], 0.0)
    bot = jnp.where(j < nj - 1, next_ref[0], 0.0)
    return top, bot


POOL_TM = 1024
POOL_HALO = 512


def _pool_window_sums_1d(u, row_len):
    n = u.shape[0]
    col = lax.broadcasted_iota(jnp.int32, u.shape, 0) & (row_len - 1)
    grp = lax.broadcasted_iota(jnp.int32, u.shape, 1) // POOL_GW

    def back(x, s):
        return jnp.where(col >= s, pltpu.roll(x, s, 0), 0.0)

    def fwd(x, s):
        return jnp.where(col < row_len - s, pltpu.roll(x, n - s, 0), 0.0)

    b = back(u, 1)
    f = u
    out = b + f
    for gi in range(1, len(POOL_WINDOWS)):
        s = POOL_WINDOWS[gi] // 4
        b = b + back(b, s)
        f = f + fwd(f, s)
        out = jnp.where(grp >= gi, b + f, out)
    return out


def _pool_counts(idx, extent, halfw):
    return jnp.minimum(idx + halfw, extent) - jnp.maximum(idx - halfw, 0)


def _pool_finish(mean, u, wbd_ref, scale_ref, o_ref):
    d = (mean - u).astype(BF16)
    o_ref[0] = (jnp.dot(d, wbd_ref[...], preferred_element_type=F32) * scale_ref[...]).astype(BF16)


def _pool_grid_kernel(prev_ref, cur_ref, next_ref, wbd_ref, scale_ref, o_ref, *, n_rows):
    tm = cur_ref.shape[1]
    top, bot = _edge_masked(prev_ref, next_ref)
    u = cur_ref[0]
    grp = lax.broadcasted_iota(jnp.int32, (tm, GROUP_W), 1) // POOL_GW
    arr, off = jnp.concatenate([top, u, bot], axis=0), 0
    tot = None
    for gi, w in enumerate(POOL_WINDOWS):
        sh = GRID_W * max(w // 4, 1) if gi else GRID_W
        if gi == 0:
            arr = arr[:-sh] + arr[sh:]
            off = sh
        else:
            arr = arr[:-2 * sh] + arr[2 * sh:]
            off = off + sh
        centre = arr[POOL_HALO - off:POOL_HALO - off + tm]
        tot = centre if tot is None else jnp.where(grp >= gi, centre, tot)
    tot = _pool_window_sums_1d(tot, GRID_W)
    tok = pl.program_id(1) * tm + lax.broadcasted_iota(jnp.int32, (tm, GROUP_W), 0)
    halfw = jnp.left_shift(1, grp)
    cnt = (_pool_counts(tok // GRID_W, n_rows, halfw) * _pool_counts(tok & (GRID_W - 1), GRID_W, halfw))
    _pool_finish(tot / cnt.astype(F32), u, wbd_ref, scale_ref, o_ref)


def _pool_seq_kernel(u_ref, wbd_ref, scale_ref, o_ref):
    u = u_ref[0]
    n = u.shape[0]
    tot = _pool_window_sums_1d(u, n)
    grp = lax.broadcasted_iota(jnp.int32, u.shape, 1) // POOL_GW
    tok = lax.broadcasted_iota(jnp.int32, u.shape, 0)
    cnt = _pool_counts(tok, n, jnp.left_shift(1, grp))
    _pool_finish(tot / cnt.astype(F32), u, wbd_ref, scale_ref, o_ref)


def _pool_mix(u, wbd, scale, grid_rows):
    b, t, w = u.shape
    out_shape = jax.ShapeDtypeStruct(u.shape, BF16)
    if grid_rows is None:
        tok = pl.BlockSpec((1, t, w), lambda i: (i, 0, 0))
        return pl.pallas_call(
            _pool_seq_kernel, grid=(b,),
            in_specs=[tok, _resident(wbd.shape), _resident(scale.shape)],
            out_specs=tok, out_shape=out_shape, compiler_params=_params(1), name="pool_seq",
        )(u, wbd, scale)
    prev, cur, nxt = _halo_specs(POOL_TM, POOL_HALO, t, w)
    return pl.pallas_call(
        functools.partial(_pool_grid_kernel, n_rows=grid_rows),
        grid=(b, t // POOL_TM),
        in_specs=[prev, cur, nxt, _resident(wbd.shape), _resident(scale.shape)],
        out_specs=cur, out_shape=out_shape, compiler_params=_params(2), name="pool_grid",
    )(u, u, u, wbd, scale)


CONV_TM = 512
CONV_HALO = 16
CONV_SUB = 64


def _glu(x):
    return x[:, :GROUP_W] * jax.nn.sigmoid(x[:, GROUP_W:])


def _conv_kernel(prev_ref, cur_ref, next_ref, dww_ref, dwb_ref, lng_ref, lnb_ref, pw_ref, o_ref,
                 ext_ref, sh_ref):
    tm = cur_ref.shape[1]
    top, bot = _edge_masked(prev_ref, next_ref)
    ext_ref[0:CONV_HALO] = _glu(top)
    ext_ref[CONV_HALO:CONV_HALO + tm] = _glu(cur_ref[0])
    ext_ref[CONV_HALO + tm:] = _glu(bot)
    n_sh = sh_ref.shape[1]
    for s in range(SUBLANES):
        sh_ref[s] = ext_ref[s:s + n_sh, :]
    base = CONV_HALO - CONV_K // 2
    for r0 in range(0, tm, CONV_SUB):
        acc = jnp.zeros((CONV_SUB, GROUP_W), F32)
        for k in range(CONV_K):
            a, s = divmod(base + k, SUBLANES)
            acc = acc + dww_ref[k:k + 1, :] * sh_ref[s, r0 + a * SUBLANES:r0 + a * SUBLANES + CONV_SUB, :]
        h = acc + dwb_ref[...]
        mu = jnp.mean(h, axis=-1, keepdims=True)
        var = jnp.mean(jnp.square(h - mu), axis=-1, keepdims=True)
        h = (h - mu) * lax.rsqrt(var + EPS) * lng_ref[...] + lnb_ref[...]
        h = (h * jax.nn.sigmoid(h)).astype(BF16)
        o_ref[0, r0:r0 + CONV_SUB, :] = jnp.dot(h, pw_ref[...], preferred_element_type=F32).astype(BF16)


def _conv_mix(u2, dw_w, dw_b, ln_g, ln_b, pw_w, tm):
    b, t, w2 = u2.shape
    prev, cur, nxt = _halo_specs(tm, CONV_HALO, t, w2)
    row = _resident((1, GROUP_W))
    return pl.pallas_call(
        _conv_kernel,
        grid=(b, t // tm),
        in_specs=[prev, cur, nxt, _resident(dw_w.shape), row, row, row, _resident(pw_w.shape)],
        out_specs=pl.BlockSpec((1, tm, GROUP_W), lambda i, j: (i, j, 0)),
        out_shape=jax.ShapeDtypeStruct((b, t, GROUP_W), BF16),
        scratch_shapes=[pltpu.VMEM((tm + 2 * CONV_HALO, GROUP_W), F32),
                        pltpu.VMEM((SUBLANES, tm + 2 * CONV_HALO - SUBLANES, GROUP_W), F32)],
        compiler_params=_params(2), name="conv_mix",
    )(u2, u2, u2, dw_w, dw_b, ln_g, ln_b, pw_w)


FOUR_N1 = 128
FOUR_TT = 8
FOUR_KT = 8


def _hilo(w):
    w = jnp.asarray(w, F32)
    hi = w.astype(BF16)
    return hi, (w - hi.astype(F32)).astype(BF16)


def _dot_hp(x, wh, wl):
    xh = x.astype(BF16)
    xl = (x - xh.astype(F32)).astype(BF16)
    return (jnp.dot(xh, wh, preferred_element_type=F32) + jnp.dot(xh, wl, preferred_element_type=F32)
            + jnp.dot(xl, wh, preferred_element_type=F32))


def _dot_hp_left(wh, wl, x):
    xh = x.astype(BF16)
    xl = (x - xh.astype(F32)).astype(BF16)
    return (jnp.dot(wh, xh, preferred_element_type=F32) + jnp.dot(wl, xh, preferred_element_type=F32)
            + jnp.dot(wh, xl, preferred_element_type=F32))


def _cos_sin(n, rows=None, cols=None):
    r = np.arange(n if rows is None else rows, dtype=np.int64)
    c = np.arange(n if cols is None else cols, dtype=np.int64)
    ang = 2.0 * np.pi * ((np.outer(r, c) % n).astype(np.float64) / n)
    return np.cos(ang), np.sin(ang)


def _channel_dft():
    c, s = _cos_sin(FOURIER_GW)
    eye = np.eye(FOURIER_GROUPS)
    return np.concatenate([np.kron(eye, c), -np.kron(eye, s)], axis=1)


def _fourier_a_kernel(ua_ref, ub_ref, cdh_ref, cdl_ref, f1h_ref, f1l_ref, twc_ref, tws_ref, o_ref):
    for j in range(FOUR_TT):
        x = jnp.concatenate([ua_ref[:, j, :], ub_ref[:, j, :]], axis=1)
        z = _dot_hp(x, cdh_ref[...], cdl_ref[...])
        zz = jnp.concatenate([z[:, :GROUP_W], z[:, GROUP_W:]], axis=0)
        a = _dot_hp_left(f1h_ref[...], f1l_ref[...], zz)
        ar, ai = a[:FOUR_N1], a[FOUR_N1:]
        c = jnp.concatenate([twc_ref[j]] * (GROUP_W // LANES), axis=1)
        s = jnp.concatenate([tws_ref[j]] * (GROUP_W // LANES), axis=1)
        o_ref[0, 0, j] = ar * c + ai * s
        o_ref[0, 1, j] = ai * c - ar * s


def _fourier_b_kernel(ga_ref, gb_ref, f2h_ref, f2l_ref, fw_ref, oa_ref, ob_ref, *, scale):
    for kk in range(FOUR_KT):
        gm = jnp.concatenate([ga_ref[:, :, kk, :], gb_ref[:, :, kk, :]], axis=-1)
        gm = gm.reshape(2 * FOUR_N1, GROUP_W)
        f = _dot_hp_left(f2h_ref[...], f2l_ref[...], gm) * scale
        y = jnp.dot(f.astype(BF16), fw_ref[...], preferred_element_type=F32)
        oa_ref[:, kk, :] = y[:, :LANES]
        ob_ref[:, kk, :] = y[:, LANES:]


def _fourier_small_kernel(u_ref, cdh_ref, cdl_ref, fh_ref, fl_ref, fw_ref, oa_ref, ob_ref, *, scale):
    z = _dot_hp(u_ref[0], cdh_ref[...], cdl_ref[...])
    zz = jnp.concatenate([z[:, :GROUP_W], z[:, GROUP_W:]], axis=0)
    f = _dot_hp_left(fh_ref[...], fl_ref[...], zz) * scale
    y = jnp.dot(f.astype(BF16), fw_ref[...], preferred_element_type=F32)
    oa_ref[0] = y[:, :LANES]
    ob_ref[0] = y[:, LANES:]


def _fourier_mix(u, fw):
    b, n, w = u.shape
    scale = 1.0 / math.sqrt(n * FOURIER_GW)
    cdh, cdl = _hilo(_channel_dft())
    if n != FOUR_N1 * FOUR_N1:
        c, s = _cos_sin(n)
        fh, fl = _hilo(np.concatenate([c, s], axis=1))
        tok = pl.BlockSpec((1, n, w), lambda i: (i, 0, 0))
        half = pl.BlockSpec((1, n, LANES), lambda i: (i, 0, 0))
        return pl.pallas_call(
            functools.partial(_fourier_small_kernel, scale=scale), grid=(b,),
            in_specs=[tok] + [_resident(a.shape) for a in (cdh, cdl, fh, fl, fw)],
            out_specs=[half, half], out_shape=[jax.ShapeDtypeStruct((b, n, LANES), F32)] * 2,
            compiler_params=_params(1), name="fourier_small",
        )(u, cdh, cdl, fh, fl, fw)
    n1 = FOUR_N1
    c1, s1 = _cos_sin(n1)
    f1h, f1l = _hilo(np.block([[c1, s1], [-s1, c1]]))
    f2h, f2l = _hilo(np.concatenate([c1, s1], axis=1))
    twc, tws = _cos_sin(n, rows=n1, cols=n1)
    twc = jnp.broadcast_to(jnp.asarray(twc, F32)[:, :, None], (n1, n1, LANES))
    tws = jnp.broadcast_to(jnp.asarray(tws, F32)[:, :, None], (n1, n1, LANES))
    tw_spec = pl.BlockSpec((FOUR_TT, n1, LANES), lambda i, j: (j, 0, 0))
    u4 = u.reshape(b, n1, n1, w)
    g = pl.pallas_call(
        _fourier_a_kernel, grid=(b, n1 // FOUR_TT),
        in_specs=[pl.BlockSpec((None, n1, FOUR_TT, LANES), lambda i, j: (i, 0, j, 0)),
                  pl.BlockSpec((None, n1, FOUR_TT, LANES), lambda i, j: (i, 0, j, 1))]
        + [_resident(a.shape) for a in (cdh, cdl, f1h, f1l)] + [tw_spec, tw_spec],
        out_specs=pl.BlockSpec((1, 2, FOUR_TT, n1, w), lambda i, j: (i, 0, j, 0, 0)),
        out_shape=jax.ShapeDtypeStruct((b, 2, n1, n1, w), F32),
        compiler_params=_params(2), name="fourier_a",
    )(u4, u4, cdh, cdl, f1h, f1l, twc, tws)
    half_out = pl.BlockSpec((None, n1, FOUR_KT, LANES), lambda i, j: (i, 0, j, 0))
    ya, yb = pl.pallas_call(
        functools.partial(_fourier_b_kernel, scale=scale), grid=(b, n1 // FOUR_KT),
        in_specs=[pl.BlockSpec((None, 2, n1, FOUR_KT, LANES), lambda i, j: (i, 0, 0, j, 0)),
                  pl.BlockSpec((None, 2, n1, FOUR_KT, LANES), lambda i, j: (i, 0, 0, j, 1))]
        + [_resident(a.shape) for a in (f2h, f2l, fw)],
        out_specs=[half_out, half_out],
        out_shape=[jax.ShapeDtypeStruct((b, n1, n1, LANES), F32)] * 2,
        compiler_params=_params(2), name="fourier_b",
    )(g, g, f2h, f2l, fw)
    return ya.reshape(b, n, LANES), yb.reshape(b, n, LANES)


GDN_C = GDN_CHUNK
GDN_PREP_TM = 512
GDN_SCAN_TM = 512
GDN_SUB = 64
N_GATE = 4 * GDN_HEADS


def _softplus(x):
    return jnp.maximum(x, 0.0) + jnp.log1p(jnp.exp(-jnp.abs(x)))


def _gdn_consts():
    i = np.arange(GDN_C)
    lower = np.concatenate([i[None, :] <= i[:, None], i[None, :] >= i[:, None]], axis=0)
    same = np.ones((GDN_C, GDN_C), bool)
    expand = np.zeros((AB_PAD, 4 * GROUP_W), np.float32)
    for s in range(4):
        for h in range(GDN_HEADS):
            c0 = s * GROUP_W + h * GDN_HEAD_DIM
            expand[s * GDN_HEADS + h, c0:c0 + GDN_HEAD_DIM] = 1.0
    lane = np.arange(GROUP_W)
    headones = (lane[:, None] // GDN_HEAD_DIM) == (lane[None, :] // GDN_HEAD_DIM)
    as_bf = lambda a: jnp.asarray(a, F32).astype(BF16)
    return as_bf(lower), as_bf(same), as_bf(expand), as_bf(headones)


def _bd(x, bdmask):
    xb = x.astype(BF16)
    return jnp.where(bdmask, jnp.concatenate([xb] * GDN_HEADS, axis=0), jnp.zeros((), BF16))


def _bd_mask():
    return (lax.broadcasted_iota(jnp.int32, (GROUP_W, GROUP_W), 0) // GDN_HEAD_DIM
            == lax.broadcasted_iota(jnp.int32, (GROUP_W, GROUP_W), 1) // GDN_HEAD_DIM)


def _dot(a, b):
    return jnp.dot(a.astype(BF16), b.astype(BF16), preferred_element_type=F32)


def _diag_blocks(full):
    head = (lax.broadcasted_iota(jnp.int32, (GDN_HEAD_DIM, full.shape[1]), 1) // GDN_HEAD_DIM) % GDN_HEADS
    out = None
    for h in range(GDN_HEADS):
        blk = jnp.where(head == h, full[h * GDN_HEAD_DIM:(h + 1) * GDN_HEAD_DIM], 0.0)
        out = blk if out is None else out + blk
    return out


def _tri_inverse(ms, eye, level_masks, bdmask):
    ds = [eye - jnp.where(level_masks[0], m, 0.0) for m in ms]
    for mask in level_masks[1:]:
        es = [_dot(jnp.where(mask, m, 0.0), _bd(d, bdmask)) for m, d in zip(ms, ds)]
        fs = [_dot(d, _bd(e, bdmask)) for d, e in zip(ds, es)]
        ds = [d - f for d, f in zip(ds, fs)]
    return ds


def _gdn_prep_kernel(prev_ref, cur_ref, next_ref, ab_ref, cw_ref, alog_ref, dtb_ref,
                     tri_ref, same_ref, exp_ref, ones_ref,
                     pf_ref, qpf_ref, bmf_ref, o0f_ref, df_ref, pb_ref, qpb_ref, bmb_ref, o0b_ref, db_ref,
                     ext_ref, q_scr, k_scr, v_scr):
    tm = cur_ref.shape[1]
    halo = prev_ref.shape[1]
    top, bot = _edge_masked(prev_ref, next_ref)
    ext_ref[0:halo] = top
    ext_ref[halo:halo + tm] = cur_ref[0]
    ext_ref[halo + tm:] = bot
    c = GDN_C
    g0, grp = 0, tm
    ls = rs = [slice(ci * c, (ci + 1) * c) for ci in range(grp // c)]
    rows_g = slice(g0, g0 + grp)
    base = halo - GDN_CONV // 2
    for r0 in range(g0, g0 + grp, GDN_SUB):
        acc = jnp.zeros((GDN_SUB, 3 * GROUP_W), F32)
        for t in range(GDN_CONV):
            acc = acc + cw_ref[t:t + 1, :] * ext_ref[r0 + base + t:r0 + base + t + GDN_SUB, :]
        x = acc * jax.nn.sigmoid(acc)
        rows = slice(r0, r0 + GDN_SUB)
        q_scr[rows, :] = x[:, :GROUP_W]
        k_scr[rows, :] = x[:, GROUP_W:2 * GROUP_W]
        v_scr[rows, :] = x[:, 2 * GROUP_W:]
    qa, ka = q_scr[rows_g, :], k_scr[rows_g, :]
    q_scr[rows_g, :] = qa * lax.rsqrt(_dot01(qa * qa, ones_ref[...], 2) + EPS) * (GDN_HEAD_DIM ** -0.5)
    k_scr[rows_g, :] = ka * lax.rsqrt(_dot01(ka * ka, ones_ref[...], 2) + EPS)

    ab = ab_ref[0, rows_g, :]
    lane = lax.broadcasted_iota(jnp.int32, ab.shape, 1)
    g = -jnp.exp(alog_ref[...]) * _softplus(ab + dtb_ref[...])
    gate = jnp.where(lane < 2 * GDN_HEADS, g, jax.nn.sigmoid(ab))
    cums = [_dot01_left(tri_ref[...], gate[r], 2) for r in ls]
    lane_c = lax.broadcasted_iota(jnp.int32, (c, AB_PAD), 1)
    cum = jnp.concatenate([jnp.where(lane_c < GDN_HEADS, cm[:c], cm[c:]) for cm in cums], axis=0)
    wide = _dot01(jnp.where(lane < 2 * GDN_HEADS, cum, gate), exp_ref[...], 2)
    gcs = (wide[:, :GROUP_W], wide[:, GROUP_W:2 * GROUP_W])
    betas = (wide[:, 2 * GROUP_W:3 * GROUP_W], wide[:, 3 * GROUP_W:])

    ii = lax.broadcasted_iota(jnp.int32, (c, GROUP_W), 0)
    jj = lax.broadcasted_iota(jnp.int32, (c, GROUP_W), 1) & (c - 1)
    diag2 = jnp.concatenate([ii == jj] * 2, axis=1)
    rows_gc = [_dot01_left(same_ref[...], jnp.where(diag2, wide[r, :2 * GROUP_W], 0.0), 2) for r in ls]
    gams = ([jnp.where(ii >= jj, jnp.exp(jnp.minimum(gcs[0][r] - rg[:, :GROUP_W], 0.0)), 0.0)
             for r, rg in zip(ls, rows_gc)],
            [jnp.where(ii <= jj, jnp.exp(jnp.minimum(gcs[1][r] - rg[:, GROUP_W:], 0.0)), 0.0)
             for r, rg in zip(ls, rows_gc)])
    eye = (ii == jj).astype(F32)
    stricts = (ii > jj, ii < jj)
    levels = [((ii // (2 * s)) == (jj // (2 * s))) & ((ii // s) != (jj // s)) for s in (1, 2, 4, 8, 16, 32)]
    bdmask = _bd_mask()
    outs = ((pf_ref, qpf_ref, bmf_ref, o0f_ref, df_ref), (pb_ref, qpb_ref, bmb_ref, o0b_ref, db_ref))

    qs = [q_scr[r, :] for r in rs]
    ks = [k_scr[r, :] for r in rs]
    vs = [v_scr[r, :] for r in rs]
    kqs = [lax.dot_general(jnp.concatenate([k, q], axis=0).astype(BF16), _bd(k, bdmask),
                           (((1,), (1,)), ((), ())), preferred_element_type=F32) for k, q in zip(ks, qs)]
    pairs = [(di, ci) for di in range(2) for ci in range(len(rs))]
    gc_l = [gcs[di][ls[ci]] for di, ci in pairs]
    gam_l = [gams[di][ci] for di, ci in pairs]
    beta_l = [betas[di][ls[ci]] for di, ci in pairs]
    ms = [jnp.where(stricts[di], beta * kqs[ci][:c] * gam, 0.0)
          for (di, ci), beta, gam in zip(pairs, beta_l, gam_l)]
    ts = _tri_inverse(ms, eye, levels, bdmask)
    egcs = [jnp.exp(gc) for gc in gc_l]
    uws = [_dot(t, jnp.concatenate([_bd(vs[ci] * beta, bdmask), _bd(ks[ci] * beta * egc, bdmask)], axis=1))
           for (_, ci), t, beta, egc in zip(pairs, ts, beta_l, egcs)]
    aqks = [kqs[ci][c:] * gam for (_, ci), gam in zip(pairs, gam_l)]
    ows = [_dot(aqk, jnp.concatenate([_bd(uw[:, :GROUP_W], bdmask), _bd(uw[:, GROUP_W:], bdmask)], axis=1))
           for aqk, uw in zip(aqks, uws)]
    for n, (di, ci) in enumerate(pairs):
        p_ref, qp_ref, bm_ref, o0_ref, d_ref = outs[di]
        r, gc, uw, ow = rs[ci], gc_l[n], uws[n], ows[n]
        glast = gc[0:1] if di else gc[c - 1:c]
        kdec = ks[ci] * jnp.exp(glast - gc)
        bp = _diag_blocks(lax.dot_general(kdec.astype(BF16), uw.astype(BF16), (((0,), (0,)), ((), ())),
                                          preferred_element_type=F32))
        bm_ref[0, r, :] = bp[:, :GROUP_W]
        p_ref[0, r, :] = bp[:, GROUP_W:].astype(BF16)
        o0_ref[0, r, :] = ow[:, :GROUP_W]
        qp_ref[0, r, :] = (qs[ci] * egcs[n] - ow[:, GROUP_W:]).astype(BF16)
        d_ref[0, g0 // c + ci:g0 // c + ci + 1, :] = jnp.exp(glast)


def _gdn_prep(qkv, ab, conv_w, a_log, dt_bias, tm):
    b, t, w3 = qkv.shape
    halo = 8
    nck = tm // GDN_C
    prev, cur, nxt = _halo_specs(tm, halo, t, w3)
    pad = lambda a: jnp.pad(a.reshape(1, -1), ((0, 0), (0, AB_PAD - a.size)))
    consts = _gdn_consts()
    tok = pl.BlockSpec((1, tm, GROUP_W), lambda i, j: (i, j, 0))
    dec = pl.BlockSpec((1, nck, GROUP_W), lambda i, j: (i, j, 0))
    tok_shape = lambda dt: jax.ShapeDtypeStruct((b, t, GROUP_W), dt)
    dir_specs = [tok, tok, tok, tok, dec]
    dir_shapes = [tok_shape(BF16), tok_shape(BF16), tok_shape(F32), tok_shape(F32),
                  jax.ShapeDtypeStruct((b, t // GDN_C, GROUP_W), F32)]
    return pl.pallas_call(
        _gdn_prep_kernel,
        grid=(b, t // tm),
        in_specs=[prev, cur, nxt, pl.BlockSpec((1, tm, AB_PAD), lambda i, j: (i, j, 0)),
                  _resident(conv_w.shape), _resident((1, AB_PAD)), _resident((1, AB_PAD))]
        + [_resident(cst.shape) for cst in consts],
        out_specs=dir_specs * 2,
        out_shape=dir_shapes * 2,
        scratch_shapes=[pltpu.VMEM((tm + 2 * halo, w3), F32)] + [pltpu.VMEM((tm, GROUP_W), F32)] * 3,
        compiler_params=_params(2), name="gdn_prep",
    )(qkv, qkv, qkv, ab, conv_w, pad(a_log), pad(dt_bias), *consts)


def _gdn_scan_kernel(pf_ref, qpf_ref, bmf_ref, o0f_ref, df_ref, pb_ref, qpb_ref, bmb_ref, o0b_ref, db_ref,
                     s0f_ref, s0b_ref, of_ref, ob_ref, sf_ref, sb_ref):
    nb, tm = pf_ref.shape[0], pf_ref.shape[1]
    nc = tm // GDN_C

    @pl.when(pl.program_id(0) == 0)
    def _():
        sf_ref[...] = s0f_ref[...]
        sb_ref[...] = s0b_ref[...]

    bdmask = _bd_mask()
    fwd = (pf_ref, qpf_ref, bmf_ref, o0f_ref, df_ref, of_ref)
    bwd = (pb_ref, qpb_ref, bmb_ref, o0b_ref, db_ref, ob_ref)
    chains = [(fwd, bi, False) for bi in range(nb)] + [(bwd, bi, True) for bi in range(nb)]
    states = [sf_ref[bi] for bi in range(nb)] + [sb_ref[bi] for bi in range(nb)]
    for step in range(nc):
        lhs, rows = [], []
        for (p_ref, qp_ref, _, _, _, _), bi, rev in chains:
            ci = nc - 1 - step if rev else step
            r = slice(ci * GDN_C, (ci + 1) * GDN_C)
            rows.append((ci, r))
            lhs.append(jnp.concatenate([p_ref[bi, r, :], qp_ref[bi, r, :]], axis=0))
        res = [jnp.dot(a, _bd(s, bdmask), preferred_element_type=F32) for a, s in zip(lhs, states)]
        new_states = []
        for (_, _, bm_ref, o0_ref, d_ref, o_ref), bi, _ in chains:
            n = len(new_states)
            ci, r = rows[n]
            o_ref[bi, r, :] = o0_ref[bi, r, :] + res[n][GDN_C:]
            new_states.append(d_ref[bi, ci:ci + 1, :] * states[n] - res[n][:GDN_C] + bm_ref[bi, r, :])
        states = new_states
    for bi in range(nb):
        sf_ref[bi] = states[bi]
        sb_ref[bi] = states[nb + bi]


def _gdn_scan(prep, s0f, s0b, tm):
    b, t, w = prep[0].shape
    nj = t // tm
    nck = tm // GDN_C
    fwd = pl.BlockSpec((b, tm, w), lambda j: (0, j, 0))
    bwd = pl.BlockSpec((b, tm, w), lambda j: (0, nj - 1 - j, 0))
    dfwd = pl.BlockSpec((b, nck, w), lambda j: (0, j, 0))
    dbwd = pl.BlockSpec((b, nck, w), lambda j: (0, nj - 1 - j, 0))
    st = pl.BlockSpec((b, GDN_HEAD_DIM, w), lambda j: (0, 0, 0))
    st_shape = jax.ShapeDtypeStruct((b, GDN_HEAD_DIM, w), F32)
    return pl.pallas_call(
        _gdn_scan_kernel,
        grid=(nj,),
        in_specs=[fwd] * 4 + [dfwd] + [bwd] * 4 + [dbwd] + [st, st],
        out_specs=[fwd, bwd, st, st],
        out_shape=[jax.ShapeDtypeStruct((b, t, w), F32)] * 2 + [st_shape] * 2,
        compiler_params=pltpu.CompilerParams(dimension_semantics=("arbitrary",),
                                             vmem_limit_bytes=VMEM_LIMIT),
        name="gdn_scan",
    )(*prep, s0f, s0b)


def _mix_out_kernel(h_ref, yp_ref, yfa_ref, yfb_ref, yc_ref, of_ref, ob_ref, z_ref, gt_ref, nw_ref, ones_ref,
                    w_ref, o_ref):
    o = of_ref[0] + ob_ref[0]
    ms = _dot01(o * o, ones_ref[...], 2) * (1.0 / GDN_HEAD_DIM)
    z = z_ref[0]
    yg = o * lax.rsqrt(ms + EPS) * nw_ref[...] * (z * jax.nn.sigmoid(z))
    y = jnp.zeros(h_ref.shape[1:], F32)
    yf = jnp.concatenate([yfa_ref[0], yfb_ref[0]], axis=-1)
    for gi, part in enumerate((yp_ref[0], yf, yc_ref[0], yg)):
        y = y + jnp.dot(part.astype(BF16), w_ref[gi * GROUP_W:(gi + 1) * GROUP_W, :],
                        preferred_element_type=F32)
    o_ref[0] = h_ref[0] + gt_ref[0] * y


def _mix_out(h, ys, o_f, o_b, z, gate, gdn_nw, w_out, layer, tm):
    b, t, d = h.shape
    tok = pl.BlockSpec((1, tm, d), lambda i, j: (i, j, 0))
    grp = pl.BlockSpec((1, tm, GROUP_W), lambda i, j: (i, j, 0))
    half = pl.BlockSpec((1, tm, LANES), lambda i, j: (i, j, 0))
    ones = _gdn_consts()[3]
    return pl.pallas_call(
        _mix_out_kernel,
        grid=(b, t // tm),
        in_specs=[tok, grp, half, half] + [grp] * 4 + [pl.BlockSpec((1, 1, d), lambda i, j: (i, 0, 0)),
                                     _resident((1, GROUP_W)), _resident(ones.shape), _layer_resident(w_out, layer)],
        out_specs=tok,
        out_shape=jax.ShapeDtypeStruct(h.shape, F32),
        compiler_params=_params(2), name="mix_out",
    )(h, *ys, o_f, o_b, z, gate, gdn_nw, ones, w_out)


MOD_ROWS = 8
MOD_TN = 1152


def _mod_kernel(c_ref, w_ref, b_ref, o_ref):
    cv = c_ref[...]
    a = (cv * jax.nn.sigmoid(cv)).astype(BF16)
    o_ref[0] = jnp.dot(a, w_ref[0].astype(BF16), preferred_element_type=F32) + b_ref[0]


def _modulation(c, c_ctx, mod_w, mod_b):
    nl, d, n = mod_w.shape
    cond = jnp.concatenate([c, c_ctx[None, :]], axis=0)
    cond = jnp.pad(cond, ((0, MOD_ROWS - cond.shape[0]), (0, 0)))
    return pl.pallas_call(
        _mod_kernel,
        grid=(nl, n // MOD_TN),
        in_specs=[_resident(cond.shape), pl.BlockSpec((1, d, MOD_TN), lambda i, j: (i, 0, j)),
                  pl.BlockSpec((1, 1, MOD_TN), lambda i, j: (i, 0, j))],
        out_specs=pl.BlockSpec((1, MOD_ROWS, MOD_TN), lambda i, j: (i, 0, j)),
        out_shape=jax.ShapeDtypeStruct((nl, MOD_ROWS, n), F32),
        compiler_params=_params(2), name="modulation",
    )(cond, mod_w, mod_b[:, None, :])


def _token_mix(p, gdn_state, grid_rows, tm, wl, need_out):
    p_pool, p_four, p_conv, p_qkv, p_z, p_ab = p
    prep = _gdn_prep(p_qkv, p_ab, wl["gdn_conv_w"], wl["gdn_a_log"], wl["gdn_dt_bias"], tm)
    o_f, o_b, s_f, s_b = _gdn_scan(prep, *gdn_state, tm)
    if not need_out:
        return None, (s_f, s_b)
    ys = (_pool_mix(p_pool, wl["pool_wbd"], wl["pool_scale"], grid_rows),
          *_fourier_mix(p_four, wl["fourier_w"]),
          _conv_mix(p_conv, wl["conv_dw_w"], wl["conv_dw_b"], wl["conv_ln_g"], wl["conv_ln_b"],
                    wl["conv_pw_w"], tm))
    return (ys, o_f, o_b, p_z), (s_f, s_b)


def kernel(x, c, ctx, c_ctx, mod_w, mod_b, norm_w, ffn1_wg, ffn1_wu, ffn1_wd, ffn2_wg, ffn2_wu,
           ffn2_wd, w_in, w_out, pool_w, pool_scale, fourier_w, conv_dw_w, conv_dw_b, conv_ln_g,
           conv_ln_b, conv_pw_w, gdn_conv_w, gdn_a_log, gdn_dt_bias, gdn_norm_w, final_norm_w):
    bsz, seq, d = x.shape
    n_ctx = ctx.shape[1]
    rows = seq // GRID_W
    tm_x, tm_c = 512, n_ctx
    hx, hc = x, ctx
    mods = _modulation(c, c_ctx, mod_w, mod_b)
    zero_state = (jnp.zeros((bsz, GDN_HEAD_DIM, GROUP_W), F32),) * 2
    f1 = tuple(_cast_bf16(w) for w in (ffn1_wg, ffn1_wu, ffn1_wd))
    f2 = tuple(_cast_bf16(w) for w in (ffn2_wg, ffn2_wu, ffn2_wd))
    w_main = _cast_bf16(w_in, cols=2 * GDN_OFF)
    w_ab = jnp.pad(w_in[:, :, 2 * GDN_OFF:], ((0, 0), (0, 0), (0, AB_PAD - N_GATE))).astype(BF16)
    w_out_b = _cast_bf16(w_out)
    for l in range(DEPTH):
        last = l == DEPTH - 1
        mx = [m[:, None, :] for m in jnp.split(mods[l, :bsz], N_MOD, axis=-1)]
        mc = [jnp.broadcast_to(m[None], (bsz, 1, d)) for m in jnp.split(mods[l, bsz:bsz + 1], N_MOD, axis=-1)]
        nw = norm_w[l][:, None, :]
        row = lambda a: a.reshape(1, -1)
        wl = dict(pool_wbd=_blockdiag(pool_w[l]).astype(BF16), pool_scale=row(pool_scale[l]),
                  fourier_w=fourier_w[l].astype(BF16), conv_dw_w=conv_dw_w[l], conv_dw_b=row(conv_dw_b[l]),
                  conv_ln_g=row(conv_ln_g[l]), conv_ln_b=row(conv_ln_b[l]),
                  conv_pw_w=conv_pw_w[l].astype(BF16), gdn_conv_w=gdn_conv_w[l],
                  gdn_a_log=gdn_a_log[l], gdn_dt_bias=gdn_dt_bias[l])
        gdn_nw = row(jnp.tile(gdn_norm_w[l], GDN_HEADS))

        hx = _ffn(hx, nw[0], mx[0], mx[1], mx[2], f1, l, tm_x)
        hc = _ffn(hc, nw[0], mc[0], mc[1], mc[2], f1, l, tm_c)

        px = _inproj(hx, nw[1], mx[3], mx[4], w_main, w_ab, l, False, tm_x)
        pc = _inproj(hc, nw[1], mc[3], mc[4], w_main, w_ab, l, last, tm_c)
        if last:
            pc = (None,) * 3 + tuple(pc)

        mix_c, ctx_state = _token_mix(pc, zero_state, None, tm_c, wl, not last)
        mix_x, _ = _token_mix(px, ctx_state, rows, tm_x, wl, True)
        hx = _mix_out(hx, *mix_x, mx[5], gdn_nw, w_out_b, l, tm_x)
        hx = _ffn(hx, nw[2], mx[6], mx[7], mx[8], f2, l, tm_x, final_norm_w[None, :] if last else None)
        if not last:
            hc = _mix_out(hc, *mix_c, mc[5], gdn_nw, w_out_b, l, tm_c)
            hc = _ffn(hc, nw[2], mc[6], mc[7], mc[8], f2, l, tm_c)
    return hx
```

```python
import functools
import math

import jax
import jax.numpy as jnp
import numpy as np
from jax import lax
from jax.experimental import pallas as pl
from jax.experimental.pallas import tpu as pltpu

D_MODEL = 1024
DEPTH = 4
GRID_W = 64
N_MIXERS = 4
GROUP_W = D_MODEL // N_MIXERS
POOL_WINDOWS = (2, 4, 8, 16)
POOL_GROUPS = 4
POOL_GW = GROUP_W // POOL_GROUPS
FOURIER_GROUPS = 4
FOURIER_GW = GROUP_W // FOURIER_GROUPS
CONV_K = 31
GDN_HEAD_DIM = 64
GDN_HEADS = GROUP_W // GDN_HEAD_DIM
GDN_CONV = 3
GDN_CHUNK = 64
FFN_HIDDEN = 128 * ((8 * D_MODEL // 3 + 127) // 128)
N_MOD = 9
EPS = 1e-6
POOL_OFF = 0
FOURIER_OFF = POOL_OFF + GROUP_W
CONV_OFF = FOURIER_OFF + GROUP_W
GDN_OFF = CONV_OFF + 2 * GROUP_W

LANES = 128
SUBLANES = 8
VMEM_LIMIT = 56 * 1024 * 1024
FFN_CHUNK = 256
AB_PAD = LANES

BF16 = jnp.bfloat16
F32 = jnp.float32


def _params(n_axes):
    return pltpu.CompilerParams(dimension_semantics=("parallel",) * n_axes,
                                vmem_limit_bytes=VMEM_LIMIT)


def _resident(shape):
    nd = len(shape)
    return pl.BlockSpec(shape, lambda *_: (0,) * nd, pipeline_mode=pl.Buffered(1))


def _layer_resident(stack, layer, cols=None, col_block=0):
    _, r, c = stack.shape
    return pl.BlockSpec((None, r, cols or c), lambda *_: (layer, 0, col_block), pipeline_mode=pl.Buffered(1))


def _cast_kernel(x_ref, o_ref):
    o_ref[...] = x_ref[...].astype(BF16)


def _cast_bf16(w, cols=None, rows_per_step=256):
    nl, r, c = w.shape
    cols = cols or c
    spec = pl.BlockSpec((1, rows_per_step, cols), lambda i, j: (i, j, 0))
    return pl.pallas_call(
        _cast_kernel, grid=(nl, r // rows_per_step), in_specs=[spec], out_specs=spec,
        out_shape=jax.ShapeDtypeStruct((nl, r, cols), BF16), compiler_params=_params(2), name="cast_bf16",
    )(w)


def _rms_mod(h, nw, shift, scale):
    ms = jnp.mean(h * h, axis=-1, keepdims=True)
    n = h * lax.rsqrt(ms + EPS) * nw
    return n * (1.0 + scale) + shift


def _ffn_apply(h, nw_ref, sh_ref, sc_ref, gt_ref, wg_ref, wu_ref, wd_ref, final_nw_ref=None):
    nb = _rms_mod(h, nw_ref[...], sh_ref[0], sc_ref[0]).astype(BF16)
    acc = jnp.zeros(h.shape, F32)
    for f0 in range(0, FFN_HIDDEN, FFN_CHUNK):
        g = jnp.dot(nb, wg_ref[:, f0:f0 + FFN_CHUNK], preferred_element_type=F32)
        u = jnp.dot(nb, wu_ref[:, f0:f0 + FFN_CHUNK], preferred_element_type=F32)
        a = (g * jax.nn.sigmoid(g) * u).astype(BF16)
        acc = acc + jnp.dot(a, wd_ref[f0:f0 + FFN_CHUNK, :], preferred_element_type=F32)
    out = h + (0.5 * gt_ref[0]) * acc
    if final_nw_ref is not None:
        out = out * lax.rsqrt(jnp.mean(out * out, axis=-1, keepdims=True) + EPS) * final_nw_ref[...]
    return out


def _ffn_kernel(h_ref, *rest):
    rest[-1][0] = _ffn_apply(h_ref[0], *rest[:-1])


def _ffn(h, nw, shift, scale, gate, weights, layer, tm, final_nw=None):
    b, t, d = h.shape
    vec = pl.BlockSpec((1, 1, d), lambda i, j: (i, 0, 0))
    tok = pl.BlockSpec((1, tm, d), lambda i, j: (i, j, 0))
    extra = [] if final_nw is None else [final_nw]
    return pl.pallas_call(
        _ffn_kernel,
        grid=(b, t // tm),
        in_specs=[tok, _resident((1, d)), vec, vec, vec] + [_layer_resident(w, layer) for w in weights]
        + [_resident((1, d))] * len(extra),
        out_specs=tok,
        out_shape=jax.ShapeDtypeStruct(h.shape, F32),
        compiler_params=_params(2),
        name="ffn",
    )(h, nw, shift, scale, gate, *weights, *extra)


IN_SPLITS = (GROUP_W, GROUP_W, 2 * GROUP_W, 3 * GROUP_W, GROUP_W, AB_PAD)
IN_DTYPES = (BF16, F32, BF16, BF16, BF16, F32)


def _inproj_kernel(h_ref, nw_ref, sh_ref, sc_ref, w_ref, wab_ref, *o_refs):
    tm = h_ref.shape[1]
    half = tm // 2 if tm % (2 * SUBLANES) == 0 else tm
    for r0 in range(0, tm, half):
        rows = slice(r0, r0 + half)
        nb = _rms_mod(h_ref[0, rows, :], nw_ref[...], sh_ref[0], sc_ref[0]).astype(BF16)
        off = 0
        for o_ref in o_refs[:-1]:
            wdt = o_ref.shape[-1]
            o_ref[0, rows, :] = jnp.dot(nb, w_ref[:, off:off + wdt],
                                        preferred_element_type=F32).astype(o_ref.dtype)
            off += wdt
        o_refs[-1][0, rows, :] = jnp.dot(nb, wab_ref[...], preferred_element_type=F32)


def _inproj(h, nw, shift, scale, w_main, w_ab, layer, gdn_only, tm):
    b, t, d = h.shape
    vec = pl.BlockSpec((1, 1, d), lambda i, j: (i, 0, 0))
    tok = pl.BlockSpec((1, tm, d), lambda i, j: (i, j, 0))
    splits = IN_SPLITS[3:] if gdn_only else IN_SPLITS
    dtypes = IN_DTYPES[3:] if gdn_only else IN_DTYPES
    w_spec = (_layer_resident(w_main, layer, GDN_OFF, 1) if gdn_only else _layer_resident(w_main, layer))
    return pl.pallas_call(
        _inproj_kernel,
        grid=(b, t // tm),
        in_specs=[tok, _resident((1, d)), vec, vec, w_spec, _layer_resident(w_ab, layer)],
        out_specs=[pl.BlockSpec((1, tm, s), lambda i, j: (i, j, 0)) for s in splits],
        out_shape=[jax.ShapeDtypeStruct((b, t, s), dt) for s, dt in zip(splits, dtypes)],
        compiler_params=_params(2),
        name="inproj",
    )(h, nw, shift, scale, w_main, w_ab)


def _split_bf16(x, parts):
    out = []
    for _ in range(parts - 1):
        p = x.astype(BF16)
        out.append(p)
        x = x - p.astype(F32)
    return out + [x.astype(BF16)]


def _dot01(x, w01, parts=3):
    return sum(jnp.dot(p, w01, preferred_element_type=F32) for p in _split_bf16(x, parts))


def _dot01_left(w01, x, parts=3):
    return sum(jnp.dot(w01, p, preferred_element_type=F32) for p in _split_bf16(x, parts))


def _blockdiag(w):
    g, a, b = w.shape
    return jnp.einsum('gab,gh->gahb', w, jnp.eye(g, dtype=w.dtype)).reshape(g * a, g * b)


def _halo_specs(tm, halo, t, width):
    r = tm // halo
    last = t // halo - 1
    prev = pl.BlockSpec((1, halo, width), lambda i, j: (i, jnp.maximum(j * r - 1, 0), 0))
    cur = pl.BlockSpec((1, tm, width), lambda i, j: (i, j, 0))
    nxt = pl.BlockSpec((1, halo, width), lambda i, j: (i, jnp.minimum((j + 1) * r, last), 0))
    return prev, cur, nxt


def _edge_masked(prev_ref, next_ref):
    j, nj = pl.program_id(1), pl.num_programs(1)
    top = jnp.where(j > 0, prev_ref[0].astype(F32), 0.0)
    bot = jnp.where(j < nj - 1, next_ref[0].astype(F32), 0.0)
    return top, bot


POOL_TM = 1024
POOL_HALO = 512


def _pool_window_sums_1d(u, row_len):
    n = u.shape[0]
    col = lax.broadcasted_iota(jnp.int32, u.shape, 0) & (row_len - 1)
    grp = lax.broadcasted_iota(jnp.int32, u.shape, 1) // POOL_GW

    def back(x, s):
        return jnp.where(col >= s, pltpu.roll(x, s, 0), 0.0)

    def fwd(x, s):
        return jnp.where(col < row_len - s, pltpu.roll(x, n - s, 0), 0.0)

    b = back(u, 1)
    f = u
    out = b + f
    for gi in range(1, len(POOL_WINDOWS)):
        s = POOL_WINDOWS[gi] // 4
        b = b + back(b, s)
        f = f + fwd(f, s)
        out = jnp.where(grp >= gi, b + f, out)
    return out


def _pool_counts(idx, extent, halfw):
    return jnp.minimum(idx + halfw, extent) - jnp.maximum(idx - halfw, 0)


def _pool_finish(mean, u, wbd_ref, scale_ref, o_ref):
    d = (mean - u).astype(BF16)
    o_ref[0] = (jnp.dot(d, wbd_ref[...], preferred_element_type=F32) * scale_ref[...]).astype(BF16)


def _pool_grid_kernel(prev_ref, cur_ref, next_ref, wbd_ref, scale_ref, o_ref, *, n_rows):
    tm = cur_ref.shape[1]
    top, bot = _edge_masked(prev_ref, next_ref)
    u = cur_ref[0].astype(F32)
    grp = lax.broadcasted_iota(jnp.int32, (tm, GROUP_W), 1) // POOL_GW
    arr, off = jnp.concatenate([top, u, bot], axis=0), 0
    tot = None
    for gi, w in enumerate(POOL_WINDOWS):
        sh = GRID_W * max(w // 4, 1) if gi else GRID_W
        if gi == 0:
            arr = arr[:-sh] + arr[sh:]
            off = sh
        else:
            arr = arr[:-2 * sh] + arr[2 * sh:]
            off = off + sh
        centre = arr[POOL_HALO - off:POOL_HALO - off + tm]
        tot = centre if tot is None else jnp.where(grp >= gi, centre, tot)
    tot = _pool_window_sums_1d(tot, GRID_W)
    tok = pl.program_id(1) * tm + lax.broadcasted_iota(jnp.int32, (tm, GROUP_W), 0)
    halfw = jnp.left_shift(1, grp)
    cnt = (_pool_counts(tok // GRID_W, n_rows, halfw) * _pool_counts(tok & (GRID_W - 1), GRID_W, halfw))
    _pool_finish(tot / cnt.astype(F32), u, wbd_ref, scale_ref, o_ref)


def _pool_seq_kernel(u_ref, wbd_ref, scale_ref, o_ref):
    u = u_ref[0].astype(F32)
    n = u.shape[0]
    tot = _pool_window_sums_1d(u, n)
    grp = lax.broadcasted_iota(jnp.int32, u.shape, 1) // POOL_GW
    tok = lax.broadcasted_iota(jnp.int32, u.shape, 0)
    cnt = _pool_counts(tok, n, jnp.left_shift(1, grp))
    _pool_finish(tot / cnt.astype(F32), u, wbd_ref, scale_ref, o_ref)


def _pool_mix(u, wbd, scale, grid_rows):
    b, t, w = u.shape
    out_shape = jax.ShapeDtypeStruct(u.shape, BF16)
    if grid_rows is None:
        tok = pl.BlockSpec((1, t, w), lambda i: (i, 0, 0))
        return pl.pallas_call(
            _pool_seq_kernel, grid=(b,),
            in_specs=[tok, _resident(wbd.shape), _resident(scale.shape)],
            out_specs=tok, out_shape=out_shape, compiler_params=_params(1), name="pool_seq",
        )(u, wbd, scale)
    prev, cur, nxt = _halo_specs(POOL_TM, POOL_HALO, t, w)
    return pl.pallas_call(
        functools.partial(_pool_grid_kernel, n_rows=grid_rows),
        grid=(b, t // POOL_TM),
        in_specs=[prev, cur, nxt, _resident(wbd.shape), _resident(scale.shape)],
        out_specs=cur, out_shape=out_shape, compiler_params=_params(2), name="pool_grid",
    )(u, u, u, wbd, scale)


CONV_TM = 512
CONV_HALO = 16
CONV_SUB = 64


def _glu(x):
    return x[:, :GROUP_W] * jax.nn.sigmoid(x[:, GROUP_W:])


def _conv_kernel(prev_ref, cur_ref, next_ref, dww_ref, dwb_ref, lng_ref, lnb_ref, pw_ref, o_ref,
                 ext_ref, sh_ref):
    tm = cur_ref.shape[1]
    top, bot = _edge_masked(prev_ref, next_ref)
    ext_ref[0:CONV_HALO] = _glu(top)
    ext_ref[CONV_HALO:CONV_HALO + tm] = _glu(cur_ref[0].astype(F32))
    ext_ref[CONV_HALO + tm:] = _glu(bot)
    n_sh = sh_ref.shape[1]
    for s in range(SUBLANES):
        sh_ref[s] = ext_ref[s:s + n_sh, :]
    base = CONV_HALO - CONV_K // 2
    for r0 in range(0, tm, CONV_SUB):
        acc = jnp.zeros((CONV_SUB, GROUP_W), F32)
        for k in range(CONV_K):
            a, s = divmod(base + k, SUBLANES)
            acc = acc + dww_ref[k:k + 1, :] * sh_ref[s, r0 + a * SUBLANES:r0 + a * SUBLANES + CONV_SUB, :]
        h = acc + dwb_ref[...]
        mu = jnp.mean(h, axis=-1, keepdims=True)
        var = jnp.mean(jnp.square(h - mu), axis=-1, keepdims=True)
        h = (h - mu) * lax.rsqrt(var + EPS) * lng_ref[...] + lnb_ref[...]
        h = (h * jax.nn.sigmoid(h)).astype(BF16)
        o_ref[0, r0:r0 + CONV_SUB, :] = jnp.dot(h, pw_ref[...], preferred_element_type=F32).astype(BF16)


def _conv_mix(u2, dw_w, dw_b, ln_g, ln_b, pw_w, tm):
    b, t, w2 = u2.shape
    prev, cur, nxt = _halo_specs(tm, CONV_HALO, t, w2)
    row = _resident((1, GROUP_W))
    return pl.pallas_call(
        _conv_kernel,
        grid=(b, t // tm),
        in_specs=[prev, cur, nxt, _resident(dw_w.shape), row, row, row, _resident(pw_w.shape)],
        out_specs=pl.BlockSpec((1, tm, GROUP_W), lambda i, j: (i, j, 0)),
        out_shape=jax.ShapeDtypeStruct((b, t, GROUP_W), BF16),
        scratch_shapes=[pltpu.VMEM((tm + 2 * CONV_HALO, GROUP_W), F32),
                        pltpu.VMEM((SUBLANES, tm + 2 * CONV_HALO - SUBLANES, GROUP_W), F32)],
        compiler_params=_params(2), name="conv_mix",
    )(u2, u2, u2, dw_w, dw_b, ln_g, ln_b, pw_w)


FOUR_N1 = 128
FOUR_TT = 8
FOUR_KT = 8


def _hilo(w):
    w = jnp.asarray(w, F32)
    hi = w.astype(BF16)
    return hi, (w - hi.astype(F32)).astype(BF16)


def _dot_hp(x, wh, wl):
    xh = x.astype(BF16)
    xl = (x - xh.astype(F32)).astype(BF16)
    return (jnp.dot(xh, wh, preferred_element_type=F32) + jnp.dot(xh, wl, preferred_element_type=F32)
            + jnp.dot(xl, wh, preferred_element_type=F32))


def _dot_hp_left(wh, wl, x):
    xh = x.astype(BF16)
    xl = (x - xh.astype(F32)).astype(BF16)
    return (jnp.dot(wh, xh, preferred_element_type=F32) + jnp.dot(wl, xh, preferred_element_type=F32)
            + jnp.dot(wh, xl, preferred_element_type=F32))


def _cos_sin(n, rows=None, cols=None):
    r = np.arange(n if rows is None else rows, dtype=np.int64)
    c = np.arange(n if cols is None else cols, dtype=np.int64)
    ang = 2.0 * np.pi * ((np.outer(r, c) % n).astype(np.float64) / n)
    return np.cos(ang), np.sin(ang)


def _channel_dft():
    c, s = _cos_sin(FOURIER_GW)
    eye = np.eye(FOURIER_GROUPS)
    return np.concatenate([np.kron(eye, c), -np.kron(eye, s)], axis=1)


def _fourier_a_kernel(ua_ref, ub_ref, cdh_ref, cdl_ref, f1h_ref, f1l_ref, twc_ref, tws_ref, o_ref):
    for j in range(FOUR_TT):
        x = jnp.concatenate([ua_ref[:, j, :], ub_ref[:, j, :]], axis=1)
        z = _dot_hp(x, cdh_ref[...], cdl_ref[...])
        zz = jnp.concatenate([z[:, :GROUP_W], z[:, GROUP_W:]], axis=0)
        a = _dot_hp_left(f1h_ref[...], f1l_ref[...], zz)
        ar, ai = a[:FOUR_N1], a[FOUR_N1:]
        c = jnp.concatenate([twc_ref[j]] * (GROUP_W // LANES), axis=1)
        s = jnp.concatenate([tws_ref[j]] * (GROUP_W // LANES), axis=1)
        o_ref[0, 0, j] = ar * c + ai * s
        o_ref[0, 1, j] = ai * c - ar * s


def _fourier_b_kernel(ga_ref, gb_ref, f2h_ref, f2l_ref, fw_ref, oa_ref, ob_ref, *, scale):
    for kk in range(FOUR_KT):
        gm = jnp.concatenate([ga_ref[:, :, kk, :], gb_ref[:, :, kk, :]], axis=-1)
        gm = gm.reshape(2 * FOUR_N1, GROUP_W)
        f = _dot_hp_left(f2h_ref[...], f2l_ref[...], gm) * scale
        y = jnp.dot(f.astype(BF16), fw_ref[...], preferred_element_type=F32)
        oa_ref[:, kk, :] = y[:, :LANES]
        ob_ref[:, kk, :] = y[:, LANES:]


def _fourier_small_kernel(u_ref, cdh_ref, cdl_ref, fh_ref, fl_ref, fw_ref, oa_ref, ob_ref, *, scale):
    z = _dot_hp(u_ref[0], cdh_ref[...], cdl_ref[...])
    zz = jnp.concatenate([z[:, :GROUP_W], z[:, GROUP_W:]], axis=0)
    f = _dot_hp_left(fh_ref[...], fl_ref[...], zz) * scale
    y = jnp.dot(f.astype(BF16), fw_ref[...], preferred_element_type=F32)
    oa_ref[0] = y[:, :LANES]
    ob_ref[0] = y[:, LANES:]


def _fourier_mix(u, fw):
    b, n, w = u.shape
    scale = 1.0 / math.sqrt(n * FOURIER_GW)
    cdh, cdl = _hilo(_channel_dft())
    if n != FOUR_N1 * FOUR_N1:
        c, s = _cos_sin(n)
        fh, fl = _hilo(np.concatenate([c, s], axis=1))
        tok = pl.BlockSpec((1, n, w), lambda i: (i, 0, 0))
        half = pl.BlockSpec((1, n, LANES), lambda i: (i, 0, 0))
        return pl.pallas_call(
            functools.partial(_fourier_small_kernel, scale=scale), grid=(b,),
            in_specs=[tok] + [_resident(a.shape) for a in (cdh, cdl, fh, fl, fw)],
            out_specs=[half, half], out_shape=[jax.ShapeDtypeStruct((b, n, LANES), F32)] * 2,
            compiler_params=_params(1), name="fourier_small",
        )(u, cdh, cdl, fh, fl, fw)
    n1 = FOUR_N1
    c1, s1 = _cos_sin(n1)
    f1h, f1l = _hilo(np.block([[c1, s1], [-s1, c1]]))
    f2h, f2l = _hilo(np.concatenate([c1, s1], axis=1))
    twc, tws = _cos_sin(n, rows=n1, cols=n1)
    twc = jnp.broadcast_to(jnp.asarray(twc, F32)[:, :, None], (n1, n1, LANES))
    tws = jnp.broadcast_to(jnp.asarray(tws, F32)[:, :, None], (n1, n1, LANES))
    tw_spec = pl.BlockSpec((FOUR_TT, n1, LANES), lambda i, j: (j, 0, 0))
    u4 = u.reshape(b, n1, n1, w)
    g = pl.pallas_call(
        _fourier_a_kernel, grid=(b, n1 // FOUR_TT),
        in_specs=[pl.BlockSpec((None, n1, FOUR_TT, LANES), lambda i, j: (i, 0, j, 0)),
                  pl.BlockSpec((None, n1, FOUR_TT, LANES), lambda i, j: (i, 0, j, 1))]
        + [_resident(a.shape) for a in (cdh, cdl, f1h, f1l)] + [tw_spec, tw_spec],
        out_specs=pl.BlockSpec((1, 2, FOUR_TT, n1, w), lambda i, j: (i, 0, j, 0, 0)),
        out_shape=jax.ShapeDtypeStruct((b, 2, n1, n1, w), F32),
        compiler_params=_params(2), name="fourier_a",
    )(u4, u4, cdh, cdl, f1h, f1l, twc, tws)
    half_out = pl.BlockSpec((None, n1, FOUR_KT, LANES), lambda i, j: (i, 0, j, 0))
    ya, yb = pl.pallas_call(
        functools.partial(_fourier_b_kernel, scale=scale), grid=(b, n1 // FOUR_KT),
        in_specs=[pl.BlockSpec((None, 2, n1, FOUR_KT, LANES), lambda i, j: (i, 0, 0, j, 0)),
                  pl.BlockSpec((None, 2, n1, FOUR_KT, LANES), lambda i, j: (i, 0, 0, j, 1))]
        + [_resident(a.shape) for a in (f2h, f2l, fw)],
        out_specs=[half_out, half_out],
        out_shape=[jax.ShapeDtypeStruct((b, n1, n1, LANES), F32)] * 2,
        compiler_params=_params(2), name="fourier_b",
    )(g, g, f2h, f2l, fw)
    return ya.reshape(b, n, LANES), yb.reshape(b, n, LANES)


GDN_C = GDN_CHUNK
GDN_PREP_TM = 512
GDN_SCAN_TM = 512
GDN_SUB = 64
N_GATE = 4 * GDN_HEADS


def _softplus(x):
    return jnp.maximum(x, 0.0) + jnp.log1p(jnp.exp(-jnp.abs(x)))


def _gdn_consts():
    i = np.arange(GDN_C)
    lower = np.concatenate([i[None, :] <= i[:, None], i[None, :] >= i[:, None]], axis=0)
    same = np.ones((GDN_C, GDN_C), bool)
    expand = np.zeros((AB_PAD, 4 * GROUP_W), np.float32)
    for s in range(4):
        for h in range(GDN_HEADS):
            c0 = s * GROUP_W + h * GDN_HEAD_DIM
            expand[s * GDN_HEADS + h, c0:c0 + GDN_HEAD_DIM] = 1.0
    lane = np.arange(GROUP_W)
    headones = (lane[:, None] // GDN_HEAD_DIM) == (lane[None, :] // GDN_HEAD_DIM)
    as_bf = lambda a: jnp.asarray(a, F32).astype(BF16)
    return as_bf(lower), as_bf(same), as_bf(expand), as_bf(headones)


def _bd(x, bdmask):
    xb = x.astype(BF16)
    return jnp.where(bdmask, jnp.concatenate([xb] * GDN_HEADS, axis=0), jnp.zeros((), BF16))


def _bd_mask():
    return (lax.broadcasted_iota(jnp.int32, (GROUP_W, GROUP_W), 0) // GDN_HEAD_DIM
            == lax.broadcasted_iota(jnp.int32, (GROUP_W, GROUP_W), 1) // GDN_HEAD_DIM)


def _dot(a, b):
    return jnp.dot(a.astype(BF16), b.astype(BF16), preferred_element_type=F32)


def _diag_blocks(full):
    head = (lax.broadcasted_iota(jnp.int32, (GDN_HEAD_DIM, full.shape[1]), 1) // GDN_HEAD_DIM) % GDN_HEADS
    out = None
    for h in range(GDN_HEADS):
        blk = jnp.where(head == h, full[h * GDN_HEAD_DIM:(h + 1) * GDN_HEAD_DIM], 0.0)
        out = blk if out is None else out + blk
    return out


def _tri_inverse(ms, eye, level_masks, bdmask):
    ds = [eye - jnp.where(level_masks[0], m, 0.0) for m in ms]
    for mask in level_masks[1:]:
        es = [_dot(jnp.where(mask, m, 0.0), _bd(d, bdmask)) for m, d in zip(ms, ds)]
        fs = [_dot(d, _bd(e, bdmask)) for d, e in zip(ds, es)]
        ds = [d - f for d, f in zip(ds, fs)]
    return ds


def _gdn_prep_kernel(prev_ref, cur_ref, next_ref, ab_ref, cw_ref, alog_ref, dtb_ref,
                     tri_ref, same_ref, exp_ref, ones_ref,
                     pf_ref, qpf_ref, bmf_ref, o0f_ref, df_ref, pb_ref, qpb_ref, bmb_ref, o0b_ref, db_ref,
                     ext_ref, q_scr, k_scr, v_scr):
    tm = cur_ref.shape[1]
    halo = prev_ref.shape[1]
    top, bot = _edge_masked(prev_ref, next_ref)
    ext_ref[0:halo] = top
    ext_ref[halo:halo + tm] = cur_ref[0].astype(F32)
    ext_ref[halo + tm:] = bot
    c = GDN_C
    g0, grp = 0, tm
    ls = rs = [slice(ci * c, (ci + 1) * c) for ci in range(grp // c)]
    rows_g = slice(g0, g0 + grp)
    base = halo - GDN_CONV // 2
    for r0 in range(g0, g0 + grp, GDN_SUB):
        acc = jnp.zeros((GDN_SUB, 3 * GROUP_W), F32)
        for t in range(GDN_CONV):
            acc = acc + cw_ref[t:t + 1, :] * ext_ref[r0 + base + t:r0 + base + t + GDN_SUB, :]
        x = acc * jax.nn.sigmoid(acc)
        rows = slice(r0, r0 + GDN_SUB)
        q_scr[rows, :] = x[:, :GROUP_W]
        k_scr[rows, :] = x[:, GROUP_W:2 * GROUP_W]
        v_scr[rows, :] = x[:, 2 * GROUP_W:]
    qa, ka = q_scr[rows_g, :], k_scr[rows_g, :]
    q_scr[rows_g, :] = qa * lax.rsqrt(_dot01(qa * qa, ones_ref[...], 2) + EPS) * (GDN_HEAD_DIM ** -0.5)
    k_scr[rows_g, :] = ka * lax.rsqrt(_dot01(ka * ka, ones_ref[...], 2) + EPS)

    ab = ab_ref[0, rows_g, :]
    lane = lax.broadcasted_iota(jnp.int32, ab.shape, 1)
    g = -jnp.exp(alog_ref[...]) * _softplus(ab + dtb_ref[...])
    gate = jnp.where(lane < 2 * GDN_HEADS, g, jax.nn.sigmoid(ab))
    cums = [_dot01_left(tri_ref[...], gate[r], 2) for r in ls]
    lane_c = lax.broadcasted_iota(jnp.int32, (c, AB_PAD), 1)
    cum = jnp.concatenate([jnp.where(lane_c < GDN_HEADS, cm[:c], cm[c:]) for cm in cums], axis=0)
    wide = _dot01(jnp.where(lane < 2 * GDN_HEADS, cum, gate), exp_ref[...], 2)
    gcs = (wide[:, :GROUP_W], wide[:, GROUP_W:2 * GROUP_W])
    betas = (wide[:, 2 * GROUP_W:3 * GROUP_W], wide[:, 3 * GROUP_W:])

    ii = lax.broadcasted_iota(jnp.int32, (c, GROUP_W), 0)
    jj = lax.broadcasted_iota(jnp.int32, (c, GROUP_W), 1) & (c - 1)
    diag2 = jnp.concatenate([ii == jj] * 2, axis=1)
    rows_gc = [_dot01_left(same_ref[...], jnp.where(diag2, wide[r, :2 * GROUP_W], 0.0), 2) for r in ls]
    gams = ([jnp.where(ii >= jj, jnp.exp(jnp.minimum(gcs[0][r] - rg[:, :GROUP_W], 0.0)), 0.0)
             for r, rg in zip(ls, rows_gc)],
            [jnp.where(ii <= jj, jnp.exp(jnp.minimum(gcs[1][r] - rg[:, GROUP_W:], 0.0)), 0.0)
             for r, rg in zip(ls, rows_gc)])
    eye = (ii == jj).astype(F32)
    stricts = (ii > jj, ii < jj)
    levels = [((ii // (2 * s)) == (jj // (2 * s))) & ((ii // s) != (jj // s)) for s in (1, 2, 4, 8, 16, 32)]
    bdmask = _bd_mask()
    outs = ((pf_ref, qpf_ref, bmf_ref, o0f_ref, df_ref), (pb_ref, qpb_ref, bmb_ref, o0b_ref, db_ref))

    qs = [q_scr[r, :] for r in rs]
    ks = [k_scr[r, :] for r in rs]
    vs = [v_scr[r, :] for r in rs]
    kqs = [lax.dot_general(jnp.concatenate([k, q], axis=0).astype(BF16), _bd(k, bdmask),
                           (((1,), (1,)), ((), ())), preferred_element_type=F32) for k, q in zip(ks, qs)]
    pairs = [(di, ci) for di in range(2) for ci in range(len(rs))]
    gc_l = [gcs[di][ls[ci]] for di, ci in pairs]
    gam_l = [gams[di][ci] for di, ci in pairs]
    beta_l = [betas[di][ls[ci]] for di, ci in pairs]
    ms = [jnp.where(stricts[di], beta * kqs[ci][:c] * gam, 0.0)
          for (di, ci), beta, gam in zip(pairs, beta_l, gam_l)]
    ts = _tri_inverse(ms, eye, levels, bdmask)
    egcs = [jnp.exp(gc) for gc in gc_l]
    uws = [_dot(t, jnp.concatenate([_bd(vs[ci] * beta, bdmask), _bd(ks[ci] * beta * egc, bdmask)], axis=1))
           for (_, ci), t, beta, egc in zip(pairs, ts, beta_l, egcs)]
    aqks = [kqs[ci][c:] * gam for (_, ci), gam in zip(pairs, gam_l)]
    ows = [_dot(aqk, jnp.concatenate([_bd(uw[:, :GROUP_W], bdmask), _bd(uw[:, GROUP_W:], bdmask)], axis=1))
           for aqk, uw in zip(aqks, uws)]
    for n, (di, ci) in enumerate(pairs):
        p_ref, qp_ref, bm_ref, o0_ref, d_ref = outs[di]
        r, gc, uw, ow = rs[ci], gc_l[n], uws[n], ows[n]
        glast = gc[0:1] if di else gc[c - 1:c]
        kdec = ks[ci] * jnp.exp(glast - gc)
        bp = _diag_blocks(lax.dot_general(kdec.astype(BF16), uw.astype(BF16), (((0,), (0,)), ((), ())),
                                          preferred_element_type=F32))
        bm_ref[0, r, :] = bp[:, :GROUP_W]
        p_ref[0, r, :] = bp[:, GROUP_W:].astype(BF16)
        o0_ref[0, r, :] = ow[:, :GROUP_W]
        qp_ref[0, r, :] = (qs[ci] * egcs[n] - ow[:, GROUP_W:]).astype(BF16)
        d_ref[0, g0 // c + ci:g0 // c + ci + 1, :] = jnp.exp(glast)


def _gdn_prep(qkv, ab, conv_w, a_log, dt_bias, tm):
    b, t, w3 = qkv.shape
    halo = 16
    nck = tm // GDN_C
    prev, cur, nxt = _halo_specs(tm, halo, t, w3)
    pad = lambda a: jnp.pad(a.reshape(1, -1), ((0, 0), (0, AB_PAD - a.size)))
    consts = _gdn_consts()
    tok = pl.BlockSpec((1, tm, GROUP_W), lambda i, j: (i, j, 0))
    dec = pl.BlockSpec((1, nck, GROUP_W), lambda i, j: (i, j, 0))
    tok_shape = lambda dt: jax.ShapeDtypeStruct((b, t, GROUP_W), dt)
    dir_specs = [tok, tok, tok, tok, dec]
    dir_shapes = [tok_shape(BF16), tok_shape(BF16), tok_shape(F32), tok_shape(F32),
                  jax.ShapeDtypeStruct((b, t // GDN_C, GROUP_W), F32)]
    return pl.pallas_call(
        _gdn_prep_kernel,
        grid=(b, t // tm),
        in_specs=[prev, cur, nxt, pl.BlockSpec((1, tm, AB_PAD), lambda i, j: (i, j, 0)),
                  _resident(conv_w.shape), _resident((1, AB_PAD)), _resident((1, AB_PAD))]
        + [_resident(cst.shape) for cst in consts],
        out_specs=dir_specs * 2,
        out_shape=dir_shapes * 2,
        scratch_shapes=[pltpu.VMEM((tm + 2 * halo, w3), F32)] + [pltpu.VMEM((tm, GROUP_W), F32)] * 3,
        compiler_params=_params(2), name="gdn_prep",
    )(qkv, qkv, qkv, ab, conv_w, pad(a_log), pad(dt_bias), *consts)


def _gdn_scan_kernel(pf_ref, qpf_ref, bmf_ref, o0f_ref, df_ref, pb_ref, qpb_ref, bmb_ref, o0b_ref, db_ref,
                     s0f_ref, s0b_ref, of_ref, ob_ref, sf_ref, sb_ref):
    nb, tm = pf_ref.shape[0], pf_ref.shape[1]
    nc = tm // GDN_C

    @pl.when(pl.program_id(0) == 0)
    def _():
        sf_ref[...] = s0f_ref[...]
        sb_ref[...] = s0b_ref[...]

    bdmask = _bd_mask()
    fwd = (pf_ref, qpf_ref, bmf_ref, o0f_ref, df_ref, of_ref)
    bwd = (pb_ref, qpb_ref, bmb_ref, o0b_ref, db_ref, ob_ref)
    chains = [(fwd, bi, False) for bi in range(nb)] + [(bwd, bi, True) for bi in range(nb)]
    states = [sf_ref[bi] for bi in range(nb)] + [sb_ref[bi] for bi in range(nb)]
    for step in range(nc):
        lhs, rows = [], []
        for (p_ref, qp_ref, _, _, _, _), bi, rev in chains:
            ci = nc - 1 - step if rev else step
            r = slice(ci * GDN_C, (ci + 1) * GDN_C)
            rows.append((ci, r))
            lhs.append(jnp.concatenate([p_ref[bi, r, :], qp_ref[bi, r, :]], axis=0))
        res = [jnp.dot(a, _bd(s, bdmask), preferred_element_type=F32) for a, s in zip(lhs, states)]
        new_states = []
        for (_, _, bm_ref, o0_ref, d_ref, o_ref), bi, _ in chains:
            n = len(new_states)
            ci, r = rows[n]
            o_ref[bi, r, :] = o0_ref[bi, r, :] + res[n][GDN_C:]
            new_states.append(d_ref[bi, ci:ci + 1, :] * states[n] - res[n][:GDN_C] + bm_ref[bi, r, :])
        states = new_states
    for bi in range(nb):
        sf_ref[bi] = states[bi]
        sb_ref[bi] = states[nb + bi]


def _gdn_scan(prep, s0f, s0b, tm):
    b, t, w = prep[0].shape
    nj = t // tm
    nck = tm // GDN_C
    fwd = pl.BlockSpec((b, tm, w), lambda j: (0, j, 0))
    bwd = pl.BlockSpec((b, tm, w), lambda j: (0, nj - 1 - j, 0))
    dfwd = pl.BlockSpec((b, nck, w), lambda j: (0, j, 0))
    dbwd = pl.BlockSpec((b, nck, w), lambda j: (0, nj - 1 - j, 0))
    st = pl.BlockSpec((b, GDN_HEAD_DIM, w), lambda j: (0, 0, 0))
    st_shape = jax.ShapeDtypeStruct((b, GDN_HEAD_DIM, w), F32)
    return pl.pallas_call(
        _gdn_scan_kernel,
        grid=(nj,),
        in_specs=[fwd] * 4 + [dfwd] + [bwd] * 4 + [dbwd] + [st, st],
        out_specs=[fwd, bwd, st, st],
        out_shape=[jax.ShapeDtypeStruct((b, t, w), F32)] * 2 + [st_shape] * 2,
        compiler_params=pltpu.CompilerParams(dimension_semantics=("arbitrary",),
                                             vmem_limit_bytes=VMEM_LIMIT),
        name="gdn_scan",
    )(*prep, s0f, s0b)


def _mix_out_kernel(h_ref, yp_ref, yfa_ref, yfb_ref, yc_ref, of_ref, ob_ref, z_ref, gt_ref, nw_ref, ones_ref,
                    w_ref, o_ref):
    o = of_ref[0] + ob_ref[0]
    ms = _dot01(o * o, ones_ref[...], 2) * (1.0 / GDN_HEAD_DIM)
    z = z_ref[0].astype(F32)
    yg = o * lax.rsqrt(ms + EPS) * nw_ref[...] * (z * jax.nn.sigmoid(z))
    y = jnp.zeros(h_ref.shape[1:], F32)
    yf = jnp.concatenate([yfa_ref[0], yfb_ref[0]], axis=-1)
    for gi, part in enumerate((yp_ref[0], yf, yc_ref[0], yg)):
        y = y + jnp.dot(part.astype(BF16), w_ref[gi * GROUP_W:(gi + 1) * GROUP_W, :],
                        preferred_element_type=F32)
    o_ref[0] = h_ref[0] + gt_ref[0] * y


def _mix_out(h, ys, o_f, o_b, z, gate, gdn_nw, w_out, layer, tm):
    b, t, d = h.shape
    tok = pl.BlockSpec((1, tm, d), lambda i, j: (i, j, 0))
    grp = pl.BlockSpec((1, tm, GROUP_W), lambda i, j: (i, j, 0))
    half = pl.BlockSpec((1, tm, LANES), lambda i, j: (i, j, 0))
    ones = _gdn_consts()[3]
    return pl.pallas_call(
        _mix_out_kernel,
        grid=(b, t // tm),
        in_specs=[tok, grp, half, half] + [grp] * 4 + [pl.BlockSpec((1, 1, d), lambda i, j: (i, 0, 0)),
                                     _resident((1, GROUP_W)), _resident(ones.shape), _layer_resident(w_out, layer)],
        out_specs=tok,
        out_shape=jax.ShapeDtypeStruct(h.shape, F32),
        compiler_params=_params(2), name="mix_out",
    )(h, *ys, o_f, o_b, z, gate, gdn_nw, ones, w_out)


MOD_ROWS = 8
MOD_TN = 1152


def _mod_kernel(c_ref, w_ref, b_ref, o_ref):
    cv = c_ref[...]
    a = (cv * jax.nn.sigmoid(cv)).astype(BF16)
    o_ref[0] = jnp.dot(a, w_ref[0].astype(BF16), preferred_element_type=F32) + b_ref[0]


def _modulation(c, c_ctx, mod_w, mod_b):
    nl, d, n = mod_w.shape
    cond = jnp.concatenate([c, c_ctx[None, :]], axis=0)
    cond = jnp.pad(cond, ((0, MOD_ROWS - cond.shape[0]), (0, 0)))
    return pl.pallas_call(
        _mod_kernel,
        grid=(nl, n // MOD_TN),
        in_specs=[_resident(cond.shape), pl.BlockSpec((1, d, MOD_TN), lambda i, j: (i, 0, j)),
                  pl.BlockSpec((1, 1, MOD_TN), lambda i, j: (i, 0, j))],
        out_specs=pl.BlockSpec((1, MOD_ROWS, MOD_TN), lambda i, j: (i, 0, j)),
        out_shape=jax.ShapeDtypeStruct((nl, MOD_ROWS, n), F32),
        compiler_params=_params(2), name="modulation",
    )(cond, mod_w, mod_b[:, None, :])


def _token_mix(p, gdn_state, grid_rows, tm, wl, need_out):
    p_pool, p_four, p_conv, p_qkv, p_z, p_ab = p
    prep = _gdn_prep(p_qkv, p_ab, wl["gdn_conv_w"], wl["gdn_a_log"], wl["gdn_dt_bias"], tm)
    o_f, o_b, s_f, s_b = _gdn_scan(prep, *gdn_state, tm)
    if not need_out:
        return None, (s_f, s_b)
    ys = (_pool_mix(p_pool, wl["pool_wbd"], wl["pool_scale"], grid_rows),
          *_fourier_mix(p_four, wl["fourier_w"]),
          _conv_mix(p_conv, wl["conv_dw_w"], wl["conv_dw_b"], wl["conv_ln_g"], wl["conv_ln_b"],
                    wl["conv_pw_w"], tm))
    return (ys, o_f, o_b, p_z), (s_f, s_b)


def kernel(x, c, ctx, c_ctx, mod_w, mod_b, norm_w, ffn1_wg, ffn1_wu, ffn1_wd, ffn2_wg, ffn2_wu,
           ffn2_wd, w_in, w_out, pool_w, pool_scale, fourier_w, conv_dw_w, conv_dw_b, conv_ln_g,
           conv_ln_b, conv_pw_w, gdn_conv_w, gdn_a_log, gdn_dt_bias, gdn_norm_w, final_norm_w):
    bsz, seq, d = x.shape
    n_ctx = ctx.shape[1]
    rows = seq // GRID_W
    tm_x, tm_c = 512, n_ctx
    hx, hc = x, ctx
    mods = _modulation(c, c_ctx, mod_w, mod_b)
    zero_state = (jnp.zeros((bsz, GDN_HEAD_DIM, GROUP_W), F32),) * 2
    f1 = tuple(_cast_bf16(w) for w in (ffn1_wg, ffn1_wu, ffn1_wd))
    f2 = tuple(_cast_bf16(w) for w in (ffn2_wg, ffn2_wu, ffn2_wd))
    w_main = _cast_bf16(w_in, cols=2 * GDN_OFF)
    w_ab = jnp.pad(w_in[:, :, 2 * GDN_OFF:], ((0, 0), (0, 0), (0, AB_PAD - N_GATE))).astype(BF16)
    w_out_b = _cast_bf16(w_out)
    for l in range(DEPTH):
        last = l == DEPTH - 1
        mx = [m[:, None, :] for m in jnp.split(mods[l, :bsz], N_MOD, axis=-1)]
        mc = [jnp.broadcast_to(m[None], (bsz, 1, d)) for m in jnp.split(mods[l, bsz:bsz + 1], N_MOD, axis=-1)]
        nw = norm_w[l][:, None, :]
        row = lambda a: a.reshape(1, -1)
        wl = dict(pool_wbd=_blockdiag(pool_w[l]).astype(BF16), pool_scale=row(pool_scale[l]),
                  fourier_w=fourier_w[l].astype(BF16), conv_dw_w=conv_dw_w[l], conv_dw_b=row(conv_dw_b[l]),
                  conv_ln_g=row(conv_ln_g[l]), conv_ln_b=row(conv_ln_b[l]),
                  conv_pw_w=conv_pw_w[l].astype(BF16), gdn_conv_w=gdn_conv_w[l],
                  gdn_a_log=gdn_a_log[l], gdn_dt_bias=gdn_dt_bias[l])
        gdn_nw = row(jnp.tile(gdn_norm_w[l], GDN_HEADS))

        hx = _ffn(hx, nw[0], mx[0], mx[1], mx[2], f1, l, tm_x)
        hc = _ffn(hc, nw[0], mc[0], mc[1], mc[2], f1, l, tm_c)

        px = _inproj(hx, nw[1], mx[3], mx[4], w_main, w_ab, l, False, tm_x)
        pc = _inproj(hc, nw[1], mc[3], mc[4], w_main, w_ab, l, last, tm_c)
        if last:
            pc = (None,) * 3 + tuple(pc)

        mix_c, ctx_state = _token_mix(pc, zero_state, None, tm_c, wl, not last)
        mix_x, _ = _token_mix(px, ctx_state, rows, tm_x, wl, True)
        hx = _mix_out(hx, *mix_x, mx[5], gdn_nw, w_out_b, l, tm_x)
        hx = _ffn(hx, nw[2], mx[6], mx[7], mx[8], f2, l, tm_x, final_norm_w[None, :] if last else None)
        if not last:
            hc = _mix_out(hc, *mix_c, mc[5], gdn_nw, w_out_b, l, tm_c)
            hc = _ffn(hc, nw[2], mc[6], mc[7], mc[8], f2, l, tm_c)
    return hx
```

```python
import functools
import math

import jax
import jax.numpy as jnp
import numpy as np
from jax import lax
from jax.experimental import pallas as pl
from jax.experimental.pallas import tpu as pltpu

D_MODEL = 1024
DEPTH = 4
GRID_W = 64
N_MIXERS = 4
GROUP_W = D_MODEL // N_MIXERS
POOL_WINDOWS = (2, 4, 8, 16)
POOL_GROUPS = 4
POOL_GW = GROUP_W // POOL_GROUPS
FOURIER_GROUPS = 4
FOURIER_GW = GROUP_W // FOURIER_GROUPS
CONV_K = 31
GDN_HEAD_DIM = 64
GDN_HEADS = GROUP_W // GDN_HEAD_DIM
GDN_CONV = 3
GDN_CHUNK = 64
FFN_HIDDEN = 128 * ((8 * D_MODEL // 3 + 127) // 128)
N_MOD = 9
EPS = 1e-6
POOL_OFF = 0
FOURIER_OFF = POOL_OFF + GROUP_W
CONV_OFF = FOURIER_OFF + GROUP_W
GDN_OFF = CONV_OFF + 2 * GROUP_W

LANES = 128
SUBLANES = 8
VMEM_LIMIT = 56 * 1024 * 1024
FFN_CHUNK = 256
AB_PAD = LANES

BF16 = jnp.bfloat16
F32 = jnp.float32


def _params(n_axes):
    return pltpu.CompilerParams(dimension_semantics=("parallel",) * n_axes,
                                vmem_limit_bytes=VMEM_LIMIT)


def _resident(shape):
    nd = len(shape)
    return pl.BlockSpec(shape, lambda *_: (0,) * nd, pipeline_mode=pl.Buffered(1))


def _layer_resident(stack, layer, cols=None, col_block=0):
    _, r, c = stack.shape
    return pl.BlockSpec((None, r, cols or c), lambda *_: (layer, 0, col_block), pipeline_mode=pl.Buffered(1))


def _cast_kernel(x_ref, o_ref):
    o_ref[...] = x_ref[...].astype(BF16)


def _cast_bf16(w, cols=None, rows_per_step=256):
    nl, r, c = w.shape
    cols = cols or c
    spec = pl.BlockSpec((1, rows_per_step, cols), lambda i, j: (i, j, 0))
    return pl.pallas_call(
        _cast_kernel, grid=(nl, r // rows_per_step), in_specs=[spec], out_specs=spec,
        out_shape=jax.ShapeDtypeStruct((nl, r, cols), BF16), compiler_params=_params(2), name="cast_bf16",
    )(w)


def _rms_mod(h, nw, shift, scale):
    ms = jnp.mean(h * h, axis=-1, keepdims=True)
    n = h * lax.rsqrt(ms + EPS) * nw
    return n * (1.0 + scale) + shift


def _ffn_apply(h, nw_ref, sh_ref, sc_ref, gt_ref, wg_ref, wu_ref, wd_ref, final_nw_ref=None):
    nb = _rms_mod(h, nw_ref[...], sh_ref[0], sc_ref[0]).astype(BF16)
    acc = jnp.zeros(h.shape, F32)
    for f0 in range(0, FFN_HIDDEN, FFN_CHUNK):
        g = jnp.dot(nb, wg_ref[:, f0:f0 + FFN_CHUNK], preferred_element_type=F32)
        u = jnp.dot(nb, wu_ref[:, f0:f0 + FFN_CHUNK], preferred_element_type=F32)
        a = (g * jax.nn.sigmoid(g) * u).astype(BF16)
        acc = acc + jnp.dot(a, wd_ref[f0:f0 + FFN_CHUNK, :], preferred_element_type=F32)
    out = h + (0.5 * gt_ref[0]) * acc
    if final_nw_ref is not None:
        out = out * lax.rsqrt(jnp.mean(out * out, axis=-1, keepdims=True) + EPS) * final_nw_ref[...]
    return out


def _ffn_kernel(h_ref, *rest):
    rest[-1][0] = _ffn_apply(h_ref[0], *rest[:-1])


def _ffn(h, nw, shift, scale, gate, weights, layer, tm, final_nw=None):
    b, t, d = h.shape
    vec = pl.BlockSpec((1, 1, d), lambda i, j: (i, 0, 0))
    tok = pl.BlockSpec((1, tm, d), lambda i, j: (i, j, 0))
    extra = [] if final_nw is None else [final_nw]
    return pl.pallas_call(
        _ffn_kernel,
        grid=(b, t // tm),
        in_specs=[tok, _resident((1, d)), vec, vec, vec] + [_layer_resident(w, layer) for w in weights]
        + [_resident((1, d))] * len(extra),
        out_specs=tok,
        out_shape=jax.ShapeDtypeStruct(h.shape, F32),
        compiler_params=_params(2),
        name="ffn",
    )(h, nw, shift, scale, gate, *weights, *extra)


IN_SPLITS = (GROUP_W, GROUP_W, 2 * GROUP_W, 3 * GROUP_W, GROUP_W, AB_PAD)


def _inproj_kernel(h_ref, nw_ref, sh_ref, sc_ref, w_ref, wab_ref, *o_refs):
    tm = h_ref.shape[1]
    half = tm // 2 if tm % (2 * SUBLANES) == 0 else tm
    for r0 in range(0, tm, half):
        rows = slice(r0, r0 + half)
        nb = _rms_mod(h_ref[0, rows, :], nw_ref[...], sh_ref[0], sc_ref[0]).astype(BF16)
        off = 0
        for o_ref in o_refs[:-1]:
            wdt = o_ref.shape[-1]
            o_ref[0, rows, :] = jnp.dot(nb, w_ref[:, off:off + wdt], preferred_element_type=F32)
            off += wdt
        o_refs[-1][0, rows, :] = jnp.dot(nb, wab_ref[...], preferred_element_type=F32)


def _inproj(h, nw, shift, scale, w_main, w_ab, layer, gdn_only, tm):
    b, t, d = h.shape
    vec = pl.BlockSpec((1, 1, d), lambda i, j: (i, 0, 0))
    tok = pl.BlockSpec((1, tm, d), lambda i, j: (i, j, 0))
    splits = IN_SPLITS[3:] if gdn_only else IN_SPLITS
    w_spec = (_layer_resident(w_main, layer, GDN_OFF, 1) if gdn_only else _layer_resident(w_main, layer))
    return pl.pallas_call(
        _inproj_kernel,
        grid=(b, t // tm),
        in_specs=[tok, _resident((1, d)), vec, vec, w_spec, _layer_resident(w_ab, layer)],
        out_specs=[pl.BlockSpec((1, tm, s), lambda i, j: (i, j, 0)) for s in splits],
        out_shape=[jax.ShapeDtypeStruct((b, t, s), F32) for s in splits],
        compiler_params=_params(2),
        name="inproj",
    )(h, nw, shift, scale, w_main, w_ab)


def _split_bf16(x, parts):
    out = []
    for _ in range(parts - 1):
        p = x.astype(BF16)
        out.append(p)
        x = x - p.astype(F32)
    return out + [x.astype(BF16)]


def _dot01(x, w01, parts=3):
    return sum(jnp.dot(p, w01, preferred_element_type=F32) for p in _split_bf16(x, parts))


def _dot01_left(w01, x, parts=3):
    return sum(jnp.dot(w01, p, preferred_element_type=F32) for p in _split_bf16(x, parts))


def _blockdiag(w):
    g, a, b = w.shape
    return jnp.einsum('gab,gh->gahb', w, jnp.eye(g, dtype=w.dtype)).reshape(g * a, g * b)


def _halo_specs(tm, halo, t, width):
    r = tm // halo
    last = t // halo - 1
    prev = pl.BlockSpec((1, halo, width), lambda i, j: (i, jnp.maximum(j * r - 1, 0), 0))
    cur = pl.BlockSpec((1, tm, width), lambda i, j: (i, j, 0))
    nxt = pl.BlockSpec((1, halo, width), lambda i, j: (i, jnp.minimum((j + 1) * r, last), 0))
    return prev, cur, nxt


def _edge_masked(prev_ref, next_ref):
    j, nj = pl.program_id(1), pl.num_programs(1)
    top = jnp.where(j > 0, prev_ref[0], 0.0)
    bot = jnp.where(j < nj - 1, next_ref[0], 0.0)
    return top, bot


POOL_TM = 1024
POOL_HALO = 512


def _pool_window_sums_1d(u, row_len):
    n = u.shape[0]
    col = lax.broadcasted_iota(jnp.int32, u.shape, 0) & (row_len - 1)
    grp = lax.broadcasted_iota(jnp.int32, u.shape, 1) // POOL_GW

    def back(x, s):
        return jnp.where(col >= s, pltpu.roll(x, s, 0), 0.0)

    def fwd(x, s):
        return jnp.where(col < row_len - s, pltpu.roll(x, n - s, 0), 0.0)

    b = back(u, 1)
    f = u
    out = b + f
    for gi in range(1, len(POOL_WINDOWS)):
        s = POOL_WINDOWS[gi] // 4
        b = b + back(b, s)
        f = f + fwd(f, s)
        out = jnp.where(grp >= gi, b + f, out)
    return out


def _pool_counts(idx, extent, halfw):
    return jnp.minimum(idx + halfw, extent) - jnp.maximum(idx - halfw, 0)


def _pool_finish(mean, u, wbd_ref, scale_ref, o_ref):
    d = (mean - u).astype(BF16)
    o_ref[0] = (jnp.dot(d, wbd_ref[...], preferred_element_type=F32) * scale_ref[...]).astype(BF16)


def _pool_grid_kernel(prev_ref, cur_ref, next_ref, wbd_ref, scale_ref, o_ref, *, n_rows):
    tm = cur_ref.shape[1]
    top, bot = _edge_masked(prev_ref, next_ref)
    u = cur_ref[0]
    grp = lax.broadcasted_iota(jnp.int32, (tm, GROUP_W), 1) // POOL_GW
    arr, off = jnp.concatenate([top, u, bot], axis=0), 0
    tot = None
    for gi, w in enumerate(POOL_WINDOWS):
        sh = GRID_W * max(w // 4, 1) if gi else GRID_W
        if gi == 0:
            arr = arr[:-sh] + arr[sh:]
            off = sh
        else:
            arr = arr[:-2 * sh] + arr[2 * sh:]
            off = off + sh
        centre = arr[POOL_HALO - off:POOL_HALO - off + tm]
        tot = centre if tot is None else jnp.where(grp >= gi, centre, tot)
    tot = _pool_window_sums_1d(tot, GRID_W)
    tok = pl.program_id(1) * tm + lax.broadcasted_iota(jnp.int32, (tm, GROUP_W), 0)
    halfw = jnp.left_shift(1, grp)
    cnt = (_pool_counts(tok // GRID_W, n_rows, halfw) * _pool_counts(tok & (GRID_W - 1), GRID_W, halfw))
    _pool_finish(tot / cnt.astype(F32), u, wbd_ref, scale_ref, o_ref)


def _pool_seq_kernel(u_ref, wbd_ref, scale_ref, o_ref):
    u = u_ref[0]
    n = u.shape[0]
    tot = _pool_window_sums_1d(u, n)
    grp = lax.broadcasted_iota(jnp.int32, u.shape, 1) // POOL_GW
    tok = lax.broadcasted_iota(jnp.int32, u.shape, 0)
    cnt = _pool_counts(tok, n, jnp.left_shift(1, grp))
    _pool_finish(tot / cnt.astype(F32), u, wbd_ref, scale_ref, o_ref)


def _pool_mix(u, wbd, scale, grid_rows):
    b, t, w = u.shape
    out_shape = jax.ShapeDtypeStruct(u.shape, BF16)
    if grid_rows is None:
        tok = pl.BlockSpec((1, t, w), lambda i: (i, 0, 0))
        return pl.pallas_call(
            _pool_seq_kernel, grid=(b,),
            in_specs=[tok, _resident(wbd.shape), _resident(scale.shape)],
            out_specs=tok, out_shape=out_shape, compiler_params=_params(1), name="pool_seq",
        )(u, wbd, scale)
    prev, cur, nxt = _halo_specs(POOL_TM, POOL_HALO, t, w)
    return pl.pallas_call(
        functools.partial(_pool_grid_kernel, n_rows=grid_rows),
        grid=(b, t // POOL_TM),
        in_specs=[prev, cur, nxt, _resident(wbd.shape), _resident(scale.shape)],
        out_specs=cur, out_shape=out_shape, compiler_params=_params(2), name="pool_grid",
    )(u, u, u, wbd, scale)


CONV_TM = 512
CONV_HALO = 16
CONV_SUB = 64


def _glu(x):
    return x[:, :GROUP_W] * jax.nn.sigmoid(x[:, GROUP_W:])


def _conv_kernel(prev_ref, cur_ref, next_ref, dww_ref, dwb_ref, lng_ref, lnb_ref, pw_ref, o_ref,
                 ext_ref, sh_ref):
    tm = cur_ref.shape[1]
    top, bot = _edge_masked(prev_ref, next_ref)
    ext_ref[0:CONV_HALO] = _glu(top)
    ext_ref[CONV_HALO:CONV_HALO + tm] = _glu(cur_ref[0])
    ext_ref[CONV_HALO + tm:] = _glu(bot)
    n_sh = sh_ref.shape[1]
    for s in range(SUBLANES):
        sh_ref[s] = ext_ref[s:s + n_sh, :]
    base = CONV_HALO - CONV_K // 2
    for r0 in range(0, tm, CONV_SUB):
        acc = jnp.zeros((CONV_SUB, GROUP_W), F32)
        for k in range(CONV_K):
            a, s = divmod(base + k, SUBLANES)
            acc = acc + dww_ref[k:k + 1, :] * sh_ref[s, r0 + a * SUBLANES:r0 + a * SUBLANES + CONV_SUB, :]
        h = acc + dwb_ref[...]
        mu = jnp.mean(h, axis=-1, keepdims=True)
        var = jnp.mean(jnp.square(h - mu), axis=-1, keepdims=True)
        h = (h - mu) * lax.rsqrt(var + EPS) * lng_ref[...] + lnb_ref[...]
        h = (h * jax.nn.sigmoid(h)).astype(BF16)
        o_ref[0, r0:r0 + CONV_SUB, :] = jnp.dot(h, pw_ref[...], preferred_element_type=F32).astype(BF16)


def _conv_mix(u2, dw_w, dw_b, ln_g, ln_b, pw_w, tm):
    b, t, w2 = u2.shape
    prev, cur, nxt = _halo_specs(tm, CONV_HALO, t, w2)
    row = _resident((1, GROUP_W))
    return pl.pallas_call(
        _conv_kernel,
        grid=(b, t // tm),
        in_specs=[prev, cur, nxt, _resident(dw_w.shape), row, row, row, _resident(pw_w.shape)],
        out_specs=pl.BlockSpec((1, tm, GROUP_W), lambda i, j: (i, j, 0)),
        out_shape=jax.ShapeDtypeStruct((b, t, GROUP_W), BF16),
        scratch_shapes=[pltpu.VMEM((tm + 2 * CONV_HALO, GROUP_W), F32),
                        pltpu.VMEM((SUBLANES, tm + 2 * CONV_HALO - SUBLANES, GROUP_W), F32)],
        compiler_params=_params(2), name="conv_mix",
    )(u2, u2, u2, dw_w, dw_b, ln_g, ln_b, pw_w)


FOUR_N1 = 128
FOUR_TT = 8
FOUR_KT = 8


def _hilo(w):
    w = jnp.asarray(w, F32)
    hi = w.astype(BF16)
    return hi, (w - hi.astype(F32)).astype(BF16)


def _dot_hp(x, wh, wl):
    xh = x.astype(BF16)
    xl = (x - xh.astype(F32)).astype(BF16)
    return (jnp.dot(xh, wh, preferred_element_type=F32) + jnp.dot(xh, wl, preferred_element_type=F32)
            + jnp.dot(xl, wh, preferred_element_type=F32))


def _dot_hp_left(wh, wl, x):
    xh = x.astype(BF16)
    xl = (x - xh.astype(F32)).astype(BF16)
    return (jnp.dot(wh, xh, preferred_element_type=F32) + jnp.dot(wl, xh, preferred_element_type=F32)
            + jnp.dot(wh, xl, preferred_element_type=F32))


def _cos_sin(n, rows=None, cols=None):
    r = np.arange(n if rows is None else rows, dtype=np.int64)
    c = np.arange(n if cols is None else cols, dtype=np.int64)
    ang = 2.0 * np.pi * ((np.outer(r, c) % n).astype(np.float64) / n)
    return np.cos(ang), np.sin(ang)


def _channel_dft():
    c, s = _cos_sin(FOURIER_GW)
    eye = np.eye(FOURIER_GROUPS)
    return np.concatenate([np.kron(eye, c), -np.kron(eye, s)], axis=1)


def _fourier_a_kernel(ua_ref, ub_ref, cdh_ref, cdl_ref, f1h_ref, f1l_ref, twc_ref, tws_ref, o_ref):
    for j in range(FOUR_TT):
        x = jnp.concatenate([ua_ref[:, j, :], ub_ref[:, j, :]], axis=1)
        z = _dot_hp(x, cdh_ref[...], cdl_ref[...])
        zz = jnp.concatenate([z[:, :GROUP_W], z[:, GROUP_W:]], axis=0)
        a = _dot_hp_left(f1h_ref[...], f1l_ref[...], zz)
        ar, ai = a[:FOUR_N1], a[FOUR_N1:]
        c = jnp.concatenate([twc_ref[j]] * (GROUP_W // LANES), axis=1)
        s = jnp.concatenate([tws_ref[j]] * (GROUP_W // LANES), axis=1)
        o_ref[0, 0, j] = ar * c + ai * s
        o_ref[0, 1, j] = ai * c - ar * s


def _fourier_b_kernel(ga_ref, gb_ref, f2h_ref, f2l_ref, fw_ref, oa_ref, ob_ref, *, scale):
    for kk in range(FOUR_KT):
        gm = jnp.concatenate([ga_ref[:, :, kk, :], gb_ref[:, :, kk, :]], axis=-1)
        gm = gm.reshape(2 * FOUR_N1, GROUP_W)
        f = _dot_hp_left(f2h_ref[...], f2l_ref[...], gm) * scale
        y = jnp.dot(f.astype(BF16), fw_ref[...], preferred_element_type=F32)
        oa_ref[:, kk, :] = y[:, :LANES]
        ob_ref[:, kk, :] = y[:, LANES:]


def _fourier_small_kernel(u_ref, cdh_ref, cdl_ref, fh_ref, fl_ref, fw_ref, oa_ref, ob_ref, *, scale):
    z = _dot_hp(u_ref[0], cdh_ref[...], cdl_ref[...])
    zz = jnp.concatenate([z[:, :GROUP_W], z[:, GROUP_W:]], axis=0)
    f = _dot_hp_left(fh_ref[...], fl_ref[...], zz) * scale
    y = jnp.dot(f.astype(BF16), fw_ref[...], preferred_element_type=F32)
    oa_ref[0] = y[:, :LANES]
    ob_ref[0] = y[:, LANES:]


def _fourier_mix(u, fw):
    b, n, w = u.shape
    scale = 1.0 / math.sqrt(n * FOURIER_GW)
    cdh, cdl = _hilo(_channel_dft())
    if n != FOUR_N1 * FOUR_N1:
        c, s = _cos_sin(n)
        fh, fl = _hilo(np.concatenate([c, s], axis=1))
        tok = pl.BlockSpec((1, n, w), lambda i: (i, 0, 0))
        half = pl.BlockSpec((1, n, LANES), lambda i: (i, 0, 0))
        return pl.pallas_call(
            functools.partial(_fourier_small_kernel, scale=scale), grid=(b,),
            in_specs=[tok] + [_resident(a.shape) for a in (cdh, cdl, fh, fl, fw)],
            out_specs=[half, half], out_shape=[jax.ShapeDtypeStruct((b, n, LANES), F32)] * 2,
            compiler_params=_params(1), name="fourier_small",
        )(u, cdh, cdl, fh, fl, fw)
    n1 = FOUR_N1
    c1, s1 = _cos_sin(n1)
    f1h, f1l = _hilo(np.block([[c1, s1], [-s1, c1]]))
    f2h, f2l = _hilo(np.concatenate([c1, s1], axis=1))
    twc, tws = _cos_sin(n, rows=n1, cols=n1)
    twc = jnp.broadcast_to(jnp.asarray(twc, F32)[:, :, None], (n1, n1, LANES))
    tws = jnp.broadcast_to(jnp.asarray(tws, F32)[:, :, None], (n1, n1, LANES))
    tw_spec = pl.BlockSpec((FOUR_TT, n1, LANES), lambda i, j: (j, 0, 0))
    u4 = u.reshape(b, n1, n1, w)
    g = pl.pallas_call(
        _fourier_a_kernel, grid=(b, n1 // FOUR_TT),
        in_specs=[pl.BlockSpec((None, n1, FOUR_TT, LANES), lambda i, j: (i, 0, j, 0)),
                  pl.BlockSpec((None, n1, FOUR_TT, LANES), lambda i, j: (i, 0, j, 1))]
        + [_resident(a.shape) for a in (cdh, cdl, f1h, f1l)] + [tw_spec, tw_spec],
        out_specs=pl.BlockSpec((1, 2, FOUR_TT, n1, w), lambda i, j: (i, 0, j, 0, 0)),
        out_shape=jax.ShapeDtypeStruct((b, 2, n1, n1, w), F32),
        compiler_params=_params(2), name="fourier_a",
    )(u4, u4, cdh, cdl, f1h, f1l, twc, tws)
    half_out = pl.BlockSpec((None, n1, FOUR_KT, LANES), lambda i, j: (i, 0, j, 0))
    ya, yb = pl.pallas_call(
        functools.partial(_fourier_b_kernel, scale=scale), grid=(b, n1 // FOUR_KT),
        in_specs=[pl.BlockSpec((None, 2, n1, FOUR_KT, LANES), lambda i, j: (i, 0, 0, j, 0)),
                  pl.BlockSpec((None, 2, n1, FOUR_KT, LANES), lambda i, j: (i, 0, 0, j, 1))]
        + [_resident(a.shape) for a in (f2h, f2l, fw)],
        out_specs=[half_out, half_out],
        out_shape=[jax.ShapeDtypeStruct((b, n1, n1, LANES), F32)] * 2,
        compiler_params=_params(2), name="fourier_b",
    )(g, g, f2h, f2l, fw)
    return ya.reshape(b, n, LANES), yb.reshape(b, n, LANES)


GDN_C = GDN_CHUNK
GDN_PREP_TM = 512
GDN_SCAN_TM = 512
GDN_SUB = 64
N_GATE = 4 * GDN_HEADS


def _softplus(x):
    return jnp.maximum(x, 0.0) + jnp.log1p(jnp.exp(-jnp.abs(x)))


def _gdn_consts():
    i = np.arange(GDN_C)
    lower = np.concatenate([i[None, :] <= i[:, None], i[None, :] >= i[:, None]], axis=0)
    same = np.ones((GDN_C, GDN_C), bool)
    expand = np.zeros((AB_PAD, 4 * GROUP_W), np.float32)
    for s in range(4):
        for h in range(GDN_HEADS):
            c0 = s * GROUP_W + h * GDN_HEAD_DIM
            expand[s * GDN_HEADS + h, c0:c0 + GDN_HEAD_DIM] = 1.0
    lane = np.arange(GROUP_W)
    headones = (lane[:, None] // GDN_HEAD_DIM) == (lane[None, :] // GDN_HEAD_DIM)
    as_bf = lambda a: jnp.asarray(a, F32).astype(BF16)
    return as_bf(lower), as_bf(same), as_bf(expand), as_bf(headones)


HEAD_PAIR = LANES // GDN_HEAD_DIM
N_PAIRS = GROUP_W // LANES


def _halves(x):
    return [x[:, i * LANES:(i + 1) * LANES] for i in range(N_PAIRS)]


def _bd(xh, bdmask):
    xb = xh.astype(BF16)
    return jnp.where(bdmask, jnp.concatenate([xb] * HEAD_PAIR, axis=0), jnp.zeros((), BF16))


def _bd_mask():
    return (lax.broadcasted_iota(jnp.int32, (LANES, LANES), 0) // GDN_HEAD_DIM
            == lax.broadcasted_iota(jnp.int32, (LANES, LANES), 1) // GDN_HEAD_DIM)


def _wide_dot(x, ys, bdmask, transpose_rhs=False):
    parts = []
    for xh, yhs in zip(_halves(x.astype(BF16)), zip(*[_halves(y) for y in ys])):
        w = jnp.concatenate([_bd(yh, bdmask) for yh in yhs], axis=0 if transpose_rhs else 1)
        dims = (((1,), (1,)), ((), ())) if transpose_rhs else (((1,), (0,)), ((), ()))
        parts.append(lax.dot_general(xh, w, dims, preferred_element_type=F32))
    return jnp.concatenate([p[:, i * LANES:(i + 1) * LANES] for i in range(len(ys)) for p in parts], axis=1)


def _head_gram(a, bs):
    outs = [[] for _ in bs]
    for ah, bhs in zip(_halves(a.astype(BF16)), zip(*[_halves(b.astype(BF16)) for b in bs])):
        full = lax.dot_general(ah, jnp.concatenate(bhs, axis=1), (((0,), (0,)), ((), ())),
                               preferred_element_type=F32)
        head = (lax.broadcasted_iota(jnp.int32, (GDN_HEAD_DIM, full.shape[1]), 1) // GDN_HEAD_DIM) % HEAD_PAIR
        keep = None
        for h in range(HEAD_PAIR):
            blk = jnp.where(head == h, full[h * GDN_HEAD_DIM:(h + 1) * GDN_HEAD_DIM], 0.0)
            keep = blk if keep is None else keep + blk
        for i in range(len(bs)):
            outs[i].append(keep[:, i * LANES:(i + 1) * LANES])
    return jnp.concatenate([jnp.concatenate(o, axis=1) for o in outs], axis=1)


def _tri_inverse(ms, eye, level_masks, bdmask):
    ds = [eye - jnp.where(level_masks[0], m, 0.0) for m in ms]
    for mask in level_masks[1:]:
        es = [_wide_dot(jnp.where(mask, m, 0.0), [d], bdmask) for m, d in zip(ms, ds)]
        fs = [_wide_dot(d, [e], bdmask) for d, e in zip(ds, es)]
        ds = [d - f for d, f in zip(ds, fs)]
    return ds


def _gdn_prep_kernel(prev_ref, cur_ref, next_ref, ab_ref, cw_ref, alog_ref, dtb_ref,
                     tri_ref, same_ref, exp_ref, ones_ref,
                     pf_ref, qpf_ref, bmf_ref, o0f_ref, df_ref, pb_ref, qpb_ref, bmb_ref, o0b_ref, db_ref,
                     ext_ref, q_scr, k_scr, v_scr):
    tm = cur_ref.shape[1]
    halo = prev_ref.shape[1]
    top, bot = _edge_masked(prev_ref, next_ref)
    ext_ref[0:halo] = top
    ext_ref[halo:halo + tm] = cur_ref[0]
    ext_ref[halo + tm:] = bot
    c = GDN_C
    g0, grp = 0, tm
    ls = rs = [slice(ci * c, (ci + 1) * c) for ci in range(grp // c)]
    rows_g = slice(g0, g0 + grp)
    base = halo - GDN_CONV // 2
    for r0 in range(g0, g0 + grp, GDN_SUB):
        acc = jnp.zeros((GDN_SUB, 3 * GROUP_W), F32)
        for t in range(GDN_CONV):
            acc = acc + cw_ref[t:t + 1, :] * ext_ref[r0 + base + t:r0 + base + t + GDN_SUB, :]
        x = acc * jax.nn.sigmoid(acc)
        rows = slice(r0, r0 + GDN_SUB)
        q_scr[rows, :] = x[:, :GROUP_W]
        k_scr[rows, :] = x[:, GROUP_W:2 * GROUP_W]
        v_scr[rows, :] = x[:, 2 * GROUP_W:]
    qa, ka = q_scr[rows_g, :], k_scr[rows_g, :]
    q_scr[rows_g, :] = qa * lax.rsqrt(_dot01(qa * qa, ones_ref[...], 2) + EPS) * (GDN_HEAD_DIM ** -0.5)
    k_scr[rows_g, :] = ka * lax.rsqrt(_dot01(ka * ka, ones_ref[...], 2) + EPS)

    ab = ab_ref[0, rows_g, :]
    lane = lax.broadcasted_iota(jnp.int32, ab.shape, 1)
    g = -jnp.exp(alog_ref[...]) * _softplus(ab + dtb_ref[...])
    gate = jnp.where(lane < 2 * GDN_HEADS, g, jax.nn.sigmoid(ab))
    cums = [_dot01_left(tri_ref[...], gate[r], 2) for r in ls]
    lane_c = lax.broadcasted_iota(jnp.int32, (c, AB_PAD), 1)
    cum = jnp.concatenate([jnp.where(lane_c < GDN_HEADS, cm[:c], cm[c:]) for cm in cums], axis=0)
    wide = _dot01(jnp.where(lane < 2 * GDN_HEADS, cum, gate), exp_ref[...], 2)
    gcs = (wide[:, :GROUP_W], wide[:, GROUP_W:2 * GROUP_W])
    betas = (wide[:, 2 * GROUP_W:3 * GROUP_W], wide[:, 3 * GROUP_W:])

    ii = lax.broadcasted_iota(jnp.int32, (c, GROUP_W), 0)
    jj = lax.broadcasted_iota(jnp.int32, (c, GROUP_W), 1) & (c - 1)
    diag2 = jnp.concatenate([ii == jj] * 2, axis=1)
    rows_gc = [_dot01_left(same_ref[...], jnp.where(diag2, wide[r, :2 * GROUP_W], 0.0), 2) for r in ls]
    gams = ([jnp.where(ii >= jj, jnp.exp(jnp.minimum(gcs[0][r] - rg[:, :GROUP_W], 0.0)), 0.0)
             for r, rg in zip(ls, rows_gc)],
            [jnp.where(ii <= jj, jnp.exp(jnp.minimum(gcs[1][r] - rg[:, GROUP_W:], 0.0)), 0.0)
             for r, rg in zip(ls, rows_gc)])
    eye = (ii == jj).astype(F32)
    stricts = (ii > jj, ii < jj)
    levels = [((ii // (2 * s)) == (jj // (2 * s))) & ((ii // s) != (jj // s)) for s in (1, 2, 4, 8, 16, 32)]
    bdmask = _bd_mask()
    outs = ((pf_ref, qpf_ref, bmf_ref, o0f_ref, df_ref), (pb_ref, qpb_ref, bmb_ref, o0b_ref, db_ref))

    qs = [q_scr[r, :] for r in rs]
    ks = [k_scr[r, :] for r in rs]
    vs = [v_scr[r, :] for r in rs]
    kqs = [_wide_dot(jnp.concatenate([k, q], axis=0), [k], bdmask, transpose_rhs=True) for k, q in zip(ks, qs)]
    pairs = [(di, ci) for di in range(2) for ci in range(len(rs))]
    gc_l = [gcs[di][ls[ci]] for di, ci in pairs]
    gam_l = [gams[di][ci] for di, ci in pairs]
    beta_l = [betas[di][ls[ci]] for di, ci in pairs]
    ms = [jnp.where(stricts[di], beta * kqs[ci][:c] * gam, 0.0)
          for (di, ci), beta, gam in zip(pairs, beta_l, gam_l)]
    ts = _tri_inverse(ms, eye, levels, bdmask)
    egcs = [jnp.exp(gc) for gc in gc_l]
    uws = [_wide_dot(t, [vs[ci] * beta, ks[ci] * beta * egc], bdmask)
           for (_, ci), t, beta, egc in zip(pairs, ts, beta_l, egcs)]
    aqks = [kqs[ci][c:] * gam for (_, ci), gam in zip(pairs, gam_l)]
    ows = [_wide_dot(aqk, [uw[:, :GROUP_W], uw[:, GROUP_W:]], bdmask) for aqk, uw in zip(aqks, uws)]
    for n, (di, ci) in enumerate(pairs):
        p_ref, qp_ref, bm_ref, o0_ref, d_ref = outs[di]
        r, gc, uw, ow = rs[ci], gc_l[n], uws[n], ows[n]
        glast = gc[0:1] if di else gc[c - 1:c]
        kdec = ks[ci] * jnp.exp(glast - gc)
        bp = _head_gram(kdec, [uw[:, :GROUP_W], uw[:, GROUP_W:]])
        bm_ref[0, r, :] = bp[:, :GROUP_W]
        p_ref[0, r, :] = bp[:, GROUP_W:].astype(BF16)
        o0_ref[0, r, :] = ow[:, :GROUP_W]
        qp_ref[0, r, :] = (qs[ci] * egcs[n] - ow[:, GROUP_W:]).astype(BF16)
        d_ref[0, g0 // c + ci:g0 // c + ci + 1, :] = jnp.exp(glast)


def _gdn_prep(qkv, ab, conv_w, a_log, dt_bias, tm):
    b, t, w3 = qkv.shape
    halo = 8
    nck = tm // GDN_C
    prev, cur, nxt = _halo_specs(tm, halo, t, w3)
    pad = lambda a: jnp.pad(a.reshape(1, -1), ((0, 0), (0, AB_PAD - a.size)))
    consts = _gdn_consts()
    tok = pl.BlockSpec((1, tm, GROUP_W), lambda i, j: (i, j, 0))
    dec = pl.BlockSpec((1, nck, GROUP_W), lambda i, j: (i, j, 0))
    tok_shape = lambda dt: jax.ShapeDtypeStruct((b, t, GROUP_W), dt)
    dir_specs = [tok, tok, tok, tok, dec]
    dir_shapes = [tok_shape(BF16), tok_shape(BF16), tok_shape(F32), tok_shape(F32),
                  jax.ShapeDtypeStruct((b, t // GDN_C, GROUP_W), F32)]
    return pl.pallas_call(
        _gdn_prep_kernel,
        grid=(b, t // tm),
        in_specs=[prev, cur, nxt, pl.BlockSpec((1, tm, AB_PAD), lambda i, j: (i, j, 0)),
                  _resident(conv_w.shape), _resident((1, AB_PAD)), _resident((1, AB_PAD))]
        + [_resident(cst.shape) for cst in consts],
        out_specs=dir_specs * 2,
        out_shape=dir_shapes * 2,
        scratch_shapes=[pltpu.VMEM((tm + 2 * halo, w3), F32)] + [pltpu.VMEM((tm, GROUP_W), F32)] * 3,
        compiler_params=_params(2), name="gdn_prep",
    )(qkv, qkv, qkv, ab, conv_w, pad(a_log), pad(dt_bias), *consts)


def _gdn_scan_kernel(pf_ref, qpf_ref, bmf_ref, o0f_ref, df_ref, pb_ref, qpb_ref, bmb_ref, o0b_ref, db_ref,
                     s0f_ref, s0b_ref, of_ref, ob_ref, sf_ref, sb_ref):
    nb, tm = pf_ref.shape[0], pf_ref.shape[1]
    nc = tm // GDN_C

    @pl.when(pl.program_id(0) == 0)
    def _():
        sf_ref[...] = s0f_ref[...]
        sb_ref[...] = s0b_ref[...]

    bdmask = _bd_mask()
    fwd = (pf_ref, qpf_ref, bmf_ref, o0f_ref, df_ref, of_ref)
    bwd = (pb_ref, qpb_ref, bmb_ref, o0b_ref, db_ref, ob_ref)
    chains = [(fwd, bi, False) for bi in range(nb)] + [(bwd, bi, True) for bi in range(nb)]
    states = [sf_ref[bi] for bi in range(nb)] + [sb_ref[bi] for bi in range(nb)]
    for step in range(nc):
        lhs, rows = [], []
        for (p_ref, qp_ref, _, _, _, _), bi, rev in chains:
            ci = nc - 1 - step if rev else step
            r = slice(ci * GDN_C, (ci + 1) * GDN_C)
            rows.append((ci, r))
            lhs.append(jnp.concatenate([p_ref[bi, r, :], qp_ref[bi, r, :]], axis=0))
        res = [_wide_dot(a, [s], bdmask) for a, s in zip(lhs, states)]
        new_states = []
        for (_, _, bm_ref, o0_ref, d_ref, o_ref), bi, _ in chains:
            n = len(new_states)
            ci, r = rows[n]
            o_ref[bi, r, :] = o0_ref[bi, r, :] + res[n][GDN_C:]
            new_states.append(d_ref[bi, ci:ci + 1, :] * states[n] - res[n][:GDN_C] + bm_ref[bi, r, :])
        states = new_states
    for bi in range(nb):
        sf_ref[bi] = states[bi]
        sb_ref[bi] = states[nb + bi]


def _gdn_scan(prep, s0f, s0b, tm):
    b, t, w = prep[0].shape
    nj = t // tm
    nck = tm // GDN_C
    fwd = pl.BlockSpec((b, tm, w), lambda j: (0, j, 0))
    bwd = pl.BlockSpec((b, tm, w), lambda j: (0, nj - 1 - j, 0))
    dfwd = pl.BlockSpec((b, nck, w), lambda j: (0, j, 0))
    dbwd = pl.BlockSpec((b, nck, w), lambda j: (0, nj - 1 - j, 0))
    st = pl.BlockSpec((b, GDN_HEAD_DIM, w), lambda j: (0, 0, 0))
    st_shape = jax.ShapeDtypeStruct((b, GDN_HEAD_DIM, w), F32)
    return pl.pallas_call(
        _gdn_scan_kernel,
        grid=(nj,),
        in_specs=[fwd] * 4 + [dfwd] + [bwd] * 4 + [dbwd] + [st, st],
        out_specs=[fwd, bwd, st, st],
        out_shape=[jax.ShapeDtypeStruct((b, t, w), F32)] * 2 + [st_shape] * 2,
        compiler_params=pltpu.CompilerParams(dimension_semantics=("arbitrary",),
                                             vmem_limit_bytes=VMEM_LIMIT),
        name="gdn_scan",
    )(*prep, s0f, s0b)


def _mix_out_kernel(h_ref, yp_ref, yfa_ref, yfb_ref, yc_ref, of_ref, ob_ref, z_ref, gt_ref, nw_ref, ones_ref,
                    w_ref, o_ref):
    o = of_ref[0] + ob_ref[0]
    ms = _dot01(o * o, ones_ref[...], 2) * (1.0 / GDN_HEAD_DIM)
    z = z_ref[0]
    yg = o * lax.rsqrt(ms + EPS) * nw_ref[...] * (z * jax.nn.sigmoid(z))
    y = jnp.zeros(h_ref.shape[1:], F32)
    yf = jnp.concatenate([yfa_ref[0], yfb_ref[0]], axis=-1)
    for gi, part in enumerate((yp_ref[0], yf, yc_ref[0], yg)):
        y = y + jnp.dot(part.astype(BF16), w_ref[gi * GROUP_W:(gi + 1) * GROUP_W, :],
                        preferred_element_type=F32)
    o_ref[0] = h_ref[0] + gt_ref[0] * y


def _mix_out(h, ys, o_f, o_b, z, gate, gdn_nw, w_out, layer, tm):
    b, t, d = h.shape
    tok = pl.BlockSpec((1, tm, d), lambda i, j: (i, j, 0))
    grp = pl.BlockSpec((1, tm, GROUP_W), lambda i, j: (i, j, 0))
    half = pl.BlockSpec((1, tm, LANES), lambda i, j: (i, j, 0))
    ones = _gdn_consts()[3]
    return pl.pallas_call(
        _mix_out_kernel,
        grid=(b, t // tm),
        in_specs=[tok, grp, half, half] + [grp] * 4 + [pl.BlockSpec((1, 1, d), lambda i, j: (i, 0, 0)),
                                     _resident((1, GROUP_W)), _resident(ones.shape), _layer_resident(w_out, layer)],
        out_specs=tok,
        out_shape=jax.ShapeDtypeStruct(h.shape, F32),
        compiler_params=_params(2), name="mix_out",
    )(h, *ys, o_f, o_b, z, gate, gdn_nw, ones, w_out)


MOD_ROWS = 8
MOD_TN = 1152


def _mod_kernel(c_ref, w_ref, b_ref, o_ref):
    cv = c_ref[...]
    a = (cv * jax.nn.sigmoid(cv)).astype(BF16)
    o_ref[0] = jnp.dot(a, w_ref[0].astype(BF16), preferred_element_type=F32) + b_ref[0]


def _modulation(c, c_ctx, mod_w, mod_b):
    nl, d, n = mod_w.shape
    cond = jnp.concatenate([c, c_ctx[None, :]], axis=0)
    cond = jnp.pad(cond, ((0, MOD_ROWS - cond.shape[0]), (0, 0)))
    return pl.pallas_call(
        _mod_kernel,
        grid=(nl, n // MOD_TN),
        in_specs=[_resident(cond.shape), pl.BlockSpec((1, d, MOD_TN), lambda i, j: (i, 0, j)),
                  pl.BlockSpec((1, 1, MOD_TN), lambda i, j: (i, 0, j))],
        out_specs=pl.BlockSpec((1, MOD_ROWS, MOD_TN), lambda i, j: (i, 0, j)),
        out_shape=jax.ShapeDtypeStruct((nl, MOD_ROWS, n), F32),
        compiler_params=_params(2), name="modulation",
    )(cond, mod_w, mod_b[:, None, :])


def _token_mix(p, gdn_state, grid_rows, tm, wl, need_out):
    p_pool, p_four, p_conv, p_qkv, p_z, p_ab = p
    prep = _gdn_prep(p_qkv, p_ab, wl["gdn_conv_w"], wl["gdn_a_log"], wl["gdn_dt_bias"], tm)
    o_f, o_b, s_f, s_b = _gdn_scan(prep, *gdn_state, tm)
    if not need_out:
        return None, (s_f, s_b)
    ys = (_pool_mix(p_pool, wl["pool_wbd"], wl["pool_scale"], grid_rows),
          *_fourier_mix(p_four, wl["fourier_w"]),
          _conv_mix(p_conv, wl["conv_dw_w"], wl["conv_dw_b"], wl["conv_ln_g"], wl["conv_ln_b"],
                    wl["conv_pw_w"], tm))
    return (ys, o_f, o_b, p_z), (s_f, s_b)


def kernel(x, c, ctx, c_ctx, mod_w, mod_b, norm_w, ffn1_wg, ffn1_wu, ffn1_wd, ffn2_wg, ffn2_wu,
           ffn2_wd, w_in, w_out, pool_w, pool_scale, fourier_w, conv_dw_w, conv_dw_b, conv_ln_g,
           conv_ln_b, conv_pw_w, gdn_conv_w, gdn_a_log, gdn_dt_bias, gdn_norm_w, final_norm_w):
    bsz, seq, d = x.shape
    n_ctx = ctx.shape[1]
    rows = seq // GRID_W
    tm_x, tm_c = 512, n_ctx
    hx, hc = x, ctx
    mods = _modulation(c, c_ctx, mod_w, mod_b)
    zero_state = (jnp.zeros((bsz, GDN_HEAD_DIM, GROUP_W), F32),) * 2
    f1 = tuple(_cast_bf16(w) for w in (ffn1_wg, ffn1_wu, ffn1_wd))
    f2 = tuple(_cast_bf16(w) for w in (ffn2_wg, ffn2_wu, ffn2_wd))
    w_main = _cast_bf16(w_in, cols=2 * GDN_OFF)
    w_ab = jnp.pad(w_in[:, :, 2 * GDN_OFF:], ((0, 0), (0, 0), (0, AB_PAD - N_GATE))).astype(BF16)
    w_out_b = _cast_bf16(w_out)
    for l in range(DEPTH):
        last = l == DEPTH - 1
        mx = [m[:, None, :] for m in jnp.split(mods[l, :bsz], N_MOD, axis=-1)]
        mc = [jnp.broadcast_to(m[None], (bsz, 1, d)) for m in jnp.split(mods[l, bsz:bsz + 1], N_MOD, axis=-1)]
        nw = norm_w[l][:, None, :]
        row = lambda a: a.reshape(1, -1)
        wl = dict(pool_wbd=_blockdiag(pool_w[l]).astype(BF16), pool_scale=row(pool_scale[l]),
                  fourier_w=fourier_w[l].astype(BF16), conv_dw_w=conv_dw_w[l], conv_dw_b=row(conv_dw_b[l]),
                  conv_ln_g=row(conv_ln_g[l]), conv_ln_b=row(conv_ln_b[l]),
                  conv_pw_w=conv_pw_w[l].astype(BF16), gdn_conv_w=gdn_conv_w[l],
                  gdn_a_log=gdn_a_log[l], gdn_dt_bias=gdn_dt_bias[l])
        gdn_nw = row(jnp.tile(gdn_norm_w[l], GDN_HEADS))

        hx = _ffn(hx, nw[0], mx[0], mx[1], mx[2], f1, l, tm_x)
        hc = _ffn(hc, nw[0], mc[0], mc[1], mc[2], f1, l, tm_c)

        px = _inproj(hx, nw[1], mx[3], mx[4], w_main, w_ab, l, False, tm_x)
        pc = _inproj(hc, nw[1], mc[3], mc[4], w_main, w_ab, l, last, tm_c)
        if last:
            pc = (None,) * 3 + tuple(pc)

        mix_c, ctx_state = _token_mix(pc, zero_state, None, tm_c, wl, not last)
        mix_x, _ = _token_mix(px, ctx_state, rows, tm_x, wl, True)
        hx = _mix_out(hx, *mix_x, mx[5], gdn_nw, w_out_b, l, tm_x)
        hx = _ffn(hx, nw[2], mx[6], mx[7], mx[8], f2, l, tm_x, final_norm_w[None, :] if last else None)
        if not last:
            hc = _mix_out(hc, *mix_c, mc[5], gdn_nw, w_out_b, l, tm_c)
            hc = _ffn(hc, nw[2], mc[6], mc[7], mc[8], f2, l, tm_c)
    return hx
```

```python
import functools
import math

import jax
import jax.numpy as jnp
import numpy as np
from jax import lax
from jax.experimental import pallas as pl
from jax.experimental.pallas import tpu as pltpu

D_MODEL = 1024
DEPTH = 4
GRID_W = 64
N_MIXERS = 4
GROUP_W = D_MODEL // N_MIXERS
POOL_WINDOWS = (2, 4, 8, 16)
POOL_GROUPS = 4
POOL_GW = GROUP_W // POOL_GROUPS
FOURIER_GROUPS = 4
FOURIER_GW = GROUP_W // FOURIER_GROUPS
CONV_K = 31
GDN_HEAD_DIM = 64
GDN_HEADS = GROUP_W // GDN_HEAD_DIM
GDN_CONV = 3
GDN_CHUNK = 64
FFN_HIDDEN = 128 * ((8 * D_MODEL // 3 + 127) // 128)
N_MOD = 9
EPS = 1e-6
POOL_OFF = 0
FOURIER_OFF = POOL_OFF + GROUP_W
CONV_OFF = FOURIER_OFF + GROUP_W
GDN_OFF = CONV_OFF + 2 * GROUP_W

LANES = 128
SUBLANES = 8
VMEM_LIMIT = 56 * 1024 * 1024
FFN_CHUNK = 256
AB_PAD = LANES

BF16 = jnp.bfloat16
F32 = jnp.float32


def _params(n_axes):
    return pltpu.CompilerParams(dimension_semantics=("parallel",) * n_axes,
                                vmem_limit_bytes=VMEM_LIMIT)


def _resident(shape):
    nd = len(shape)
    return pl.BlockSpec(shape, lambda *_: (0,) * nd, pipeline_mode=pl.Buffered(1))


def _layer_resident(stack, layer, cols=None, col_block=0):
    _, r, c = stack.shape
    return pl.BlockSpec((None, r, cols or c), lambda *_: (layer, 0, col_block), pipeline_mode=pl.Buffered(1))


def _cast_kernel(x_ref, o_ref):
    o_ref[...] = x_ref[...].astype(BF16)


def _cast_bf16(w, cols=None, rows_per_step=256):
    nl, r, c = w.shape
    cols = cols or c
    spec = pl.BlockSpec((1, rows_per_step, cols), lambda i, j: (i, j, 0))
    return pl.pallas_call(
        _cast_kernel, grid=(nl, r // rows_per_step), in_specs=[spec], out_specs=spec,
        out_shape=jax.ShapeDtypeStruct((nl, r, cols), BF16), compiler_params=_params(2), name="cast_bf16",
    )(w)


def _rms_mod(h, nw, shift, scale):
    ms = jnp.mean(h * h, axis=-1, keepdims=True)
    n = h * lax.rsqrt(ms + EPS) * nw
    return n * (1.0 + scale) + shift


def _ffn_apply(h, nw_ref, sh_ref, sc_ref, gt_ref, wg_ref, wu_ref, wd_ref, final_nw_ref=None):
    nb = _rms_mod(h, nw_ref[...], sh_ref[0], sc_ref[0]).astype(BF16)
    acc = jnp.zeros(h.shape, F32)
    for f0 in range(0, FFN_HIDDEN, FFN_CHUNK):
        g = jnp.dot(nb, wg_ref[:, f0:f0 + FFN_CHUNK], preferred_element_type=F32)
        u = jnp.dot(nb, wu_ref[:, f0:f0 + FFN_CHUNK], preferred_element_type=F32)
        a = (g * jax.nn.sigmoid(g) * u).astype(BF16)
        acc = acc + jnp.dot(a, wd_ref[f0:f0 + FFN_CHUNK, :], preferred_element_type=F32)
    out = h + (0.5 * gt_ref[0]) * acc
    if final_nw_ref is not None:
        out = out * lax.rsqrt(jnp.mean(out * out, axis=-1, keepdims=True) + EPS) * final_nw_ref[...]
    return out


def _ffn_kernel(h_ref, *rest):
    rest[-1][0] = _ffn_apply(h_ref[0], *rest[:-1])


def _ffn(h, nw, shift, scale, gate, weights, layer, tm, final_nw=None):
    b, t, d = h.shape
    vec = pl.BlockSpec((1, 1, d), lambda i, j: (i, 0, 0))
    tok = pl.BlockSpec((1, tm, d), lambda i, j: (i, j, 0))
    extra = [] if final_nw is None else [final_nw]
    return pl.pallas_call(
        _ffn_kernel,
        grid=(b, t // tm),
        in_specs=[tok, _resident((1, d)), vec, vec, vec] + [_layer_resident(w, layer) for w in weights]
        + [_resident((1, d))] * len(extra),
        out_specs=tok,
        out_shape=jax.ShapeDtypeStruct(h.shape, F32),
        compiler_params=_params(2),
        name="ffn",
    )(h, nw, shift, scale, gate, *weights, *extra)


IN_SPLITS = (GROUP_W, GROUP_W, 2 * GROUP_W, 3 * GROUP_W, GROUP_W, AB_PAD)


def _inproj_kernel(h_ref, nw_ref, sh_ref, sc_ref, w_ref, wab_ref, *o_refs):
    tm = h_ref.shape[1]
    half = tm // 2 if tm % (2 * SUBLANES) == 0 else tm
    for r0 in range(0, tm, half):
        rows = slice(r0, r0 + half)
        nb = _rms_mod(h_ref[0, rows, :], nw_ref[...], sh_ref[0], sc_ref[0]).astype(BF16)
        off = 0
        for o_ref in o_refs[:-1]:
            wdt = o_ref.shape[-1]
            o_ref[0, rows, :] = jnp.dot(nb, w_ref[:, off:off + wdt], preferred_element_type=F32)
            off += wdt
        o_refs[-1][0, rows, :] = jnp.dot(nb, wab_ref[...], preferred_element_type=F32)


def _inproj(h, nw, shift, scale, w_main, w_ab, layer, gdn_only, tm):
    b, t, d = h.shape
    vec = pl.BlockSpec((1, 1, d), lambda i, j: (i, 0, 0))
    tok = pl.BlockSpec((1, tm, d), lambda i, j: (i, j, 0))
    splits = IN_SPLITS[3:] if gdn_only else IN_SPLITS
    w_spec = (_layer_resident(w_main, layer, GDN_OFF, 1) if gdn_only else _layer_resident(w_main, layer))
    return pl.pallas_call(
        _inproj_kernel,
        grid=(b, t // tm),
        in_specs=[tok, _resident((1, d)), vec, vec, w_spec, _layer_resident(w_ab, layer)],
        out_specs=[pl.BlockSpec((1, tm, s), lambda i, j: (i, j, 0)) for s in splits],
        out_shape=[jax.ShapeDtypeStruct((b, t, s), F32) for s in splits],
        compiler_params=_params(2),
        name="inproj",
    )(h, nw, shift, scale, w_main, w_ab)


def _split_bf16(x, parts):
    out = []
    for _ in range(parts - 1):
        p = x.astype(BF16)
        out.append(p)
        x = x - p.astype(F32)
    return out + [x.astype(BF16)]


def _dot01(x, w01, parts=3):
    return sum(jnp.dot(p, w01, preferred_element_type=F32) for p in _split_bf16(x, parts))


def _dot01_left(w01, x, parts=3):
    return sum(jnp.dot(w01, p, preferred_element_type=F32) for p in _split_bf16(x, parts))


def _blockdiag(w):
    g, a, b = w.shape
    return jnp.einsum('gab,gh->gahb', w, jnp.eye(g, dtype=w.dtype)).reshape(g * a, g * b)


def _halo_specs(tm, halo, t, width):
    r = tm // halo
    last = t // halo - 1
    prev = pl.BlockSpec((1, halo, width), lambda i, j: (i, jnp.maximum(j * r - 1, 0), 0))
    cur = pl.BlockSpec((1, tm, width), lambda i, j: (i, j, 0))
    nxt = pl.BlockSpec((1, halo, width), lambda i, j: (i, jnp.minimum((j + 1) * r, last), 0))
    return prev, cur, nxt


def _edge_masked(prev_ref, next_ref):
    j, nj = pl.program_id(1), pl.num_programs(1)
    top = jnp.where(j > 0, prev_ref[0], 0.0)
    bot = jnp.where(j < nj - 1, next_ref[0], 0.0)
    return top, bot


POOL_TM = 1024
POOL_HALO = 512


def _pool_window_sums_1d(u, row_len):
    n = u.shape[0]
    col = lax.broadcasted_iota(jnp.int32, u.shape, 0) & (row_len - 1)
    grp = lax.broadcasted_iota(jnp.int32, u.shape, 1) // POOL_GW

    def back(x, s):
        return jnp.where(col >= s, pltpu.roll(x, s, 0), 0.0)

    def fwd(x, s):
        return jnp.where(col < row_len - s, pltpu.roll(x, n - s, 0), 0.0)

    b = back(u, 1)
    f = u
    out = b + f
    for gi in range(1, len(POOL_WINDOWS)):
        s = POOL_WINDOWS[gi] // 4
        b = b + back(b, s)
        f = f + fwd(f, s)
        out = jnp.where(grp >= gi, b + f, out)
    return out


def _pool_counts(idx, extent, halfw):
    return jnp.minimum(idx + halfw, extent) - jnp.maximum(idx - halfw, 0)


def _pool_finish(mean, u, wbd_ref, scale_ref, o_ref):
    d = (mean - u).astype(BF16)
    o_ref[0] = (jnp.dot(d, wbd_ref[...], preferred_element_type=F32) * scale_ref[...]).astype(BF16)


def _pool_grid_kernel(prev_ref, cur_ref, next_ref, wbd_ref, scale_ref, o_ref, *, n_rows):
    tm = cur_ref.shape[1]
    top, bot = _edge_masked(prev_ref, next_ref)
    u = cur_ref[0]
    grp = lax.broadcasted_iota(jnp.int32, (tm, GROUP_W), 1) // POOL_GW
    arr, off = jnp.concatenate([top, u, bot], axis=0), 0
    tot = None
    for gi, w in enumerate(POOL_WINDOWS):
        sh = GRID_W * max(w // 4, 1) if gi else GRID_W
        if gi == 0:
            arr = arr[:-sh] + arr[sh:]
            off = sh
        else:
            arr = arr[:-2 * sh] + arr[2 * sh:]
            off = off + sh
        centre = arr[POOL_HALO - off:POOL_HALO - off + tm]
        tot = centre if tot is None else jnp.where(grp >= gi, centre, tot)
    tot = _pool_window_sums_1d(tot, GRID_W)
    tok = pl.program_id(1) * tm + lax.broadcasted_iota(jnp.int32, (tm, GROUP_W), 0)
    halfw = jnp.left_shift(1, grp)
    cnt = (_pool_counts(tok // GRID_W, n_rows, halfw) * _pool_counts(tok & (GRID_W - 1), GRID_W, halfw))
    _pool_finish(tot / cnt.astype(F32), u, wbd_ref, scale_ref, o_ref)


def _pool_seq_kernel(u_ref, wbd_ref, scale_ref, o_ref):
    u = u_ref[0]
    n = u.shape[0]
    tot = _pool_window_sums_1d(u, n)
    grp = lax.broadcasted_iota(jnp.int32, u.shape, 1) // POOL_GW
    tok = lax.broadcasted_iota(jnp.int32, u.shape, 0)
    cnt = _pool_counts(tok, n, jnp.left_shift(1, grp))
    _pool_finish(tot / cnt.astype(F32), u, wbd_ref, scale_ref, o_ref)


def _pool_mix(u, wbd, scale, grid_rows):
    b, t, w = u.shape
    out_shape = jax.ShapeDtypeStruct(u.shape, BF16)
    if grid_rows is None:
        tok = pl.BlockSpec((1, t, w), lambda i: (i, 0, 0))
        return pl.pallas_call(
            _pool_seq_kernel, grid=(b,),
            in_specs=[tok, _resident(wbd.shape), _resident(scale.shape)],
            out_specs=tok, out_shape=out_shape, compiler_params=_params(1), name="pool_seq",
        )(u, wbd, scale)
    prev, cur, nxt = _halo_specs(POOL_TM, POOL_HALO, t, w)
    return pl.pallas_call(
        functools.partial(_pool_grid_kernel, n_rows=grid_rows),
        grid=(b, t // POOL_TM),
        in_specs=[prev, cur, nxt, _resident(wbd.shape), _resident(scale.shape)],
        out_specs=cur, out_shape=out_shape, compiler_params=_params(2), name="pool_grid",
    )(u, u, u, wbd, scale)


CONV_TM = 512
CONV_HALO = 16
CONV_SUB = 64


def _glu(x):
    return x[:, :GROUP_W] * jax.nn.sigmoid(x[:, GROUP_W:])


def _conv_kernel(prev_ref, cur_ref, next_ref, dww_ref, dwb_ref, lng_ref, lnb_ref, pw_ref, o_ref,
                 ext_ref, sh_ref):
    tm = cur_ref.shape[1]
    top, bot = _edge_masked(prev_ref, next_ref)
    ext_ref[0:CONV_HALO] = _glu(top)
    ext_ref[CONV_HALO:CONV_HALO + tm] = _glu(cur_ref[0])
    ext_ref[CONV_HALO + tm:] = _glu(bot)
    n_sh = sh_ref.shape[1]
    for s in range(SUBLANES):
        sh_ref[s] = ext_ref[s:s + n_sh, :]
    base = CONV_HALO - CONV_K // 2
    for r0 in range(0, tm, CONV_SUB):
        acc = jnp.zeros((CONV_SUB, GROUP_W), F32)
        for k in range(CONV_K):
            a, s = divmod(base + k, SUBLANES)
            acc = acc + dww_ref[k:k + 1, :] * sh_ref[s, r0 + a * SUBLANES:r0 + a * SUBLANES + CONV_SUB, :]
        h = acc + dwb_ref[...]
        mu = jnp.mean(h, axis=-1, keepdims=True)
        var = jnp.mean(jnp.square(h - mu), axis=-1, keepdims=True)
        h = (h - mu) * lax.rsqrt(var + EPS) * lng_ref[...] + lnb_ref[...]
        h = (h * jax.nn.sigmoid(h)).astype(BF16)
        o_ref[0, r0:r0 + CONV_SUB, :] = jnp.dot(h, pw_ref[...], preferred_element_type=F32).astype(BF16)


def _conv_mix(u2, dw_w, dw_b, ln_g, ln_b, pw_w, tm):
    b, t, w2 = u2.shape
    prev, cur, nxt = _halo_specs(tm, CONV_HALO, t, w2)
    row = _resident((1, GROUP_W))
    return pl.pallas_call(
        _conv_kernel,
        grid=(b, t // tm),
        in_specs=[prev, cur, nxt, _resident(dw_w.shape), row, row, row, _resident(pw_w.shape)],
        out_specs=pl.BlockSpec((1, tm, GROUP_W), lambda i, j: (i, j, 0)),
        out_shape=jax.ShapeDtypeStruct((b, t, GROUP_W), BF16),
        scratch_shapes=[pltpu.VMEM((tm + 2 * CONV_HALO, GROUP_W), F32),
                        pltpu.VMEM((SUBLANES, tm + 2 * CONV_HALO - SUBLANES, GROUP_W), F32)],
        compiler_params=_params(2), name="conv_mix",
    )(u2, u2, u2, dw_w, dw_b, ln_g, ln_b, pw_w)


FOUR_N1 = 128
FOUR_TT = 8
FOUR_KT = 8


def _hilo(w):
    w = jnp.asarray(w, F32)
    hi = w.astype(BF16)
    return hi, (w - hi.astype(F32)).astype(BF16)


def _dot_hp(x, wh, wl):
    xh = x.astype(BF16)
    xl = (x - xh.astype(F32)).astype(BF16)
    return (jnp.dot(xh, wh, preferred_element_type=F32) + jnp.dot(xh, wl, preferred_element_type=F32)
            + jnp.dot(xl, wh, preferred_element_type=F32))


def _dot_hp_left(wh, wl, x):
    xh = x.astype(BF16)
    xl = (x - xh.astype(F32)).astype(BF16)
    return (jnp.dot(wh, xh, preferred_element_type=F32) + jnp.dot(wl, xh, preferred_element_type=F32)
            + jnp.dot(wh, xl, preferred_element_type=F32))


def _cos_sin(n, rows=None, cols=None):
    r = np.arange(n if rows is None else rows, dtype=np.int64)
    c = np.arange(n if cols is None else cols, dtype=np.int64)
    ang = 2.0 * np.pi * ((np.outer(r, c) % n).astype(np.float64) / n)
    return np.cos(ang), np.sin(ang)


def _channel_dft():
    c, s = _cos_sin(FOURIER_GW)
    eye = np.eye(FOURIER_GROUPS)
    return np.concatenate([np.kron(eye, c), -np.kron(eye, s)], axis=1)


def _fourier_a_kernel(ua_ref, ub_ref, cdh_ref, cdl_ref, f1h_ref, f1l_ref, twc_ref, tws_ref, o_ref):
    x = jnp.concatenate([jnp.concatenate([ua_ref[:, j, :], ub_ref[:, j, :]], axis=1) for j in range(FOUR_TT)],
                        axis=0)
    z_all = _dot_hp(x, cdh_ref[...], cdl_ref[...])
    for j in range(FOUR_TT):
        z = z_all[j * FOUR_N1:(j + 1) * FOUR_N1]
        zz = jnp.concatenate([z[:, :GROUP_W], z[:, GROUP_W:]], axis=0)
        a = _dot_hp_left(f1h_ref[...], f1l_ref[...], zz)
        ar, ai = a[:FOUR_N1], a[FOUR_N1:]
        c = jnp.concatenate([twc_ref[j]] * (GROUP_W // LANES), axis=1)
        s = jnp.concatenate([tws_ref[j]] * (GROUP_W // LANES), axis=1)
        o_ref[0, 0, j] = ar * c + ai * s
        o_ref[0, 1, j] = ai * c - ar * s


def _fourier_b_kernel(ga_ref, gb_ref, f2h_ref, f2l_ref, fw_ref, oa_ref, ob_ref, *, scale):
    fs = []
    for kk in range(FOUR_KT):
        gm = jnp.concatenate([ga_ref[:, :, kk, :], gb_ref[:, :, kk, :]], axis=-1)
        gm = gm.reshape(2 * FOUR_N1, GROUP_W)
        fs.append((_dot_hp_left(f2h_ref[...], f2l_ref[...], gm) * scale).astype(BF16))
    y = jnp.dot(jnp.concatenate(fs, axis=0), fw_ref[...], preferred_element_type=F32)
    for kk in range(FOUR_KT):
        yk = y[kk * FOUR_N1:(kk + 1) * FOUR_N1]
        oa_ref[:, kk, :] = yk[:, :LANES]
        ob_ref[:, kk, :] = yk[:, LANES:]


def _fourier_small_kernel(u_ref, cdh_ref, cdl_ref, fh_ref, fl_ref, fw_ref, oa_ref, ob_ref, *, scale):
    z = _dot_hp(u_ref[0], cdh_ref[...], cdl_ref[...])
    zz = jnp.concatenate([z[:, :GROUP_W], z[:, GROUP_W:]], axis=0)
    f = _dot_hp_left(fh_ref[...], fl_ref[...], zz) * scale
    y = jnp.dot(f.astype(BF16), fw_ref[...], preferred_element_type=F32)
    oa_ref[0] = y[:, :LANES]
    ob_ref[0] = y[:, LANES:]


def _fourier_mix(u, fw):
    b, n, w = u.shape
    scale = 1.0 / math.sqrt(n * FOURIER_GW)
    cdh, cdl = _hilo(_channel_dft())
    if n != FOUR_N1 * FOUR_N1:
        c, s = _cos_sin(n)
        fh, fl = _hilo(np.concatenate([c, s], axis=1))
        tok = pl.BlockSpec((1, n, w), lambda i: (i, 0, 0))
        half = pl.BlockSpec((1, n, LANES), lambda i: (i, 0, 0))
        return pl.pallas_call(
            functools.partial(_fourier_small_kernel, scale=scale), grid=(b,),
            in_specs=[tok] + [_resident(a.shape) for a in (cdh, cdl, fh, fl, fw)],
            out_specs=[half, half], out_shape=[jax.ShapeDtypeStruct((b, n, LANES), F32)] * 2,
            compiler_params=_params(1), name="fourier_small",
        )(u, cdh, cdl, fh, fl, fw)
    n1 = FOUR_N1
    c1, s1 = _cos_sin(n1)
    f1h, f1l = _hilo(np.block([[c1, s1], [-s1, c1]]))
    f2h, f2l = _hilo(np.concatenate([c1, s1], axis=1))
    twc, tws = _cos_sin(n, rows=n1, cols=n1)
    twc = jnp.broadcast_to(jnp.asarray(twc, F32)[:, :, None], (n1, n1, LANES))
    tws = jnp.broadcast_to(jnp.asarray(tws, F32)[:, :, None], (n1, n1, LANES))
    tw_spec = pl.BlockSpec((FOUR_TT, n1, LANES), lambda i, j: (j, 0, 0))
    u4 = u.reshape(b, n1, n1, w)
    g = pl.pallas_call(
        _fourier_a_kernel, grid=(b, n1 // FOUR_TT),
        in_specs=[pl.BlockSpec((None, n1, FOUR_TT, LANES), lambda i, j: (i, 0, j, 0)),
                  pl.BlockSpec((None, n1, FOUR_TT, LANES), lambda i, j: (i, 0, j, 1))]
        + [_resident(a.shape) for a in (cdh, cdl, f1h, f1l)] + [tw_spec, tw_spec],
        out_specs=pl.BlockSpec((1, 2, FOUR_TT, n1, w), lambda i, j: (i, 0, j, 0, 0)),
        out_shape=jax.ShapeDtypeStruct((b, 2, n1, n1, w), F32),
        compiler_params=_params(2), name="fourier_a",
    )(u4, u4, cdh, cdl, f1h, f1l, twc, tws)
    half_out = pl.BlockSpec((None, n1, FOUR_KT, LANES), lambda i, j: (i, 0, j, 0))
    ya, yb = pl.pallas_call(
        functools.partial(_fourier_b_kernel, scale=scale), grid=(b, n1 // FOUR_KT),
        in_specs=[pl.BlockSpec((None, 2, n1, FOUR_KT, LANES), lambda i, j: (i, 0, 0, j, 0)),
                  pl.BlockSpec((None, 2, n1, FOUR_KT, LANES), lambda i, j: (i, 0, 0, j, 1))]
        + [_resident(a.shape) for a in (f2h, f2l, fw)],
        out_specs=[half_out, half_out],
        out_shape=[jax.ShapeDtypeStruct((b, n1, n1, LANES), F32)] * 2,
        compiler_params=_params(2), name="fourier_b",
    )(g, g, f2h, f2l, fw)
    return ya.reshape(b, n, LANES), yb.reshape(b, n, LANES)


GDN_C = GDN_CHUNK
GDN_PREP_TM = 512
GDN_SCAN_TM = 512
GDN_SUB = 64
N_GATE = 4 * GDN_HEADS


def _softplus(x):
    return jnp.maximum(x, 0.0) + jnp.log1p(jnp.exp(-jnp.abs(x)))


def _gdn_consts():
    i = np.arange(GDN_C)
    lower = np.concatenate([i[None, :] <= i[:, None], i[None, :] >= i[:, None]], axis=0)
    same = np.ones((GDN_C, GDN_C), bool)
    expand = np.zeros((AB_PAD, 4 * GROUP_W), np.float32)
    for s in range(4):
        for h in range(GDN_HEADS):
            c0 = s * GROUP_W + h * GDN_HEAD_DIM
            expand[s * GDN_HEADS + h, c0:c0 + GDN_HEAD_DIM] = 1.0
    lane = np.arange(GROUP_W)
    headones = (lane[:, None] // GDN_HEAD_DIM) == (lane[None, :] // GDN_HEAD_DIM)
    as_bf = lambda a: jnp.asarray(a, F32).astype(BF16)
    return as_bf(lower), as_bf(same), as_bf(expand), as_bf(headones)


HEAD_PAIR = LANES // GDN_HEAD_DIM
N_PAIRS = GROUP_W // LANES


def _halves(x):
    return [x[:, i * LANES:(i + 1) * LANES] for i in range(N_PAIRS)]


def _bd(xh, bdmask):
    xb = xh.astype(BF16)
    return jnp.where(bdmask, jnp.concatenate([xb] * HEAD_PAIR, axis=0), jnp.zeros((), BF16))


def _bd_mask():
    return (lax.broadcasted_iota(jnp.int32, (LANES, LANES), 0) // GDN_HEAD_DIM
            == lax.broadcasted_iota(jnp.int32, (LANES, LANES), 1) // GDN_HEAD_DIM)


def _wide_dot(x, ys, bdmask, transpose_rhs=False):
    parts = []
    for xh, yhs in zip(_halves(x.astype(BF16)), zip(*[_halves(y) for y in ys])):
        w = jnp.concatenate([_bd(yh, bdmask) for yh in yhs], axis=0 if transpose_rhs else 1)
        dims = (((1,), (1,)), ((), ())) if transpose_rhs else (((1,), (0,)), ((), ()))
        parts.append(lax.dot_general(xh, w, dims, preferred_element_type=F32))
    return jnp.concatenate([p[:, i * LANES:(i + 1) * LANES] for i in range(len(ys)) for p in parts], axis=1)


def _head_gram(a, bs):
    outs = [[] for _ in bs]
    for ah, bhs in zip(_halves(a.astype(BF16)), zip(*[_halves(b.astype(BF16)) for b in bs])):
        full = lax.dot_general(ah, jnp.concatenate(bhs, axis=1), (((0,), (0,)), ((), ())),
                               preferred_element_type=F32)
        head = (lax.broadcasted_iota(jnp.int32, (GDN_HEAD_DIM, full.shape[1]), 1) // GDN_HEAD_DIM) % HEAD_PAIR
        keep = None
        for h in range(HEAD_PAIR):
            blk = jnp.where(head == h, full[h * GDN_HEAD_DIM:(h + 1) * GDN_HEAD_DIM], 0.0)
            keep = blk if keep is None else keep + blk
        for i in range(len(bs)):
            outs[i].append(keep[:, i * LANES:(i + 1) * LANES])
    return jnp.concatenate([jnp.concatenate(o, axis=1) for o in outs], axis=1)


def _tri_inverse(ms, eye, level_masks, bdmask):
    ds = [eye - jnp.where(level_masks[0], m, 0.0) for m in ms]
    for mask in level_masks[1:]:
        es = [_wide_dot(jnp.where(mask, m, 0.0), [d], bdmask) for m, d in zip(ms, ds)]
        fs = [_wide_dot(d, [e], bdmask) for d, e in zip(ds, es)]
        ds = [d - f for d, f in zip(ds, fs)]
    return ds


def _gdn_prep_kernel(prev_ref, cur_ref, next_ref, ab_ref, cw_ref, alog_ref, dtb_ref,
                     tri_ref, same_ref, exp_ref, ones_ref,
                     pf_ref, qpf_ref, bmf_ref, o0f_ref, df_ref, pb_ref, qpb_ref, bmb_ref, o0b_ref, db_ref,
                     ext_ref, q_scr, k_scr, v_scr):
    tm = cur_ref.shape[1]
    halo = prev_ref.shape[1]
    top, bot = _edge_masked(prev_ref, next_ref)
    ext_ref[0:halo] = top
    ext_ref[halo:halo + tm] = cur_ref[0]
    ext_ref[halo + tm:] = bot
    c = GDN_C
    g0, grp = 0, tm
    ls = rs = [slice(ci * c, (ci + 1) * c) for ci in range(grp // c)]
    rows_g = slice(g0, g0 + grp)
    base = halo - GDN_CONV // 2
    for r0 in range(g0, g0 + grp, GDN_SUB):
        acc = jnp.zeros((GDN_SUB, 3 * GROUP_W), F32)
        for t in range(GDN_CONV):
            acc = acc + cw_ref[t:t + 1, :] * ext_ref[r0 + base + t:r0 + base + t + GDN_SUB, :]
        x = acc * jax.nn.sigmoid(acc)
        rows = slice(r0, r0 + GDN_SUB)
        q_scr[rows, :] = x[:, :GROUP_W]
        k_scr[rows, :] = x[:, GROUP_W:2 * GROUP_W]
        v_scr[rows, :] = x[:, 2 * GROUP_W:]
    qa, ka = q_scr[rows_g, :], k_scr[rows_g, :]
    q_scr[rows_g, :] = qa * lax.rsqrt(_dot01(qa * qa, ones_ref[...], 2) + EPS) * (GDN_HEAD_DIM ** -0.5)
    k_scr[rows_g, :] = ka * lax.rsqrt(_dot01(ka * ka, ones_ref[...], 2) + EPS)

    ab = ab_ref[0, rows_g, :]
    lane = lax.broadcasted_iota(jnp.int32, ab.shape, 1)
    g = -jnp.exp(alog_ref[...]) * _softplus(ab + dtb_ref[...])
    gate = jnp.where(lane < 2 * GDN_HEADS, g, jax.nn.sigmoid(ab))
    cums = [_dot01_left(tri_ref[...], gate[r], 2) for r in ls]
    lane_c = lax.broadcasted_iota(jnp.int32, (c, AB_PAD), 1)
    cum = jnp.concatenate([jnp.where(lane_c < GDN_HEADS, cm[:c], cm[c:]) for cm in cums], axis=0)
    wide = _dot01(jnp.where(lane < 2 * GDN_HEADS, cum, gate), exp_ref[...], 2)
    gcs = (wide[:, :GROUP_W], wide[:, GROUP_W:2 * GROUP_W])
    betas = (wide[:, 2 * GROUP_W:3 * GROUP_W], wide[:, 3 * GROUP_W:])

    ii = lax.broadcasted_iota(jnp.int32, (c, GROUP_W), 0)
    jj = lax.broadcasted_iota(jnp.int32, (c, GROUP_W), 1) & (c - 1)
    diag2 = jnp.concatenate([ii == jj] * 2, axis=1)
    rows_gc = [_dot01_left(same_ref[...], jnp.where(diag2, wide[r, :2 * GROUP_W], 0.0), 2) for r in ls]
    gams = ([jnp.where(ii >= jj, jnp.exp(jnp.minimum(gcs[0][r] - rg[:, :GROUP_W], 0.0)), 0.0)
             for r, rg in zip(ls, rows_gc)],
            [jnp.where(ii <= jj, jnp.exp(jnp.minimum(gcs[1][r] - rg[:, GROUP_W:], 0.0)), 0.0)
             for r, rg in zip(ls, rows_gc)])
    eye = (ii == jj).astype(F32)
    stricts = (ii > jj, ii < jj)
    levels = [((ii // (2 * s)) == (jj // (2 * s))) & ((ii // s) != (jj // s)) for s in (1, 2, 4, 8, 16, 32)]
    bdmask = _bd_mask()
    outs = ((pf_ref, qpf_ref, bmf_ref, o0f_ref, df_ref), (pb_ref, qpb_ref, bmb_ref, o0b_ref, db_ref))

    qs = [q_scr[r, :] for r in rs]
    ks = [k_scr[r, :] for r in rs]
    vs = [v_scr[r, :] for r in rs]
    kqs = [_wide_dot(jnp.concatenate([k, q], axis=0), [k], bdmask, transpose_rhs=True) for k, q in zip(ks, qs)]
    pairs = [(di, ci) for di in range(2) for ci in range(len(rs))]
    gc_l = [gcs[di][ls[ci]] for di, ci in pairs]
    gam_l = [gams[di][ci] for di, ci in pairs]
    beta_l = [betas[di][ls[ci]] for di, ci in pairs]
    ms = [jnp.where(stricts[di], beta * kqs[ci][:c] * gam, 0.0)
          for (di, ci), beta, gam in zip(pairs, beta_l, gam_l)]
    ts = _tri_inverse(ms, eye, levels, bdmask)
    egcs = [jnp.exp(gc) for gc in gc_l]
    uws = [_wide_dot(t, [vs[ci] * beta, ks[ci] * beta * egc], bdmask)
           for (_, ci), t, beta, egc in zip(pairs, ts, beta_l, egcs)]
    aqks = [kqs[ci][c:] * gam for (_, ci), gam in zip(pairs, gam_l)]
    ows = [_wide_dot(aqk, [uw[:, :GROUP_W], uw[:, GROUP_W:]], bdmask) for aqk, uw in zip(aqks, uws)]
    for n, (di, ci) in enumerate(pairs):
        p_ref, qp_ref, bm_ref, o0_ref, d_ref = outs[di]
        r, gc, uw, ow = rs[ci], gc_l[n], uws[n], ows[n]
        glast = gc[0:1] if di else gc[c - 1:c]
        kdec = ks[ci] * jnp.exp(glast - gc)
        bp = _head_gram(kdec, [uw[:, :GROUP_W], uw[:, GROUP_W:]])
        bm_ref[0, r, :] = bp[:, :GROUP_W]
        p_ref[0, r, :] = bp[:, GROUP_W:].astype(BF16)
        o0_ref[0, r, :] = ow[:, :GROUP_W]
        qp_ref[0, r, :] = (qs[ci] * egcs[n] - ow[:, GROUP_W:]).astype(BF16)
        d_ref[0, g0 // c + ci:g0 // c + ci + 1, :] = jnp.exp(glast)


def _gdn_prep(qkv, ab, conv_w, a_log, dt_bias, tm):
    b, t, w3 = qkv.shape
    halo = 8
    nck = tm // GDN_C
    prev, cur, nxt = _halo_specs(tm, halo, t, w3)
    pad = lambda a: jnp.pad(a.reshape(1, -1), ((0, 0), (0, AB_PAD - a.size)))
    consts = _gdn_consts()
    tok = pl.BlockSpec((1, tm, GROUP_W), lambda i, j: (i, j, 0))
    dec = pl.BlockSpec((1, nck, GROUP_W), lambda i, j: (i, j, 0))
    tok_shape = lambda dt: jax.ShapeDtypeStruct((b, t, GROUP_W), dt)
    dir_specs = [tok, tok, tok, tok, dec]
    dir_shapes = [tok_shape(BF16), tok_shape(BF16), tok_shape(F32), tok_shape(F32),
                  jax.ShapeDtypeStruct((b, t // GDN_C, GROUP_W), F32)]
    return pl.pallas_call(
        _gdn_prep_kernel,
        grid=(b, t // tm),
        in_specs=[prev, cur, nxt, pl.BlockSpec((1, tm, AB_PAD), lambda i, j: (i, j, 0)),
                  _resident(conv_w.shape), _resident((1, AB_PAD)), _resident((1, AB_PAD))]
        + [_resident(cst.shape) for cst in consts],
        out_specs=dir_specs * 2,
        out_shape=dir_shapes * 2,
        scratch_shapes=[pltpu.VMEM((tm + 2 * halo, w3), F32)] + [pltpu.VMEM((tm, GROUP_W), F32)] * 3,
        compiler_params=_params(2), name="gdn_prep",
    )(qkv, qkv, qkv, ab, conv_w, pad(a_log), pad(dt_bias), *consts)


def _gdn_scan_kernel(pf_ref, qpf_ref, bmf_ref, o0f_ref, df_ref, pb_ref, qpb_ref, bmb_ref, o0b_ref, db_ref,
                     s0f_ref, s0b_ref, of_ref, ob_ref, sf_ref, sb_ref):
    nb, tm = pf_ref.shape[0], pf_ref.shape[1]
    nc = tm // GDN_C

    @pl.when(pl.program_id(0) == 0)
    def _():
        sf_ref[...] = s0f_ref[...]
        sb_ref[...] = s0b_ref[...]

    bdmask = _bd_mask()
    fwd = (pf_ref, qpf_ref, bmf_ref, o0f_ref, df_ref, of_ref)
    bwd = (pb_ref, qpb_ref, bmb_ref, o0b_ref, db_ref, ob_ref)
    chains = [(fwd, bi, False) for bi in range(nb)] + [(bwd, bi, True) for bi in range(nb)]
    states = [sf_ref[bi] for bi in range(nb)] + [sb_ref[bi] for bi in range(nb)]
    for step in range(nc):
        lhs, rows = [], []
        for (p_ref, qp_ref, _, _, _, _), bi, rev in chains:
            ci = nc - 1 - step if rev else step
            r = slice(ci * GDN_C, (ci + 1) * GDN_C)
            rows.append((ci, r))
            lhs.append(jnp.concatenate([p_ref[bi, r, :], qp_ref[bi, r, :]], axis=0))
        res = [_wide_dot(a, [s], bdmask) for a, s in zip(lhs, states)]
        new_states = []
        for (_, _, bm_ref, o0_ref, d_ref, o_ref), bi, _ in chains:
            n = len(new_states)
            ci, r = rows[n]
            o_ref[bi, r, :] = o0_ref[bi, r, :] + res[n][GDN_C:]
            new_states.append(d_ref[bi, ci:ci + 1, :] * states[n] - res[n][:GDN_C] + bm_ref[bi, r, :])
        states = new_states
    for bi in range(nb):
        sf_ref[bi] = states[bi]
        sb_ref[bi] = states[nb + bi]


def _gdn_scan(prep, s0f, s0b, tm):
    b, t, w = prep[0].shape
    nj = t // tm
    nck = tm // GDN_C
    fwd = pl.BlockSpec((b, tm, w), lambda j: (0, j, 0))
    bwd = pl.BlockSpec((b, tm, w), lambda j: (0, nj - 1 - j, 0))
    dfwd = pl.BlockSpec((b, nck, w), lambda j: (0, j, 0))
    dbwd = pl.BlockSpec((b, nck, w), lambda j: (0, nj - 1 - j, 0))
    st = pl.BlockSpec((b, GDN_HEAD_DIM, w), lambda j: (0, 0, 0))
    st_shape = jax.ShapeDtypeStruct((b, GDN_HEAD_DIM, w), F32)
    return pl.pallas_call(
        _gdn_scan_kernel,
        grid=(nj,),
        in_specs=[fwd] * 4 + [dfwd] + [bwd] * 4 + [dbwd] + [st, st],
        out_specs=[fwd, bwd, st, st],
        out_shape=[jax.ShapeDtypeStruct((b, t, w), F32)] * 2 + [st_shape] * 2,
        compiler_params=pltpu.CompilerParams(dimension_semantics=("arbitrary",),
                                             vmem_limit_bytes=VMEM_LIMIT),
        name="gdn_scan",
    )(*prep, s0f, s0b)


def _mix_out_kernel(h_ref, yp_ref, yfa_ref, yfb_ref, yc_ref, of_ref, ob_ref, z_ref, gt_ref, nw_ref, ones_ref,
                    w_ref, o_ref):
    o = of_ref[0] + ob_ref[0]
    ms = _dot01(o * o, ones_ref[...], 2) * (1.0 / GDN_HEAD_DIM)
    z = z_ref[0]
    yg = o * lax.rsqrt(ms + EPS) * nw_ref[...] * (z * jax.nn.sigmoid(z))
    y = jnp.zeros(h_ref.shape[1:], F32)
    yf = jnp.concatenate([yfa_ref[0], yfb_ref[0]], axis=-1)
    for gi, part in enumerate((yp_ref[0], yf, yc_ref[0], yg)):
        y = y + jnp.dot(part.astype(BF16), w_ref[gi * GROUP_W:(gi + 1) * GROUP_W, :],
                        preferred_element_type=F32)
    o_ref[0] = h_ref[0] + gt_ref[0] * y


def _mix_out(h, ys, o_f, o_b, z, gate, gdn_nw, w_out, layer, tm):
    b, t, d = h.shape
    tok = pl.BlockSpec((1, tm, d), lambda i, j: (i, j, 0))
    grp = pl.BlockSpec((1, tm, GROUP_W), lambda i, j: (i, j, 0))
    half = pl.BlockSpec((1, tm, LANES), lambda i, j: (i, j, 0))
    ones = _gdn_consts()[3]
    return pl.pallas_call(
        _mix_out_kernel,
        grid=(b, t // tm),
        in_specs=[tok, grp, half, half] + [grp] * 4 + [pl.BlockSpec((1, 1, d), lambda i, j: (i, 0, 0)),
                                     _resident((1, GROUP_W)), _resident(ones.shape), _layer_resident(w_out, layer)],
        out_specs=tok,
        out_shape=jax.ShapeDtypeStruct(h.shape, F32),
        compiler_params=_params(2), name="mix_out",
    )(h, *ys, o_f, o_b, z, gate, gdn_nw, ones, w_out)


MOD_ROWS = 8
MOD_TN = 1152


def _mod_kernel(c_ref, w_ref, b_ref, o_ref):
    cv = c_ref[...]
    a = (cv * jax.nn.sigmoid(cv)).astype(BF16)
    o_ref[0] = jnp.dot(a, w_ref[0].astype(BF16), preferred_element_type=F32) + b_ref[0]


def _modulation(c, c_ctx, mod_w, mod_b):
    nl, d, n = mod_w.shape
    cond = jnp.concatenate([c, c_ctx[None, :]], axis=0)
    cond = jnp.pad(cond, ((0, MOD_ROWS - cond.shape[0]), (0, 0)))
    return pl.pallas_call(
        _mod_kernel,
        grid=(nl, n // MOD_TN),
        in_specs=[_resident(cond.shape), pl.BlockSpec((1, d, MOD_TN), lambda i, j: (i, 0, j)),
                  pl.BlockSpec((1, 1, MOD_TN), lambda i, j: (i, 0, j))],
        out_specs=pl.BlockSpec((1, MOD_ROWS, MOD_TN), lambda i, j: (i, 0, j)),
        out_shape=jax.ShapeDtypeStruct((nl, MOD_ROWS, n), F32),
        compiler_params=_params(2), name="modulation",
    )(cond, mod_w, mod_b[:, None, :])


def _token_mix(p, gdn_state, grid_rows, tm, wl, need_out):
    p_pool, p_four, p_conv, p_qkv, p_z, p_ab = p
    prep = _gdn_prep(p_qkv, p_ab, wl["gdn_conv_w"], wl["gdn_a_log"], wl["gdn_dt_bias"], tm)
    o_f, o_b, s_f, s_b = _gdn_scan(prep, *gdn_state, tm)
    if not need_out:
        return None, (s_f, s_b)
    ys = (_pool_mix(p_pool, wl["pool_wbd"], wl["pool_scale"], grid_rows),
          *_fourier_mix(p_four, wl["fourier_w"]),
          _conv_mix(p_conv, wl["conv_dw_w"], wl["conv_dw_b"], wl["conv_ln_g"], wl["conv_ln_b"],
                    wl["conv_pw_w"], tm))
    return (ys, o_f, o_b, p_z), (s_f, s_b)


def kernel(x, c, ctx, c_ctx, mod_w, mod_b, norm_w, ffn1_wg, ffn1_wu, ffn1_wd, ffn2_wg, ffn2_wu,
           ffn2_wd, w_in, w_out, pool_w, pool_scale, fourier_w, conv_dw_w, conv_dw_b, conv_ln_g,
           conv_ln_b, conv_pw_w, gdn_conv_w, gdn_a_log, gdn_dt_bias, gdn_norm_w, final_norm_w):
    bsz, seq, d = x.shape
    n_ctx = ctx.shape[1]
    rows = seq // GRID_W
    tm_x, tm_c = 512, n_ctx
    hx, hc = x, ctx
    mods = _modulation(c, c_ctx, mod_w, mod_b)
    zero_state = (jnp.zeros((bsz, GDN_HEAD_DIM, GROUP_W), F32),) * 2
    f1 = tuple(_cast_bf16(w) for w in (ffn1_wg, ffn1_wu, ffn1_wd))
    f2 = tuple(_cast_bf16(w) for w in (ffn2_wg, ffn2_wu, ffn2_wd))
    w_main = _cast_bf16(w_in, cols=2 * GDN_OFF)
    w_ab = jnp.pad(w_in[:, :, 2 * GDN_OFF:], ((0, 0), (0, 0), (0, AB_PAD - N_GATE))).astype(BF16)
    w_out_b = _cast_bf16(w_out)
    for l in range(DEPTH):
        last = l == DEPTH - 1
        mx = [m[:, None, :] for m in jnp.split(mods[l, :bsz], N_MOD, axis=-1)]
        mc = [jnp.broadcast_to(m[None], (bsz, 1, d)) for m in jnp.split(mods[l, bsz:bsz + 1], N_MOD, axis=-1)]
        nw = norm_w[l][:, None, :]
        row = lambda a: a.reshape(1, -1)
        wl = dict(pool_wbd=_blockdiag(pool_w[l]).astype(BF16), pool_scale=row(pool_scale[l]),
                  fourier_w=fourier_w[l].astype(BF16), conv_dw_w=conv_dw_w[l], conv_dw_b=row(conv_dw_b[l]),
                  conv_ln_g=row(conv_ln_g[l]), conv_ln_b=row(conv_ln_b[l]),
                  conv_pw_w=conv_pw_w[l].astype(BF16), gdn_conv_w=gdn_conv_w[l],
                  gdn_a_log=gdn_a_log[l], gdn_dt_bias=gdn_dt_bias[l])
        gdn_nw = row(jnp.tile(gdn_norm_w[l], GDN_HEADS))

        hx = _ffn(hx, nw[0], mx[0], mx[1], mx[2], f1, l, tm_x)
        hc = _ffn(hc, nw[0], mc[0], mc[1], mc[2], f1, l, tm_c)

        px = _inproj(hx, nw[1], mx[3], mx[4], w_main, w_ab, l, False, tm_x)
        pc = _inproj(hc, nw[1], mc[3], mc[4], w_main, w_ab, l, last, tm_c)
        if last:
            pc = (None,) * 3 + tuple(pc)

        mix_c, ctx_state = _token_mix(pc, zero_state, None, tm_c, wl, not last)
        mix_x, _ = _token_mix(px, ctx_state, rows, tm_x, wl, True)
        hx = _mix_out(hx, *mix_x, mx[5], gdn_nw, w_out_b, l, tm_x)
        hx = _ffn(hx, nw[2], mx[6], mx[7], mx[8], f2, l, tm_x, final_norm_w[None, :] if last else None)
        if not last:
            hc = _mix_out(hc, *mix_c, mc[5], gdn_nw, w_out_b, l, tm_c)
            hc = _ffn(hc, nw[2], mc[6], mc[7], mc[8], f2, l, tm_c)
    return hx
```

```python
import functools
import math

import jax
import jax.numpy as jnp
import numpy as np
from jax import lax
from jax.experimental import pallas as pl
from jax.experimental.pallas import tpu as pltpu

D_MODEL = 1024
DEPTH = 4
GRID_W = 64
N_MIXERS = 4
GROUP_W = D_MODEL // N_MIXERS
POOL_WINDOWS = (2, 4, 8, 16)
POOL_GROUPS = 4
POOL_GW = GROUP_W // POOL_GROUPS
FOURIER_GROUPS = 4
FOURIER_GW = GROUP_W // FOURIER_GROUPS
CONV_K = 31
GDN_HEAD_DIM = 64
GDN_HEADS = GROUP_W // GDN_HEAD_DIM
GDN_CONV = 3
GDN_CHUNK = 64
FFN_HIDDEN = 128 * ((8 * D_MODEL // 3 + 127) // 128)
N_MOD = 9
EPS = 1e-6
POOL_OFF = 0
FOURIER_OFF = POOL_OFF + GROUP_W
CONV_OFF = FOURIER_OFF + GROUP_W
GDN_OFF = CONV_OFF + 2 * GROUP_W

LANES = 128
SUBLANES = 8
VMEM_LIMIT = 56 * 1024 * 1024
FFN_CHUNK = 256
AB_PAD = LANES

BF16 = jnp.bfloat16
F32 = jnp.float32


def _params(n_axes):
    return pltpu.CompilerParams(dimension_semantics=("parallel",) * n_axes,
                                vmem_limit_bytes=VMEM_LIMIT)


def _resident(shape):
    nd = len(shape)
    return pl.BlockSpec(shape, lambda *_: (0,) * nd, pipeline_mode=pl.Buffered(1))


def _layer_resident(stack, layer, cols=None, col_block=0):
    _, r, c = stack.shape
    return pl.BlockSpec((None, r, cols or c), lambda *_: (layer, 0, col_block), pipeline_mode=pl.Buffered(1))


def _cast_kernel(x_ref, o_ref):
    o_ref[...] = x_ref[...].astype(BF16)


def _cast_bf16(w, cols=None, rows_per_step=256):
    nl, r, c = w.shape
    cols = cols or c
    spec = pl.BlockSpec((1, rows_per_step, cols), lambda i, j: (i, j, 0))
    return pl.pallas_call(
        _cast_kernel, grid=(nl, r // rows_per_step), in_specs=[spec], out_specs=spec,
        out_shape=jax.ShapeDtypeStruct((nl, r, cols), BF16), compiler_params=_params(2), name="cast_bf16",
    )(w)


def _rms_mod(h, nw, shift, scale):
    ms = jnp.mean(h * h, axis=-1, keepdims=True)
    n = h * lax.rsqrt(ms + EPS) * nw
    return n * (1.0 + scale) + shift


def _ffn_apply(h, nw_ref, sh_ref, sc_ref, gt_ref, wg_ref, wu_ref, wd_ref, final_nw_ref=None):
    nb = _rms_mod(h, nw_ref[...], sh_ref[0], sc_ref[0]).astype(BF16)
    acc = jnp.zeros(h.shape, F32)
    for f0 in range(0, FFN_HIDDEN, FFN_CHUNK):
        g = jnp.dot(nb, wg_ref[:, f0:f0 + FFN_CHUNK], preferred_element_type=F32)
        u = jnp.dot(nb, wu_ref[:, f0:f0 + FFN_CHUNK], preferred_element_type=F32)
        a = (g * jax.nn.sigmoid(g) * u).astype(BF16)
        acc = acc + jnp.dot(a, wd_ref[f0:f0 + FFN_CHUNK, :], preferred_element_type=F32)
    out = h + (0.5 * gt_ref[0]) * acc
    if final_nw_ref is not None:
        out = out * lax.rsqrt(jnp.mean(out * out, axis=-1, keepdims=True) + EPS) * final_nw_ref[...]
    return out


def _ffn_kernel(h_ref, *rest):
    rest[-1][0] = _ffn_apply(h_ref[0], *rest[:-1])


def _ffn(h, nw, shift, scale, gate, weights, layer, tm, final_nw=None):
    b, t, d = h.shape
    vec = pl.BlockSpec((1, 1, d), lambda i, j: (i, 0, 0))
    tok = pl.BlockSpec((1, tm, d), lambda i, j: (i, j, 0))
    extra = [] if final_nw is None else [final_nw]
    return pl.pallas_call(
        _ffn_kernel,
        grid=(b, t // tm),
        in_specs=[tok, _resident((1, d)), vec, vec, vec] + [_layer_resident(w, layer) for w in weights]
        + [_resident((1, d))] * len(extra),
        out_specs=tok,
        out_shape=jax.ShapeDtypeStruct(h.shape, F32),
        compiler_params=_params(2),
        name="ffn",
    )(h, nw, shift, scale, gate, *weights, *extra)


IN_SPLITS = (GROUP_W, GROUP_W, 2 * GROUP_W, 3 * GROUP_W, GROUP_W, AB_PAD)


def _inproj_kernel(h_ref, nw_ref, sh_ref, sc_ref, w_ref, wab_ref, *o_refs):
    tm = h_ref.shape[1]
    half = tm // 2 if tm % (2 * SUBLANES) == 0 else tm
    for r0 in range(0, tm, half):
        rows = slice(r0, r0 + half)
        nb = _rms_mod(h_ref[0, rows, :], nw_ref[...], sh_ref[0], sc_ref[0]).astype(BF16)
        off = 0
        for o_ref in o_refs[:-1]:
            wdt = o_ref.shape[-1]
            o_ref[0, rows, :] = jnp.dot(nb, w_ref[:, off:off + wdt], preferred_element_type=F32)
            off += wdt
        o_refs[-1][0, rows, :] = jnp.dot(nb, wab_ref[...], preferred_element_type=F32)


def _inproj(h, nw, shift, scale, w_main, w_ab, layer, gdn_only, tm):
    b, t, d = h.shape
    vec = pl.BlockSpec((1, 1, d), lambda i, j: (i, 0, 0))
    tok = pl.BlockSpec((1, tm, d), lambda i, j: (i, j, 0))
    splits = IN_SPLITS[3:] if gdn_only else IN_SPLITS
    w_spec = (_layer_resident(w_main, layer, GDN_OFF, 1) if gdn_only else _layer_resident(w_main, layer))
    return pl.pallas_call(
        _inproj_kernel,
        grid=(b, t // tm),
        in_specs=[tok, _resident((1, d)), vec, vec, w_spec, _layer_resident(w_ab, layer)],
        out_specs=[pl.BlockSpec((1, tm, s), lambda i, j: (i, j, 0)) for s in splits],
        out_shape=[jax.ShapeDtypeStruct((b, t, s), F32) for s in splits],
        compiler_params=_params(2),
        name="inproj",
    )(h, nw, shift, scale, w_main, w_ab)


def _split_bf16(x, parts):
    out = []
    for _ in range(parts - 1):
        p = x.astype(BF16)
        out.append(p)
        x = x - p.astype(F32)
    return out + [x.astype(BF16)]


def _dot01(x, w01, parts=3):
    return sum(jnp.dot(p, w01, preferred_element_type=F32) for p in _split_bf16(x, parts))


def _dot01_left(w01, x, parts=3):
    return sum(jnp.dot(w01, p, preferred_element_type=F32) for p in _split_bf16(x, parts))


def _blockdiag(w):
    g, a, b = w.shape
    return jnp.einsum('gab,gh->gahb', w, jnp.eye(g, dtype=w.dtype)).reshape(g * a, g * b)


def _halo_specs(tm, halo, t, width):
    r = tm // halo
    last = t // halo - 1
    prev = pl.BlockSpec((1, halo, width), lambda i, j: (i, jnp.maximum(j * r - 1, 0), 0))
    cur = pl.BlockSpec((1, tm, width), lambda i, j: (i, j, 0))
    nxt = pl.BlockSpec((1, halo, width), lambda i, j: (i, jnp.minimum((j + 1) * r, last), 0))
    return prev, cur, nxt


def _edge_masked(prev_ref, next_ref):
    j, nj = pl.program_id(1), pl.num_programs(1)
    top = jnp.where(j > 0, prev_ref[0], 0.0)
    bot = jnp.where(j < nj - 1, next_ref[0], 0.0)
    return top, bot


POOL_TM = 1024
POOL_HALO = 512


def _pool_window_sums_1d(u, row_len):
    n = u.shape[0]
    col = lax.broadcasted_iota(jnp.int32, u.shape, 0) & (row_len - 1)
    grp = lax.broadcasted_iota(jnp.int32, u.shape, 1) // POOL_GW

    def back(x, s):
        return jnp.where(col >= s, pltpu.roll(x, s, 0), 0.0)

    def fwd(x, s):
        return jnp.where(col < row_len - s, pltpu.roll(x, n - s, 0), 0.0)

    b = back(u, 1)
    f = u
    out = b + f
    for gi in range(1, len(POOL_WINDOWS)):
        s = POOL_WINDOWS[gi] // 4
        b = b + back(b, s)
        f = f + fwd(f, s)
        out = jnp.where(grp >= gi, b + f, out)
    return out


def _pool_counts(idx, extent, halfw):
    return jnp.minimum(idx + halfw, extent) - jnp.maximum(idx - halfw, 0)


def _pool_finish(mean, u, wbd_ref, scale_ref, o_ref):
    d = (mean - u).astype(BF16)
    o_ref[0] = (jnp.dot(d, wbd_ref[...], preferred_element_type=F32) * scale_ref[...]).astype(BF16)


def _pool_grid_kernel(prev_ref, cur_ref, next_ref, wbd_ref, scale_ref, o_ref, *, n_rows):
    tm = cur_ref.shape[1]
    top, bot = _edge_masked(prev_ref, next_ref)
    u = cur_ref[0]
    grp = lax.broadcasted_iota(jnp.int32, (tm, GROUP_W), 1) // POOL_GW
    arr, off = jnp.concatenate([top, u, bot], axis=0), 0
    tot = None
    for gi, w in enumerate(POOL_WINDOWS):
        sh = GRID_W * max(w // 4, 1) if gi else GRID_W
        if gi == 0:
            arr = arr[:-sh] + arr[sh:]
            off = sh
        else:
            arr = arr[:-2 * sh] + arr[2 * sh:]
            off = off + sh
        centre = arr[POOL_HALO - off:POOL_HALO - off + tm]
        tot = centre if tot is None else jnp.where(grp >= gi, centre, tot)
    tot = _pool_window_sums_1d(tot, GRID_W)
    tok = pl.program_id(1) * tm + lax.broadcasted_iota(jnp.int32, (tm, GROUP_W), 0)
    halfw = jnp.left_shift(1, grp)
    cnt = (_pool_counts(tok // GRID_W, n_rows, halfw) * _pool_counts(tok & (GRID_W - 1), GRID_W, halfw))
    _pool_finish(tot / cnt.astype(F32), u, wbd_ref, scale_ref, o_ref)


def _pool_seq_kernel(u_ref, wbd_ref, scale_ref, o_ref):
    u = u_ref[0]
    n = u.shape[0]
    tot = _pool_window_sums_1d(u, n)
    grp = lax.broadcasted_iota(jnp.int32, u.shape, 1) // POOL_GW
    tok = lax.broadcasted_iota(jnp.int32, u.shape, 0)
    cnt = _pool_counts(tok, n, jnp.left_shift(1, grp))
    _pool_finish(tot / cnt.astype(F32), u, wbd_ref, scale_ref, o_ref)


def _pool_mix(u, wbd, scale, grid_rows):
    b, t, w = u.shape
    out_shape = jax.ShapeDtypeStruct(u.shape, BF16)
    if grid_rows is None:
        tok = pl.BlockSpec((1, t, w), lambda i: (i, 0, 0))
        return pl.pallas_call(
            _pool_seq_kernel, grid=(b,),
            in_specs=[tok, _resident(wbd.shape), _resident(scale.shape)],
            out_specs=tok, out_shape=out_shape, compiler_params=_params(1), name="pool_seq",
        )(u, wbd, scale)
    prev, cur, nxt = _halo_specs(POOL_TM, POOL_HALO, t, w)
    return pl.pallas_call(
        functools.partial(_pool_grid_kernel, n_rows=grid_rows),
        grid=(b, t // POOL_TM),
        in_specs=[prev, cur, nxt, _resident(wbd.shape), _resident(scale.shape)],
        out_specs=cur, out_shape=out_shape, compiler_params=_params(2), name="pool_grid",
    )(u, u, u, wbd, scale)


CONV_TM = 512
CONV_HALO = 16
CONV_SUB = 64


def _glu(x):
    return x[:, :GROUP_W] * jax.nn.sigmoid(x[:, GROUP_W:])


def _conv_kernel(prev_ref, cur_ref, next_ref, dww_ref, dwb_ref, lng_ref, lnb_ref, pw_ref, o_ref,
                 ext_ref, sh_ref):
    tm = cur_ref.shape[1]
    top, bot = _edge_masked(prev_ref, next_ref)
    ext_ref[0:CONV_HALO] = _glu(top)
    ext_ref[CONV_HALO:CONV_HALO + tm] = _glu(cur_ref[0])
    ext_ref[CONV_HALO + tm:] = _glu(bot)
    n_sh = sh_ref.shape[1]
    for s in range(SUBLANES):
        sh_ref[s] = ext_ref[s:s + n_sh, :]
    base = CONV_HALO - CONV_K // 2
    for r0 in range(0, tm, CONV_SUB):
        acc = jnp.zeros((CONV_SUB, GROUP_W), F32)
        for k in range(CONV_K):
            a, s = divmod(base + k, SUBLANES)
            acc = acc + dww_ref[k:k + 1, :] * sh_ref[s, r0 + a * SUBLANES:r0 + a * SUBLANES + CONV_SUB, :]
        h = acc + dwb_ref[...]
        mu = jnp.mean(h, axis=-1, keepdims=True)
        var = jnp.mean(jnp.square(h - mu), axis=-1, keepdims=True)
        h = (h - mu) * lax.rsqrt(var + EPS) * lng_ref[...] + lnb_ref[...]
        h = (h * jax.nn.sigmoid(h)).astype(BF16)
        o_ref[0, r0:r0 + CONV_SUB, :] = jnp.dot(h, pw_ref[...], preferred_element_type=F32).astype(BF16)


def _conv_mix(u2, dw_w, dw_b, ln_g, ln_b, pw_w, tm):
    b, t, w2 = u2.shape
    prev, cur, nxt = _halo_specs(tm, CONV_HALO, t, w2)
    row = _resident((1, GROUP_W))
    return pl.pallas_call(
        _conv_kernel,
        grid=(b, t // tm),
        in_specs=[prev, cur, nxt, _resident(dw_w.shape), row, row, row, _resident(pw_w.shape)],
        out_specs=pl.BlockSpec((1, tm, GROUP_W), lambda i, j: (i, j, 0)),
        out_shape=jax.ShapeDtypeStruct((b, t, GROUP_W), BF16),
        scratch_shapes=[pltpu.VMEM((tm + 2 * CONV_HALO, GROUP_W), F32),
                        pltpu.VMEM((SUBLANES, tm + 2 * CONV_HALO - SUBLANES, GROUP_W), F32)],
        compiler_params=_params(2), name="conv_mix",
    )(u2, u2, u2, dw_w, dw_b, ln_g, ln_b, pw_w)


FOUR_N1 = 128
FOUR_TT = 8
FOUR_KT = 8


def _hilo(w):
    w = jnp.asarray(w, F32)
    hi = w.astype(BF16)
    return hi, (w - hi.astype(F32)).astype(BF16)


def _dot_hp(x, wh, wl):
    xh = x.astype(BF16)
    xl = (x - xh.astype(F32)).astype(BF16)
    return (jnp.dot(xh, wh, preferred_element_type=F32) + jnp.dot(xh, wl, preferred_element_type=F32)
            + jnp.dot(xl, wh, preferred_element_type=F32))


def _dot_hp_left(wh, wl, x):
    xh = x.astype(BF16)
    xl = (x - xh.astype(F32)).astype(BF16)
    return (jnp.dot(wh, xh, preferred_element_type=F32) + jnp.dot(wl, xh, preferred_element_type=F32)
            + jnp.dot(wh, xl, preferred_element_type=F32))


def _cos_sin(n, rows=None, cols=None):
    r = np.arange(n if rows is None else rows, dtype=np.int64)
    c = np.arange(n if cols is None else cols, dtype=np.int64)
    ang = 2.0 * np.pi * ((np.outer(r, c) % n).astype(np.float64) / n)
    return np.cos(ang), np.sin(ang)


def _channel_dft():
    c, s = _cos_sin(FOURIER_GW)
    eye = np.eye(FOURIER_GROUPS)
    return np.concatenate([np.kron(eye, c), -np.kron(eye, s)], axis=1)


def _fourier_a_kernel(ua_ref, ub_ref, cdh_ref, cdl_ref, f1h_ref, f1l_ref, twc_ref, tws_ref, o_ref):
    x = jnp.concatenate([jnp.concatenate([ua_ref[:, j, :], ub_ref[:, j, :]], axis=1) for j in range(FOUR_TT)],
                        axis=0)
    z_all = _dot_hp(x, cdh_ref[...], cdl_ref[...])
    for j in range(FOUR_TT):
        z = z_all[j * FOUR_N1:(j + 1) * FOUR_N1]
        zz = jnp.concatenate([z[:, :GROUP_W], z[:, GROUP_W:]], axis=0)
        a = _dot_hp_left(f1h_ref[...], f1l_ref[...], zz)
        ar, ai = a[:FOUR_N1], a[FOUR_N1:]
        c = jnp.concatenate([twc_ref[j]] * (GROUP_W // LANES), axis=1)
        s = jnp.concatenate([tws_ref[j]] * (GROUP_W // LANES), axis=1)
        o_ref[0, :, 0, j, :] = ar * c + ai * s
        o_ref[0, :, 1, j, :] = ai * c - ar * s


def _fourier_b_kernel(g_ref, f2h_ref, f2l_ref, fw_ref, oa_ref, ob_ref, *, scale):
    fs = []
    for kk in range(FOUR_KT):
        gm = g_ref[0, kk].reshape(2 * FOUR_N1, GROUP_W)
        fs.append((_dot_hp_left(f2h_ref[...], f2l_ref[...], gm) * scale).astype(BF16))
    y = jnp.dot(jnp.concatenate(fs, axis=0), fw_ref[...], preferred_element_type=F32)
    for kk in range(FOUR_KT):
        yk = y[kk * FOUR_N1:(kk + 1) * FOUR_N1]
        oa_ref[:, kk, :] = yk[:, :LANES]
        ob_ref[:, kk, :] = yk[:, LANES:]


def _fourier_small_kernel(u_ref, cdh_ref, cdl_ref, fh_ref, fl_ref, fw_ref, oa_ref, ob_ref, *, scale):
    z = _dot_hp(u_ref[0], cdh_ref[...], cdl_ref[...])
    zz = jnp.concatenate([z[:, :GROUP_W], z[:, GROUP_W:]], axis=0)
    f = _dot_hp_left(fh_ref[...], fl_ref[...], zz) * scale
    y = jnp.dot(f.astype(BF16), fw_ref[...], preferred_element_type=F32)
    oa_ref[0] = y[:, :LANES]
    ob_ref[0] = y[:, LANES:]


def _fourier_mix(u, fw):
    b, n, w = u.shape
    scale = 1.0 / math.sqrt(n * FOURIER_GW)
    cdh, cdl = _hilo(_channel_dft())
    if n != FOUR_N1 * FOUR_N1:
        c, s = _cos_sin(n)
        fh, fl = _hilo(np.concatenate([c, s], axis=1))
        tok = pl.BlockSpec((1, n, w), lambda i: (i, 0, 0))
        half = pl.BlockSpec((1, n, LANES), lambda i: (i, 0, 0))
        return pl.pallas_call(
            functools.partial(_fourier_small_kernel, scale=scale), grid=(b,),
            in_specs=[tok] + [_resident(a.shape) for a in (cdh, cdl, fh, fl, fw)],
            out_specs=[half, half], out_shape=[jax.ShapeDtypeStruct((b, n, LANES), F32)] * 2,
            compiler_params=_params(1), name="fourier_small",
        )(u, cdh, cdl, fh, fl, fw)
    n1 = FOUR_N1
    c1, s1 = _cos_sin(n1)
    f1h, f1l = _hilo(np.block([[c1, s1], [-s1, c1]]))
    f2h, f2l = _hilo(np.concatenate([c1, s1], axis=1))
    twc, tws = _cos_sin(n, rows=n1, cols=n1)
    twc = jnp.broadcast_to(jnp.asarray(twc, F32)[:, :, None], (n1, n1, LANES))
    tws = jnp.broadcast_to(jnp.asarray(tws, F32)[:, :, None], (n1, n1, LANES))
    tw_spec = pl.BlockSpec((FOUR_TT, n1, LANES), lambda i, j: (j, 0, 0))
    u4 = u.reshape(b, n1, n1, w)
    g = pl.pallas_call(
        _fourier_a_kernel, grid=(b, n1 // FOUR_TT),
        in_specs=[pl.BlockSpec((None, n1, FOUR_TT, LANES), lambda i, j: (i, 0, j, 0)),
                  pl.BlockSpec((None, n1, FOUR_TT, LANES), lambda i, j: (i, 0, j, 1))]
        + [_resident(a.shape) for a in (cdh, cdl, f1h, f1l)] + [tw_spec, tw_spec],
        out_specs=pl.BlockSpec((1, n1, 2, FOUR_TT, w), lambda i, j: (i, 0, 0, j, 0)),
        out_shape=jax.ShapeDtypeStruct((b, n1, 2, n1, w), F32),
        compiler_params=_params(2), name="fourier_a",
    )(u4, u4, cdh, cdl, f1h, f1l, twc, tws)
    half_out = pl.BlockSpec((None, n1, FOUR_KT, LANES), lambda i, j: (i, 0, j, 0))
    ya, yb = pl.pallas_call(
        functools.partial(_fourier_b_kernel, scale=scale), grid=(b, n1 // FOUR_KT),
        in_specs=[pl.BlockSpec((1, FOUR_KT, 2, n1, w), lambda i, j: (i, j, 0, 0, 0))]
        + [_resident(a.shape) for a in (f2h, f2l, fw)],
        out_specs=[half_out, half_out],
        out_shape=[jax.ShapeDtypeStruct((b, n1, n1, LANES), F32)] * 2,
        compiler_params=_params(2), name="fourier_b",
    )(g, f2h, f2l, fw)
    return ya.reshape(b, n, LANES), yb.reshape(b, n, LANES)


GDN_C = GDN_CHUNK
GDN_PREP_TM = 512
GDN_SCAN_TM = 512
GDN_SUB = 64
N_GATE = 4 * GDN_HEADS


def _softplus(x):
    return jnp.maximum(x, 0.0) + jnp.log1p(jnp.exp(-jnp.abs(x)))


def _gdn_consts():
    i = np.arange(GDN_C)
    lower = np.concatenate([i[None, :] <= i[:, None], i[None, :] >= i[:, None]], axis=0)
    same = np.ones((GDN_C, GDN_C), bool)
    expand = np.zeros((AB_PAD, 4 * GROUP_W), np.float32)
    for s in range(4):
        for h in range(GDN_HEADS):
            c0 = s * GROUP_W + h * GDN_HEAD_DIM
            expand[s * GDN_HEADS + h, c0:c0 + GDN_HEAD_DIM] = 1.0
    lane = np.arange(GROUP_W)
    headones = (lane[:, None] // GDN_HEAD_DIM) == (lane[None, :] // GDN_HEAD_DIM)
    as_bf = lambda a: jnp.asarray(a, F32).astype(BF16)
    return as_bf(lower), as_bf(same), as_bf(expand), as_bf(headones)


HEAD_PAIR = LANES // GDN_HEAD_DIM
N_PAIRS = GROUP_W // LANES


def _halves(x):
    return [x[:, i * LANES:(i + 1) * LANES] for i in range(N_PAIRS)]


def _bd(xh, bdmask):
    xb = xh.astype(BF16)
    return jnp.where(bdmask, jnp.concatenate([xb] * HEAD_PAIR, axis=0), jnp.zeros((), BF16))


def _bd_mask():
    return (lax.broadcasted_iota(jnp.int32, (LANES, LANES), 0) // GDN_HEAD_DIM
            == lax.broadcasted_iota(jnp.int32, (LANES, LANES), 1) // GDN_HEAD_DIM)


def _wide_dot(x, ys, bdmask, transpose_rhs=False):
    parts = []
    for xh, yhs in zip(_halves(x.astype(BF16)), zip(*[_halves(y) for y in ys])):
        w = jnp.concatenate([_bd(yh, bdmask) for yh in yhs], axis=0 if transpose_rhs else 1)
        dims = (((1,), (1,)), ((), ())) if transpose_rhs else (((1,), (0,)), ((), ()))
        parts.append(lax.dot_general(xh, w, dims, preferred_element_type=F32))
    return jnp.concatenate([p[:, i * LANES:(i + 1) * LANES] for i in range(len(ys)) for p in parts], axis=1)


def _head_gram(a, bs):
    outs = [[] for _ in bs]
    for ah, bhs in zip(_halves(a.astype(BF16)), zip(*[_halves(b.astype(BF16)) for b in bs])):
        full = lax.dot_general(ah, jnp.concatenate(bhs, axis=1), (((0,), (0,)), ((), ())),
                               preferred_element_type=F32)
        head = (lax.broadcasted_iota(jnp.int32, (GDN_HEAD_DIM, full.shape[1]), 1) // GDN_HEAD_DIM) % HEAD_PAIR
        keep = None
        for h in range(HEAD_PAIR):
            blk = jnp.where(head == h, full[h * GDN_HEAD_DIM:(h + 1) * GDN_HEAD_DIM], 0.0)
            keep = blk if keep is None else keep + blk
        for i in range(len(bs)):
            outs[i].append(keep[:, i * LANES:(i + 1) * LANES])
    return jnp.concatenate([jnp.concatenate(o, axis=1) for o in outs], axis=1)


def _tri_inverse(ms, eye, level_masks, bdmask):
    ds = [eye - jnp.where(level_masks[0], m, 0.0) for m in ms]
    for mask in level_masks[1:]:
        es = [_wide_dot(jnp.where(mask, m, 0.0), [d], bdmask) for m, d in zip(ms, ds)]
        fs = [_wide_dot(d, [e], bdmask) for d, e in zip(ds, es)]
        ds = [d - f for d, f in zip(ds, fs)]
    return ds


def _gdn_prep_kernel(prev_ref, cur_ref, next_ref, ab_ref, cw_ref, alog_ref, dtb_ref,
                     tri_ref, same_ref, exp_ref, ones_ref,
                     pf_ref, qpf_ref, bmf_ref, o0f_ref, df_ref, pb_ref, qpb_ref, bmb_ref, o0b_ref, db_ref,
                     ext_ref, q_scr, k_scr, v_scr):
    tm = cur_ref.shape[1]
    halo = prev_ref.shape[1]
    top, bot = _edge_masked(prev_ref, next_ref)
    ext_ref[0:halo] = top
    ext_ref[halo:halo + tm] = cur_ref[0]
    ext_ref[halo + tm:] = bot
    c = GDN_C
    g0, grp = 0, tm
    ls = rs = [slice(ci * c, (ci + 1) * c) for ci in range(grp // c)]
    rows_g = slice(g0, g0 + grp)
    base = halo - GDN_CONV // 2
    for r0 in range(g0, g0 + grp, GDN_SUB):
        acc = jnp.zeros((GDN_SUB, 3 * GROUP_W), F32)
        for t in range(GDN_CONV):
            acc = acc + cw_ref[t:t + 1, :] * ext_ref[r0 + base + t:r0 + base + t + GDN_SUB, :]
        x = acc * jax.nn.sigmoid(acc)
        rows = slice(r0, r0 + GDN_SUB)
        q_scr[rows, :] = x[:, :GROUP_W]
        k_scr[rows, :] = x[:, GROUP_W:2 * GROUP_W]
        v_scr[rows, :] = x[:, 2 * GROUP_W:]
    qa, ka = q_scr[rows_g, :], k_scr[rows_g, :]
    q_scr[rows_g, :] = qa * lax.rsqrt(_dot01(qa * qa, ones_ref[...], 2) + EPS) * (GDN_HEAD_DIM ** -0.5)
    k_scr[rows_g, :] = ka * lax.rsqrt(_dot01(ka * ka, ones_ref[...], 2) + EPS)

    ab = ab_ref[0, rows_g, :]
    lane = lax.broadcasted_iota(jnp.int32, ab.shape, 1)
    g = -jnp.exp(alog_ref[...]) * _softplus(ab + dtb_ref[...])
    gate = jnp.where(lane < 2 * GDN_HEADS, g, jax.nn.sigmoid(ab))
    cums = [_dot01_left(tri_ref[...], gate[r], 2) for r in ls]
    lane_c = lax.broadcasted_iota(jnp.int32, (c, AB_PAD), 1)
    cum = jnp.concatenate([jnp.where(lane_c < GDN_HEADS, cm[:c], cm[c:]) for cm in cums], axis=0)
    wide = _dot01(jnp.where(lane < 2 * GDN_HEADS, cum, gate), exp_ref[...], 2)
    gcs = (wide[:, :GROUP_W], wide[:, GROUP_W:2 * GROUP_W])
    betas = (wide[:, 2 * GROUP_W:3 * GROUP_W], wide[:, 3 * GROUP_W:])

    ii = lax.broadcasted_iota(jnp.int32, (c, GROUP_W), 0)
    jj = lax.broadcasted_iota(jnp.int32, (c, GROUP_W), 1) & (c - 1)
    diag2 = jnp.concatenate([ii == jj] * 2, axis=1)
    rows_gc = [_dot01_left(same_ref[...], jnp.where(diag2, wide[r, :2 * GROUP_W], 0.0), 2) for r in ls]
    gams = ([jnp.where(ii >= jj, jnp.exp(jnp.minimum(gcs[0][r] - rg[:, :GROUP_W], 0.0)), 0.0)
             for r, rg in zip(ls, rows_gc)],
            [jnp.where(ii <= jj, jnp.exp(jnp.minimum(gcs[1][r] - rg[:, GROUP_W:], 0.0)), 0.0)
             for r, rg in zip(ls, rows_gc)])
    eye = (ii == jj).astype(F32)
    stricts = (ii > jj, ii < jj)
    levels = [((ii // (2 * s)) == (jj // (2 * s))) & ((ii // s) != (jj // s)) for s in (1, 2, 4, 8, 16, 32)]
    bdmask = _bd_mask()
    outs = ((pf_ref, qpf_ref, bmf_ref, o0f_ref, df_ref), (pb_ref, qpb_ref, bmb_ref, o0b_ref, db_ref))

    qs = [q_scr[r, :] for r in rs]
    ks = [k_scr[r, :] for r in rs]
    vs = [v_scr[r, :] for r in rs]
    kqs = [_wide_dot(jnp.concatenate([k, q], axis=0), [k], bdmask, transpose_rhs=True) for k, q in zip(ks, qs)]
    pairs = [(di, ci) for di in range(2) for ci in range(len(rs))]
    gc_l = [gcs[di][ls[ci]] for di, ci in pairs]
    gam_l = [gams[di][ci] for di, ci in pairs]
    beta_l = [betas[di][ls[ci]] for di, ci in pairs]
    ms = [jnp.where(stricts[di], beta * kqs[ci][:c] * gam, 0.0)
          for (di, ci), beta, gam in zip(pairs, beta_l, gam_l)]
    ts = _tri_inverse(ms, eye, levels, bdmask)
    egcs = [jnp.exp(gc) for gc in gc_l]
    uws = [_wide_dot(t, [vs[ci] * beta, ks[ci] * beta * egc], bdmask)
           for (_, ci), t, beta, egc in zip(pairs, ts, beta_l, egcs)]
    aqks = [kqs[ci][c:] * gam for (_, ci), gam in zip(pairs, gam_l)]
    ows = [_wide_dot(aqk, [uw[:, :GROUP_W], uw[:, GROUP_W:]], bdmask) for aqk, uw in zip(aqks, uws)]
    for n, (di, ci) in enumerate(pairs):
        p_ref, qp_ref, bm_ref, o0_ref, d_ref = outs[di]
        r, gc, uw, ow = rs[ci], gc_l[n], uws[n], ows[n]
        glast = gc[0:1] if di else gc[c - 1:c]
        kdec = ks[ci] * jnp.exp(glast - gc)
        bp = _head_gram(kdec, [uw[:, :GROUP_W], uw[:, GROUP_W:]])
        bm_ref[0, r, :] = bp[:, :GROUP_W]
        p_ref[0, r, :] = bp[:, GROUP_W:].astype(BF16)
        o0_ref[0, r, :] = ow[:, :GROUP_W]
        qp_ref[0, r, :] = (qs[ci] * egcs[n] - ow[:, GROUP_W:]).astype(BF16)
        d_ref[0, g0 // c + ci:g0 // c + ci + 1, :] = jnp.exp(glast)


def _gdn_prep(qkv, ab, conv_w, a_log, dt_bias, tm):
    b, t, w3 = qkv.shape
    halo = 8
    nck = tm // GDN_C
    prev, cur, nxt = _halo_specs(tm, halo, t, w3)
    pad = lambda a: jnp.pad(a.reshape(1, -1), ((0, 0), (0, AB_PAD - a.size)))
    consts = _gdn_consts()
    tok = pl.BlockSpec((1, tm, GROUP_W), lambda i, j: (i, j, 0))
    dec = pl.BlockSpec((1, nck, GROUP_W), lambda i, j: (i, j, 0))
    tok_shape = lambda dt: jax.ShapeDtypeStruct((b, t, GROUP_W), dt)
    dir_specs = [tok, tok, tok, tok, dec]
    dir_shapes = [tok_shape(BF16), tok_shape(BF16), tok_shape(F32), tok_shape(F32),
                  jax.ShapeDtypeStruct((b, t // GDN_C, GROUP_W), F32)]
    return pl.pallas_call(
        _gdn_prep_kernel,
        grid=(b, t // tm),
        in_specs=[prev, cur, nxt, pl.BlockSpec((1, tm, AB_PAD), lambda i, j: (i, j, 0)),
                  _resident(conv_w.shape), _resident((1, AB_PAD)), _resident((1, AB_PAD))]
        + [_resident(cst.shape) for cst in consts],
        out_specs=dir_specs * 2,
        out_shape=dir_shapes * 2,
        scratch_shapes=[pltpu.VMEM((tm + 2 * halo, w3), F32)] + [pltpu.VMEM((tm, GROUP_W), F32)] * 3,
        compiler_params=_params(2), name="gdn_prep",
    )(qkv, qkv, qkv, ab, conv_w, pad(a_log), pad(dt_bias), *consts)


def _gdn_scan_kernel(pf_ref, qpf_ref, bmf_ref, o0f_ref, df_ref, pb_ref, qpb_ref, bmb_ref, o0b_ref, db_ref,
                     s0f_ref, s0b_ref, of_ref, ob_ref, sf_ref, sb_ref):
    nb, tm = pf_ref.shape[0], pf_ref.shape[1]
    nc = tm // GDN_C

    @pl.when(pl.program_id(0) == 0)
    def _():
        sf_ref[...] = s0f_ref[...]
        sb_ref[...] = s0b_ref[...]

    bdmask = _bd_mask()
    fwd = (pf_ref, qpf_ref, bmf_ref, o0f_ref, df_ref, of_ref)
    bwd = (pb_ref, qpb_ref, bmb_ref, o0b_ref, db_ref, ob_ref)
    chains = [(fwd, bi, False) for bi in range(nb)] + [(bwd, bi, True) for bi in range(nb)]
    states = [sf_ref[bi] for bi in range(nb)] + [sb_ref[bi] for bi in range(nb)]
    for step in range(nc):
        lhs, rows = [], []
        for (p_ref, qp_ref, _, _, _, _), bi, rev in chains:
            ci = nc - 1 - step if rev else step
            r = slice(ci * GDN_C, (ci + 1) * GDN_C)
            rows.append((ci, r))
            lhs.append(jnp.concatenate([p_ref[bi, r, :], qp_ref[bi, r, :]], axis=0))
        res = [_wide_dot(a, [s], bdmask) for a, s in zip(lhs, states)]
        new_states = []
        for (_, _, bm_ref, o0_ref, d_ref, o_ref), bi, _ in chains:
            n = len(new_states)
            ci, r = rows[n]
            o_ref[bi, r, :] = o0_ref[bi, r, :] + res[n][GDN_C:]
            new_states.append(d_ref[bi, ci:ci + 1, :] * states[n] - res[n][:GDN_C] + bm_ref[bi, r, :])
        states = new_states
    for bi in range(nb):
        sf_ref[bi] = states[bi]
        sb_ref[bi] = states[nb + bi]


def _gdn_scan(prep, s0f, s0b, tm):
    b, t, w = prep[0].shape
    nj = t // tm
    nck = tm // GDN_C
    fwd = pl.BlockSpec((b, tm, w), lambda j: (0, j, 0))
    bwd = pl.BlockSpec((b, tm, w), lambda j: (0, nj - 1 - j, 0))
    dfwd = pl.BlockSpec((b, nck, w), lambda j: (0, j, 0))
    dbwd = pl.BlockSpec((b, nck, w), lambda j: (0, nj - 1 - j, 0))
    st = pl.BlockSpec((b, GDN_HEAD_DIM, w), lambda j: (0, 0, 0))
    st_shape = jax.ShapeDtypeStruct((b, GDN_HEAD_DIM, w), F32)
    return pl.pallas_call(
        _gdn_scan_kernel,
        grid=(nj,),
        in_specs=[fwd] * 4 + [dfwd] + [bwd] * 4 + [dbwd] + [st, st],
        out_specs=[fwd, bwd, st, st],
        out_shape=[jax.ShapeDtypeStruct((b, t, w), F32)] * 2 + [st_shape] * 2,
        compiler_params=pltpu.CompilerParams(dimension_semantics=("arbitrary",),
                                             vmem_limit_bytes=VMEM_LIMIT),
        name="gdn_scan",
    )(*prep, s0f, s0b)


def _mix_out_kernel(h_ref, yp_ref, yfa_ref, yfb_ref, yc_ref, of_ref, ob_ref, z_ref, gt_ref, nw_ref, ones_ref,
                    w_ref, o_ref):
    o = of_ref[0] + ob_ref[0]
    ms = _dot01(o * o, ones_ref[...], 2) * (1.0 / GDN_HEAD_DIM)
    z = z_ref[0]
    yg = o * lax.rsqrt(ms + EPS) * nw_ref[...] * (z * jax.nn.sigmoid(z))
    y = jnp.zeros(h_ref.shape[1:], F32)
    yf = jnp.concatenate([yfa_ref[0], yfb_ref[0]], axis=-1)
    for gi, part in enumerate((yp_ref[0], yf, yc_ref[0], yg)):
        y = y + jnp.dot(part.astype(BF16), w_ref[gi * GROUP_W:(gi + 1) * GROUP_W, :],
                        preferred_element_type=F32)
    o_ref[0] = h_ref[0] + gt_ref[0] * y


def _mix_out(h, ys, o_f, o_b, z, gate, gdn_nw, w_out, layer, tm):
    b, t, d = h.shape
    tok = pl.BlockSpec((1, tm, d), lambda i, j: (i, j, 0))
    grp = pl.BlockSpec((1, tm, GROUP_W), lambda i, j: (i, j, 0))
    half = pl.BlockSpec((1, tm, LANES), lambda i, j: (i, j, 0))
    ones = _gdn_consts()[3]
    return pl.pallas_call(
        _mix_out_kernel,
        grid=(b, t // tm),
        in_specs=[tok, grp, half, half] + [grp] * 4 + [pl.BlockSpec((1, 1, d), lambda i, j: (i, 0, 0)),
                                     _resident((1, GROUP_W)), _resident(ones.shape), _layer_resident(w_out, layer)],
        out_specs=tok,
        out_shape=jax.ShapeDtypeStruct(h.shape, F32),
        compiler_params=_params(2), name="mix_out",
    )(h, *ys, o_f, o_b, z, gate, gdn_nw, ones, w_out)


MOD_ROWS = 8
MOD_TN = 1152


def _mod_kernel(c_ref, w_ref, b_ref, o_ref):
    cv = c_ref[...]
    a = (cv * jax.nn.sigmoid(cv)).astype(BF16)
    o_ref[0] = jnp.dot(a, w_ref[0].astype(BF16), preferred_element_type=F32) + b_ref[0]


def _modulation(c, c_ctx, mod_w, mod_b):
    nl, d, n = mod_w.shape
    cond = jnp.concatenate([c, c_ctx[None, :]], axis=0)
    cond = jnp.pad(cond, ((0, MOD_ROWS - cond.shape[0]), (0, 0)))
    return pl.pallas_call(
        _mod_kernel,
        grid=(nl, n // MOD_TN),
        in_specs=[_resident(cond.shape), pl.BlockSpec((1, d, MOD_TN), lambda i, j: (i, 0, j)),
                  pl.BlockSpec((1, 1, MOD_TN), lambda i, j: (i, 0, j))],
        out_specs=pl.BlockSpec((1, MOD_ROWS, MOD_TN), lambda i, j: (i, 0, j)),
        out_shape=jax.ShapeDtypeStruct((nl, MOD_ROWS, n), F32),
        compiler_params=_params(2), name="modulation",
    )(cond, mod_w, mod_b[:, None, :])


def _token_mix(p, gdn_state, grid_rows, tm, wl, need_out):
    p_pool, p_four, p_conv, p_qkv, p_z, p_ab = p
    prep = _gdn_prep(p_qkv, p_ab, wl["gdn_conv_w"], wl["gdn_a_log"], wl["gdn_dt_bias"], tm)
    o_f, o_b, s_f, s_b = _gdn_scan(prep, *gdn_state, tm)
    if not need_out:
        return None, (s_f, s_b)
    ys = (_pool_mix(p_pool, wl["pool_wbd"], wl["pool_scale"], grid_rows),
          *_fourier_mix(p_four, wl["fourier_w"]),
          _conv_mix(p_conv, wl["conv_dw_w"], wl["conv_dw_b"], wl["conv_ln_g"], wl["conv_ln_b"],
                    wl["conv_pw_w"], tm))
    return (ys, o_f, o_b, p_z), (s_f, s_b)


def kernel(x, c, ctx, c_ctx, mod_w, mod_b, norm_w, ffn1_wg, ffn1_wu, ffn1_wd, ffn2_wg, ffn2_wu,
           ffn2_wd, w_in, w_out, pool_w, pool_scale, fourier_w, conv_dw_w, conv_dw_b, conv_ln_g,
           conv_ln_b, conv_pw_w, gdn_conv_w, gdn_a_log, gdn_dt_bias, gdn_norm_w, final_norm_w):
    bsz, seq, d = x.shape
    n_ctx = ctx.shape[1]
    rows = seq // GRID_W
    tm_x, tm_c = 512, n_ctx
    hx, hc = x, ctx
    mods = _modulation(c, c_ctx, mod_w, mod_b)
    zero_state = (jnp.zeros((bsz, GDN_HEAD_DIM, GROUP_W), F32),) * 2
    f1 = tuple(_cast_bf16(w) for w in (ffn1_wg, ffn1_wu, ffn1_wd))
    f2 = tuple(_cast_bf16(w) for w in (ffn2_wg, ffn2_wu, ffn2_wd))
    w_main = _cast_bf16(w_in, cols=2 * GDN_OFF)
    w_ab = jnp.pad(w_in[:, :, 2 * GDN_OFF:], ((0, 0), (0, 0), (0, AB_PAD - N_GATE))).astype(BF16)
    w_out_b = _cast_bf16(w_out)
    for l in range(DEPTH):
        last = l == DEPTH - 1
        mx = [m[:, None, :] for m in jnp.split(mods[l, :bsz], N_MOD, axis=-1)]
        mc = [jnp.broadcast_to(m[None], (bsz, 1, d)) for m in jnp.split(mods[l, bsz:bsz + 1], N_MOD, axis=-1)]
        nw = norm_w[l][:, None, :]
        row = lambda a: a.reshape(1, -1)
        wl = dict(pool_wbd=_blockdiag(pool_w[l]).astype(BF16), pool_scale=row(pool_scale[l]),
                  fourier_w=fourier_w[l].astype(BF16), conv_dw_w=conv_dw_w[l], conv_dw_b=row(conv_dw_b[l]),
                  conv_ln_g=row(conv_ln_g[l]), conv_ln_b=row(conv_ln_b[l]),
                  conv_pw_w=conv_pw_w[l].astype(BF16), gdn_conv_w=gdn_conv_w[l],
                  gdn_a_log=gdn_a_log[l], gdn_dt_bias=gdn_dt_bias[l])
        gdn_nw = row(jnp.tile(gdn_norm_w[l], GDN_HEADS))

        hx = _ffn(hx, nw[0], mx[0], mx[1], mx[2], f1, l, tm_x)
        hc = _ffn(hc, nw[0], mc[0], mc[1], mc[2], f1, l, tm_c)

        px = _inproj(hx, nw[1], mx[3], mx[4], w_main, w_ab, l, False, tm_x)
        pc = _inproj(hc, nw[1], mc[3], mc[4], w_main, w_ab, l, last, tm_c)
        if last:
            pc = (None,) * 3 + tuple(pc)

        mix_c, ctx_state = _token_mix(pc, zero_state, None, tm_c, wl, not last)
        mix_x, _ = _token_mix(px, ctx_state, rows, tm_x, wl, True)
        hx = _mix_out(hx, *mix_x, mx[5], gdn_nw, w_out_b, l, tm_x)
        hx = _ffn(hx, nw[2], mx[6], mx[7], mx[8], f2, l, tm_x, final_norm_w[None, :] if last else None)
        if not last:
            hc = _mix_out(hc, *mix_c, mc[5], gdn_nw, w_out_b, l, tm_c)
            hc = _ffn(hc, nw[2], mc[6], mc[7], mc[8], f2, l, tm_c)
    return hx
```
